```python
import jax, jax.numpy as jnp
from jax import lax
import numpy as np

D_MODEL = 1024
BATCH = 8
SEQ = 4096
DEPTH = 4

N_MIXERS = 3
N_MLA_LAYERS = (DEPTH + 2) // 3
N_DIL_LAYERS = (DEPTH + 1) // 3
N_RWKV_LAYERS = DEPTH // 3

MLA_HEADS = 16
MLA_Q_LORA = 384
MLA_KV_LORA = 256
MLA_NOPE = 64
MLA_ROPE = 32
MLA_V = 64
ROPE_THETA = 10000.0
Q_BLOCK = 128

DIL_GROUPS = ((128, 1), (512, 4), (2048, 16))
DIL_HEADS = 16
DIL_HEAD_DIM = 64
DIL_BLOCK = 128

RWKV_HEAD = 64
RWKV_HEADS = D_MODEL // RWKV_HEAD
RWKV_DECAY_LORA = 64
RWKV_A_LORA = 64
RWKV_GATE_LORA = 160
RWKV_GN_EPS = 64e-5

D_FF = 2752
CONV_WIDTH = 3

ALPHA = (2 * DEPTH) ** 0.25
BETA = (8 * DEPTH) ** -0.25

LN_EPS = 1e-5
RMS_EPS = 1e-6
NEG_INF = -1e30

kernel_name = "hybrid_mla_dilated_rwkv7_convffn_deepnorm"


def layer_norm(x, g, b):
    xf = x.astype(jnp.float32)
    mu = jnp.mean(xf, axis=-1, keepdims=True)
    var = jnp.mean(jnp.square(xf - mu), axis=-1, keepdims=True)
    return ((xf - mu) * lax.rsqrt(var + LN_EPS) * g + b).astype(x.dtype)


def rms_norm(x, g):
    xf = x.astype(jnp.float32)
    y = xf * lax.rsqrt(jnp.mean(jnp.square(xf), axis=-1, keepdims=True) + RMS_EPS)
    return (y * g).astype(x.dtype)


def rope_tables(positions):
    inv_freq = ROPE_THETA ** (-jnp.arange(0, MLA_ROPE, 2, dtype=jnp.float32) / MLA_ROPE)
    ang = positions.astype(jnp.float32)[..., None] * inv_freq
    return jnp.cos(ang), jnp.sin(ang)


def apply_rope(t, cos, sin):
    half = t.shape[-1] // 2
    t1, t2 = t[..., :half], t[..., half:]
    cos = cos.astype(t.dtype)
    sin = sin.astype(t.dtype)
    return jnp.concatenate([t1 * cos - t2 * sin, t2 * cos + t1 * sin], axis=-1)


def causal_block_attention(q, k, v, scale):
    B, S, H, Dk = q.shape
    nb = S // Q_BLOCK
    qb = jnp.moveaxis(q.reshape(B, nb, Q_BLOCK, H, Dk), 1, 0)
    key_pos = jnp.arange(S, dtype=jnp.int32)

    def one_block(args):
        q_blk, i = args
        s = jnp.einsum('bqhd,bkhd->bhqk', q_blk, k, preferred_element_type=jnp.float32) * scale
        q_pos = i * Q_BLOCK + jnp.arange(Q_BLOCK, dtype=jnp.int32)
        s = jnp.where(key_pos[None, :] <= q_pos[:, None], s, NEG_INF)
        p = jax.nn.softmax(s, axis=-1)
        return jnp.einsum('bhqk,bkhd->bqhd', p.astype(v.dtype), v)

    o = lax.map(one_block, (qb, jnp.arange(nb, dtype=jnp.int32)))
    return jnp.moveaxis(o, 0, 1).reshape(B, S, H, v.shape[-1])


def mla_mixer(x, positions, w_down, q_norm, kv_norm, w_uq, w_ukv, w_o):
    B, S, _ = x.shape
    lat = x @ w_down
    c_q = rms_norm(lat[..., :MLA_Q_LORA], q_norm)
    c_kv = rms_norm(lat[..., MLA_Q_LORA:MLA_Q_LORA + MLA_KV_LORA], kv_norm)
    k_pe = lat[..., MLA_Q_LORA + MLA_KV_LORA:]
    q = (c_q @ w_uq).reshape(B, S, MLA_HEADS, MLA_NOPE + MLA_ROPE)
    kv = (c_kv @ w_ukv).reshape(B, S, MLA_HEADS, MLA_NOPE + MLA_V)
    k_nope, v = kv[..., :MLA_NOPE], kv[..., MLA_NOPE:]
    cos, sin = rope_tables(positions)
    q_pe = apply_rope(q[..., MLA_NOPE:], cos[:, :, None], sin[:, :, None])
    k_pe = apply_rope(k_pe, cos, sin)
    q = jnp.concatenate([q[..., :MLA_NOPE], q_pe], axis=-1)
    k = jnp.concatenate([k_nope, jnp.broadcast_to(k_pe[:, :, None], (B, S, MLA_HEADS, MLA_ROPE))], axis=-1)
    o = causal_block_attention(q, k, v, (MLA_NOPE + MLA_ROPE) ** -0.5)
    return o.reshape(B, S, MLA_HEADS * MLA_V) @ w_o


def dilated_group_attention(q, k, v, window, dilation):
    B, S, H, Dh = q.shape
    L = window // dilation
    Bk = DIL_BLOCK
    unit = dilation * Bk
    Sp = -(-S // unit) * unit
    nb = Sp // unit

    def split(t):
        return jnp.pad(t, ((0, 0), (0, Sp - S), (0, 0), (0, 0))).reshape(B, nb, Bk, dilation, H, Dh)

    def with_prev(t):
        prev = jnp.pad(t, ((0, 0), (1, 0), (0, 0), (0, 0), (0, 0), (0, 0)))[:, :-1]
        return jnp.concatenate([prev, t], axis=2)

    qb = split(q)
    kc = with_prev(split(k))
    vc = with_prev(split(v))
    s = jnp.einsum('bnqrhd,bnkrhd->bnrhqk', qb, kc, preferred_element_type=jnp.float32) * Dh ** -0.5
    i = jnp.arange(Bk)[:, None]
    j = jnp.arange(2 * Bk)[None, :]
    dist = i + Bk - j
    blk = jnp.arange(nb)[:, None, None]
    valid = (dist >= 0) & (dist <= L) & ((blk > 0) | (j >= Bk))
    s = jnp.where(valid[None, :, None, None], s, NEG_INF)
    lse = jax.nn.logsumexp(s, axis=-1)
    p = jnp.exp(s - lse[..., None])
    o = jnp.einsum('bnrhqk,bnkrhd->bnqrhd', p.astype(v.dtype), vc)
    o = o.reshape(B, Sp, H, Dh)[:, :S]
    lse = jnp.transpose(lse, (0, 1, 4, 2, 3)).reshape(B, Sp, H)[:, :S]
    return o, lse


def dilated_mixer(x, w_qkv, w_o):
    B, S, _ = x.shape
    qkv = (x @ w_qkv).reshape(B, S, len(DIL_GROUPS), 3, DIL_HEADS, DIL_HEAD_DIM)
    outs, lses = [], []
    for g, (window, dilation) in enumerate(DIL_GROUPS):
        o, lse = dilated_group_attention(qkv[:, :, g, 0], qkv[:, :, g, 1], qkv[:, :, g, 2], window, dilation)
        outs.append(o)
        lses.append(lse)
    wts = jax.nn.softmax(jnp.stack(lses), axis=0)
    o = jnp.einsum('gbsh,gbshd->bshd', wts, jnp.stack(outs).astype(jnp.float32)).astype(x.dtype)
    return o.reshape(B, S, DIL_HEADS * DIL_HEAD_DIM) @ w_o


def rwkv7_mixer(x, mu, w_rkv, w0, w1, w2, a0, a1, a2, g1, g2, k_k, k_a, r_k, ln_w, ln_b, w_o):
    B, T, C = x.shape
    H, N = RWKV_HEADS, RWKV_HEAD
    f32 = jnp.float32
    xx = jnp.pad(x, ((0, 0), (1, 0), (0, 0)))[:, :-1] - x
    xs = x[None] + xx[None] * mu[:, None, None, :]
    rkv = jnp.einsum('jbtc,jcd->jbtd', xs[:3], w_rkv)
    r, k, v = rkv[0], rkv[1], rkv[2]
    w_log = -jax.nn.softplus(-(w0 + jnp.tanh(xs[3] @ w1) @ w2).astype(f32)) - 0.5
    decay = jnp.exp(-jnp.exp(w_log))
    a = jax.nn.sigmoid(a0 + (xs[4] @ a1) @ a2)
    g = jax.nn.sigmoid(xs[5] @ g1) @ g2
    kk = (k * k_k).reshape(B, T, H, N).astype(f32)
    kk = kk / jnp.maximum(jnp.linalg.norm(kk, axis=-1, keepdims=True), 1e-12)
    k = k * (1.0 + (a - 1.0) * k_a)

    def to_time(t):
        return jnp.moveaxis(t.reshape(B, T, H, N).astype(f32), 1, 0)

    def step(S, inp):
        r_t, w_t, k_t, v_t, kk_t, a_t = inp
        s_kk = jnp.einsum('bhij,bhj->bhi', S, kk_t)
        S = S * w_t[:, :, None, :] - s_kk[..., None] * (kk_t * a_t)[:, :, None, :] + v_t[..., None] * k_t[:, :, None, :]
        return S, jnp.einsum('bhij,bhj->bhi', S, r_t)

    S0 = jnp.zeros((B, H, N, N), f32)
    _, out = lax.scan(step, S0, (to_time(r), to_time(decay), to_time(k), to_time(v), jnp.moveaxis(kk, 1, 0), to_time(a)))
    out = jnp.moveaxis(out, 0, 1)
    mean = jnp.mean(out, axis=-1, keepdims=True)
    var = jnp.mean(jnp.square(out - mean), axis=-1, keepdims=True)
    out = ((out - mean) * lax.rsqrt(var + RWKV_GN_EPS)).reshape(B, T, C) * ln_w + ln_b
    rh = r.reshape(B, T, H, N).astype(f32)
    kh = k.reshape(B, T, H, N).astype(f32)
    bonus = jnp.sum(rh * kh * r_k, axis=-1, keepdims=True) * v.reshape(B, T, H, N).astype(f32)
    y = ((out + bonus.reshape(B, T, C)) * g).astype(x.dtype)
    return y @ w_o


def conv_ffn(x, w_in, conv_w, conv_b, w_out):
    h = x @ w_in
    a, b = h[..., :D_FF], h[..., D_FF:]
    a = lax.conv_general_dilated(a, conv_w[:, None, :].astype(a.dtype), window_strides=(1,),
                                 padding=[(CONV_WIDTH - 1, 0)],
                                 dimension_numbers=('NWC', 'WIO', 'NWC'),
                                 feature_group_count=D_FF) + conv_b
    return (jax.nn.silu(a) * b) @ w_out


def setup_inputs(seed: int = 0) -> dict:
    key = jax.random.key(seed)
    ks_all = jax.random.split(key, 48)
    ks = iter([ks_all[i] for i in range(48)])
    f32 = jnp.float32

    def nrm(shape, scale):
        return jax.random.normal(next(ks), shape, f32) * scale

    D = D_MODEL
    NA, NB, NC = N_MLA_LAYERS, N_DIL_LAYERS, N_RWKV_LAYERS
    x = nrm((BATCH, SEQ, D), 1.0)
    offsets = jax.random.randint(next(ks), (BATCH, 1), 0, 1024, dtype=jnp.int32)
    positions = offsets + jnp.arange(SEQ, dtype=jnp.int32)[None, :]
    ln_g = 1.0 + nrm((DEPTH, 2, D), 0.05)
    ln_b = nrm((DEPTH, 2, D), 0.01)
    mla_w_down = nrm((NA, D, MLA_Q_LORA + MLA_KV_LORA + MLA_ROPE), D ** -0.5)
    mla_q_norm = 1.0 + nrm((NA, MLA_Q_LORA), 0.05)
    mla_kv_norm = 1.0 + nrm((NA, MLA_KV_LORA), 0.05)
    mla_w_uq = nrm((NA, MLA_Q_LORA, MLA_HEADS * (MLA_NOPE + MLA_ROPE)), MLA_Q_LORA ** -0.5)
    mla_w_ukv = nrm((NA, MLA_KV_LORA, MLA_HEADS * (MLA_NOPE + MLA_V)), MLA_KV_LORA ** -0.5)
    mla_w_o = nrm((NA, MLA_HEADS * MLA_V, D), (MLA_HEADS * MLA_V) ** -0.5 * BETA)
    dil_w_qkv = nrm((NB, D, len(DIL_GROUPS) * 3 * DIL_HEADS * DIL_HEAD_DIM), D ** -0.5)
    dil_w_o = nrm((NB, DIL_HEADS * DIL_HEAD_DIM, D), (DIL_HEADS * DIL_HEAD_DIM) ** -0.5 * BETA)
    rwkv_mu = jax.random.uniform(next(ks), (NC, 6, D), f32)
    rwkv_w_rkv = nrm((NC, 3, D, D), D ** -0.5)
    rwkv_w0 = jnp.linspace(-6.0, -1.0, D, dtype=f32)[None, :] + nrm((NC, D), 0.1)
    rwkv_w1 = nrm((NC, D, RWKV_DECAY_LORA), D ** -0.5)
    rwkv_w2 = nrm((NC, RWKV_DECAY_LORA, D), 0.1 * RWKV_DECAY_LORA ** -0.5)
    rwkv_a0 = nrm((NC, D), 0.1)
    rwkv_a1 = nrm((NC, D, RWKV_A_LORA), D ** -0.5)
    rwkv_a2 = nrm((NC, RWKV_A_LORA, D), 0.1 * RWKV_A_LORA ** -0.5)
    rwkv_g1 = nrm((NC, D, RWKV_GATE_LORA), D ** -0.5)
    rwkv_g2 = nrm((NC, RWKV_GATE_LORA, D), RWKV_GATE_LORA ** -0.5)
    rwkv_k_k = 0.85 + nrm((NC, D), 0.05)
    rwkv_k_a = 1.0 + nrm((NC, D), 0.05)
    rwkv_r_k = nrm((NC, RWKV_HEADS, RWKV_HEAD), 0.1)
    rwkv_ln_w = 1.0 + nrm((NC, D), 0.05)
    rwkv_ln_b = nrm((NC, D), 0.01)
    rwkv_w_o = nrm((NC, D, D), D ** -0.5 * BETA)
    ffn_w_in = nrm((DEPTH, D, 2 * D_FF), D ** -0.5)
    ffn_conv_w = nrm((DEPTH, CONV_WIDTH, D_FF), CONV_WIDTH ** -0.5)
    ffn_conv_b = nrm((DEPTH, D_FF), 0.01)
    ffn_w_out = nrm((DEPTH, D_FF, D), D_FF ** -0.5 * BETA)
    return {"x": x, "positions": positions, "ln_g": ln_g, "ln_b": ln_b,
            "mla_w_down": mla_w_down, "mla_q_norm": mla_q_norm, "mla_kv_norm": mla_kv_norm,
            "mla_w_uq": mla_w_uq, "mla_w_ukv": mla_w_ukv, "mla_w_o": mla_w_o,
            "dil_w_qkv": dil_w_qkv, "dil_w_o": dil_w_o,
            "rwkv_mu": rwkv_mu, "rwkv_w_rkv": rwkv_w_rkv, "rwkv_w0": rwkv_w0, "rwkv_w1": rwkv_w1,
            "rwkv_w2": rwkv_w2, "rwkv_a0": rwkv_a0, "rwkv_a1": rwkv_a1, "rwkv_a2": rwkv_a2,
            "rwkv_g1": rwkv_g1, "rwkv_g2": rwkv_g2, "rwkv_k_k": rwkv_k_k, "rwkv_k_a": rwkv_k_a,
            "rwkv_r_k": rwkv_r_k, "rwkv_ln_w": rwkv_ln_w, "rwkv_ln_b": rwkv_ln_b, "rwkv_w_o": rwkv_w_o,
            "ffn_w_in": ffn_w_in, "ffn_conv_w": ffn_conv_w, "ffn_conv_b": ffn_conv_b, "ffn_w_out": ffn_w_out}


def reference(x, positions, ln_g, ln_b, mla_w_down, mla_q_norm, mla_kv_norm, mla_w_uq, mla_w_ukv, mla_w_o,
              dil_w_qkv, dil_w_o, rwkv_mu, rwkv_w_rkv, rwkv_w0, rwkv_w1, rwkv_w2, rwkv_a0, rwkv_a1, rwkv_a2,
              rwkv_g1, rwkv_g2, rwkv_k_k, rwkv_k_a, rwkv_r_k, rwkv_ln_w, rwkv_ln_b, rwkv_w_o,
              ffn_w_in, ffn_conv_w, ffn_conv_b, ffn_w_out):
    ia = ib = ic = 0
    for i in range(DEPTH):
        kind = i % N_MIXERS
        if kind == 0:
            h = mla_mixer(x, positions, mla_w_down[ia], mla_q_norm[ia], mla_kv_norm[ia],
                          mla_w_uq[ia], mla_w_ukv[ia], mla_w_o[ia])
            ia += 1
        elif kind == 1:
            h = dilated_mixer(x, dil_w_qkv[ib], dil_w_o[ib])
            ib += 1
        else:
            h = rwkv7_mixer(x, rwkv_mu[ic], rwkv_w_rkv[ic], rwkv_w0[ic], rwkv_w1[ic], rwkv_w2[ic],
                            rwkv_a0[ic], rwkv_a1[ic], rwkv_a2[ic], rwkv_g1[ic], rwkv_g2[ic],
                            rwkv_k_k[ic], rwkv_k_a[ic], rwkv_r_k[ic], rwkv_ln_w[ic], rwkv_ln_b[ic], rwkv_w_o[ic])
            ic += 1
        x = layer_norm(ALPHA * x + h, ln_g[i, 0], ln_b[i, 0])
        h = conv_ffn(x, ffn_w_in[i], ffn_conv_w[i], ffn_conv_b[i], ffn_w_out[i])
        x = layer_norm(ALPHA * x + h, ln_g[i, 1], ln_b[i, 1])
    return x
```

```python
import functools

import jax
import jax.numpy as jnp
from jax import lax
from jax.experimental import pallas as pl
from jax.experimental.pallas import tpu as pltpu

F32 = jnp.float32
BF16 = jnp.bfloat16

D_MODEL = 1024
DEPTH = 4
MLA_HEADS = 16
MLA_Q_LORA = 384
MLA_KV_LORA = 256
MLA_NOPE = 64
MLA_ROPE = 32
MLA_V = 64
ROPE_THETA = 10000.0
DIL_GROUPS = ((128, 1), (512, 4), (2048, 16))
DIL_HEADS = 16
DIL_HEAD_DIM = 64
DIL_BLOCK = 128
RWKV_HEAD = 64
RWKV_GN_EPS = 64e-5
D_FF = 2752
ALPHA = (2 * DEPTH) ** 0.25
LN_EPS = 1e-5
RMS_EPS = 1e-6
NEG_INF = -1e30

LANES = 128
VMEM_LIMIT_BYTES = 56 * 2**20

SLOT = LANES
ROPE_LO = MLA_NOPE
ROPE_HALF = MLA_ROPE // 2
FF_CHUNK = 256
D_FF_PAD = -(-D_FF // FF_CHUNK) * FF_CHUNK
RWKV_CHUNK = 64
RWKV_INV_BASE = 8


def _params(*sem):
    return pltpu.CompilerParams(dimension_semantics=sem, vmem_limit_bytes=VMEM_LIMIT_BYTES)


def _dot(a, b):
    return jnp.dot(a, b, preferred_element_type=F32)


def _dot_nt(a, b):
    return lax.dot_general(a, b, (((1,), (1,)), ((), ())), preferred_element_type=F32)


def _resident(shape):
    nd = len(shape)
    return pl.BlockSpec(shape, lambda *_: (0,) * nd, pipeline_mode=pl.Buffered(1))


def _rows(tm, width):
    return pl.BlockSpec((tm, width), lambda i: (i, 0))


def _post_ln(x, h, g, b):
    y = ALPHA * x + h
    mu = jnp.mean(y, axis=-1, keepdims=True)
    d = y - mu
    var = jnp.mean(d * d, axis=-1, keepdims=True)
    return d * lax.rsqrt(var + LN_EPS) * g + b


def _rms(x, g):
    return x * lax.rsqrt(jnp.mean(x * x, axis=-1, keepdims=True) + RMS_EPS) * g


def _split_bf16(x, parts):
    out = []
    for _ in range(parts - 1):
        hi = x.astype(BF16)
        out.append(hi)
        x = x - hi.astype(F32)
    out.append(x.astype(BF16))
    return out


def _head_sums(x, ones_bd):
    cols = []
    for p in range(x.shape[1] // LANES):
        xs = x[:, p * LANES:(p + 1) * LANES]
        hi, lo = _split_bf16(xs, 2)
        cols.append(_dot(hi, ones_bd) + _dot(lo, ones_bd))
    return jnp.concatenate(cols, axis=1)


def _rope_table_kernel(pos_ref, freq_ref, cos_ref, sin_ref):
    ang = pos_ref[...].astype(F32) * freq_ref[...]
    cos_ref[...] = jnp.cos(ang)
    sin_ref[...] = jnp.sin(ang)


def _rope_tables(positions, tm):
    t = positions.size
    inv_freq = ROPE_THETA ** (-jnp.arange(0, MLA_ROPE, 2, dtype=F32) / MLA_ROPE)
    freq = jnp.zeros((1, SLOT), F32)
    freq = freq.at[0, ROPE_LO:ROPE_LO + ROPE_HALF].set(inv_freq)
    freq = freq.at[0, ROPE_LO + ROPE_HALF:ROPE_LO + MLA_ROPE].set(inv_freq)
    return pl.pallas_call(
        _rope_table_kernel,
        grid=(t // tm,),
        in_specs=[_rows(tm, 1), _resident((1, SLOT))],
        out_specs=[_rows(tm, SLOT), _rows(tm, SLOT)],
        out_shape=[jax.ShapeDtypeStruct((t, SLOT), F32)] * 2,
        compiler_params=_params("parallel"),
        name="rope_tables",
    )(positions.reshape(t, 1), freq)


def _rope_slots(z, cos, sin):
    width = z.shape[1]
    lane = lax.broadcasted_iota(jnp.int32, (1, width), 1) % SLOT
    first = (lane >= ROPE_LO) & (lane < ROPE_LO + ROPE_HALF)
    second = (lane >= ROPE_LO + ROPE_HALF) & (lane < ROPE_LO + MLA_ROPE)
    from_left = pltpu.roll(z, ROPE_HALF, axis=1)
    from_right = pltpu.roll(z, width - ROPE_HALF, axis=1)
    partner = jnp.where(second, from_left, jnp.where(first, -from_right, 0.0))
    return z * cos + partner * sin


def _mla_down_kernel(xb_ref, w_ref, qn_ref, kvn_ref, cos_ref, sin_ref, cq_ref, ckv_ref, kpe_ref):
    lat = _dot(xb_ref[...], w_ref[...])
    cq_ref[...] = _rms(lat[:, :MLA_Q_LORA], qn_ref[...]).astype(BF16)
    ckv_ref[...] = _rms(lat[:, MLA_Q_LORA:MLA_Q_LORA + MLA_KV_LORA], kvn_ref[...]).astype(BF16)
    kpe_ref[...] = _rope_slots(lat[:, MLA_Q_LORA + MLA_KV_LORA:], cos_ref[...], sin_ref[...])


def _mla_q_kernel(cq_ref, w_ref, cos_ref, sin_ref, q_ref, *, scale):
    q = _dot(cq_ref[...], w_ref[...])
    cos = jnp.tile(cos_ref[...], (1, MLA_HEADS))
    sin = jnp.tile(sin_ref[...], (1, MLA_HEADS))
    q_ref[...] = (_rope_slots(q, cos, sin) * scale).astype(BF16)


def _mla_kv_kernel(ckv_ref, w_ref, kpe_ref, k_ref, v_ref):
    kv = _dot(ckv_ref[...], w_ref[...])
    kw = MLA_HEADS * SLOT
    k_ref[...] = (kv[:, :kw] + jnp.tile(kpe_ref[...], (1, MLA_HEADS))).astype(BF16)
    v_ref[...] = kv[:, kw:].astype(BF16)


def _mla_attn_kernel(q_ref, k_ref, v_ref, o_ref, *, tq, tk):
    i = pl.program_id(2)
    row = lax.broadcasted_iota(jnp.int32, (tq, tk), 0)
    col = lax.broadcasted_iota(jnp.int32, (tq, tk), 1)
    lane = lax.broadcasted_iota(jnp.int32, (1, LANES), 1)
    n_full = i * (tq // tk)
    outs = []
    for hh in range(2):
        q = q_ref[:, hh * SLOT:(hh + 1) * SLOT]

        def step(j, carry, masked, hh=hh, q=q):
            m, l, acc = carry
            start = pl.multiple_of(j * tk, tk)
            kb = k_ref[pl.ds(start, tk), hh * SLOT:(hh + 1) * SLOT]
            vb = v_ref[pl.ds(start, tk), :]
            s = _dot_nt(q, kb)
            if masked:
                s = jnp.where(col + (j - n_full) * tk <= row, s, NEG_INF)
            m_new = jnp.maximum(m, jnp.max(s, axis=-1, keepdims=True))
            alpha = jnp.exp(m - m_new)
            p = jnp.exp(s - m_new)
            l = alpha * l + jnp.sum(p, axis=-1, keepdims=True)
            acc = alpha * acc + _dot(p.astype(BF16), vb)
            return m_new, l, acc

        carry = (jnp.full((tq, 1), NEG_INF, F32), jnp.zeros((tq, 1), F32), jnp.zeros((tq, LANES), F32))
        carry = lax.fori_loop(0, n_full, functools.partial(step, masked=False), carry)
        for jj in range(tq // tk):
            carry = step(n_full + jj, carry, masked=True)
        _, l, acc = carry
        outs.append(acc / l)
    o_ref[...] = jnp.where(lane < MLA_V, outs[0], outs[1]).astype(o_ref.dtype)


def _proj_ln_kernel(a_ref, w_ref, x_ref, g_ref, b_ref, xo_ref, xbo_ref):
    y = _post_ln(x_ref[...], _dot(a_ref[...], w_ref[...]), g_ref[...], b_ref[...])
    xo_ref[...] = y
    xbo_ref[...] = y.astype(BF16)


def _proj_ln(a, w, x, g, b, tm):
    t, d = x.shape
    return pl.pallas_call(
        _proj_ln_kernel,
        grid=(t // tm,),
        in_specs=[_rows(tm, a.shape[1]), _resident(w.shape), _rows(tm, d), _resident((1, d)), _resident((1, d))],
        out_specs=[_rows(tm, d), _rows(tm, d)],
        out_shape=[jax.ShapeDtypeStruct((t, d), F32), jax.ShapeDtypeStruct((t, d), BF16)],
        compiler_params=_params("parallel"),
        name="proj_ln",
    )(a, w, x, g.reshape(1, d), b.reshape(1, d))


def _mla_layer(x, xb, cos, sin, batch, w_down, q_norm, kv_norm, w_uq, w_ukv, w_o, ln_g, ln_b, tm):
    t, d = x.shape
    seq = t // batch
    h = MLA_HEADS
    pad = jnp.zeros((d, ROPE_LO), F32)
    w_down_s = jnp.concatenate(
        [w_down[:, :MLA_Q_LORA + MLA_KV_LORA], pad, w_down[:, MLA_Q_LORA + MLA_KV_LORA:],
         jnp.zeros((d, SLOT - ROPE_LO - MLA_ROPE), F32)], axis=1).astype(BF16)
    w_uq_s = jnp.pad(w_uq.reshape(MLA_Q_LORA, h, MLA_NOPE + MLA_ROPE),
                     ((0, 0), (0, 0), (0, SLOT - MLA_NOPE - MLA_ROPE))).reshape(MLA_Q_LORA, h * SLOT).astype(BF16)
    w_ukv3 = w_ukv.reshape(MLA_KV_LORA, h, MLA_NOPE + MLA_V)
    w_k_s = jnp.pad(w_ukv3[:, :, :MLA_NOPE], ((0, 0), (0, 0), (0, SLOT - MLA_NOPE))).reshape(MLA_KV_LORA, h * SLOT)
    w_v_s = w_ukv3[:, :, MLA_NOPE:].reshape(MLA_KV_LORA, h * MLA_V)
    w_kv_s = jnp.concatenate([w_k_s, w_v_s], axis=1).astype(BF16)

    lat_w = MLA_Q_LORA + MLA_KV_LORA + SLOT
    cq, ckv, kpe = pl.pallas_call(
        _mla_down_kernel,
        grid=(t // tm,),
        in_specs=[_rows(tm, d), _resident((d, lat_w)), _resident((1, MLA_Q_LORA)), _resident((1, MLA_KV_LORA)),
                  _rows(tm, SLOT), _rows(tm, SLOT)],
        out_specs=[_rows(tm, MLA_Q_LORA), _rows(tm, MLA_KV_LORA), _rows(tm, SLOT)],
        out_shape=[jax.ShapeDtypeStruct((t, MLA_Q_LORA), BF16), jax.ShapeDtypeStruct((t, MLA_KV_LORA), BF16),
                   jax.ShapeDtypeStruct((t, SLOT), F32)],
        compiler_params=_params("parallel"),
        name="mla_down",
    )(xb, w_down_s, q_norm.reshape(1, -1), kv_norm.reshape(1, -1), cos, sin)

    q = pl.pallas_call(
        functools.partial(_mla_q_kernel, scale=(MLA_NOPE + MLA_ROPE) ** -0.5),
        grid=(t // tm,),
        in_specs=[_rows(tm, MLA_Q_LORA), _resident((MLA_Q_LORA, h * SLOT)), _rows(tm, SLOT), _rows(tm, SLOT)],
        out_specs=_rows(tm, h * SLOT),
        out_shape=jax.ShapeDtypeStruct((t, h * SLOT), BF16),
        compiler_params=_params("parallel"),
        name="mla_q",
    )(cq, w_uq_s, cos, sin)

    k, v = pl.pallas_call(
        _mla_kv_kernel,
        grid=(t // tm,),
        in_specs=[_rows(tm, MLA_KV_LORA), _resident((MLA_KV_LORA, h * SLOT + h * MLA_V)), _rows(tm, SLOT)],
        out_specs=[_rows(tm, h * SLOT), _rows(tm, h * MLA_V)],
        out_shape=[jax.ShapeDtypeStruct((t, h * SLOT), BF16), jax.ShapeDtypeStruct((t, h * MLA_V), BF16)],
        compiler_params=_params("parallel"),
        name="mla_kv",
    )(ckv, w_kv_s, kpe)

    tq = tk = min(512, seq)
    nq = seq // tq
    o = pl.pallas_call(
        functools.partial(_mla_attn_kernel, tq=tq, tk=tk),
        grid=(batch, h // 2, nq),
        in_specs=[pl.BlockSpec((tq, 2 * SLOT), lambda b, p, i: (b * nq + i, p)),
                  pl.BlockSpec((seq, 2 * SLOT), lambda b, p, i: (b, p)),
                  pl.BlockSpec((seq, 2 * MLA_V), lambda b, p, i: (b, p))],
        out_specs=pl.BlockSpec((tq, 2 * MLA_V), lambda b, p, i: (b * nq + i, p)),
        out_shape=jax.ShapeDtypeStruct((t, h * MLA_V), BF16),
        compiler_params=_params("parallel", "parallel", "arbitrary"),
        name="mla_attn",
    )(q, k, v)

    return _proj_ln(o, w_o.astype(BF16), x, ln_g, ln_b, tm)


def _linear_kernel(x_ref, w_ref, o_ref):
    o_ref[...] = _dot(x_ref[...], w_ref[...]).astype(o_ref.dtype)


def _dil_attn_kernel(q_ref, kp_ref, kc_ref, vp_ref, vc_ref, o_ref, lse_ref):
    n = pl.program_id(2)
    bk = DIL_BLOCK
    i = lax.broadcasted_iota(jnp.int32, (bk, 2 * bk), 0)
    j = lax.broadcasted_iota(jnp.int32, (bk, 2 * bk), 1)
    dist = i + bk - j
    valid = (dist >= 0) & (dist <= bk) & ((n > 0) | (j >= bk))
    lo = lax.broadcasted_iota(jnp.int32, (1, LANES), 1) < DIL_HEAD_DIM
    for p in range(DIL_HEADS // 2):
        sl = slice(p * LANES, (p + 1) * LANES)
        q = q_ref[0, :, sl]
        k = jnp.concatenate([kp_ref[0, :, sl], kc_ref[0, :, sl]], axis=0)
        v = jnp.concatenate([vp_ref[0, :, sl], vc_ref[0, :, sl]], axis=0)
        zero = jnp.zeros_like(q)
        res = []
        for qh in (jnp.where(lo, q, zero), jnp.where(lo, zero, q)):
            s = jnp.where(valid, _dot_nt(qh, k), NEG_INF)
            m = jnp.max(s, axis=-1, keepdims=True)
            e = jnp.exp(s - m)
            l = jnp.sum(e, axis=-1, keepdims=True)
            res.append((_dot(e.astype(BF16), v) / l, m + jnp.log(l)))
        o_ref[0, :, sl] = jnp.where(lo, res[0][0], res[1][0])
        lse_ref[0, :, sl] = jnp.where(lo, res[0][1], res[1][1])


def _dil_out_kernel(o1_ref, o2_ref, o3_ref, l1_ref, l2_ref, l3_ref, w_ref, x_ref, g_ref, b_ref, xo_ref, xbo_ref):
    l1, l2, l3 = l1_ref[...], l2_ref[...], l3_ref[...]
    m = jnp.maximum(jnp.maximum(l1, l2), l3)
    e1, e2, e3 = jnp.exp(l1 - m), jnp.exp(l2 - m), jnp.exp(l3 - m)
    o = (e1 * o1_ref[...] + e2 * o2_ref[...] + e3 * o3_ref[...]) / (e1 + e2 + e3)
    y = _post_ln(x_ref[...], _dot(o.astype(BF16), w_ref[...]), g_ref[...], b_ref[...])
    xo_ref[...] = y
    xbo_ref[...] = y.astype(BF16)


def _dil_layer(x, xb, batch, w_qkv, w_o, ln_g, ln_b, tm):
    t, d = x.shape
    seq = t // batch
    ng = len(DIL_GROUPS)
    hd = DIL_HEADS * DIL_HEAD_DIM
    nqkv = ng * 3 * hd
    col_kind = (jnp.arange(nqkv) // hd) % 3
    w_s = (w_qkv * jnp.where(col_kind == 0, DIL_HEAD_DIM ** -0.5, 1.0)[None, :]).astype(BF16)
    tml = min(1024, t)
    qkv = pl.pallas_call(
        _linear_kernel,
        grid=(nqkv // hd, t // tml),
        in_specs=[pl.BlockSpec((tml, d), lambda n, i: (i, 0)), pl.BlockSpec((d, hd), lambda n, i: (0, n))],
        out_specs=pl.BlockSpec((tml, hd), lambda n, i: (i, n)),
        out_shape=jax.ShapeDtypeStruct((t, nqkv), BF16),
        compiler_params=_params("parallel", "parallel"),
        name="dil_qkv",
    )(xb, w_s)

    outs, lses = [], []
    bk = DIL_BLOCK
    for gi, (window, dil) in enumerate(DIL_GROUPS):
        assert window // dil == bk and seq % (dil * bk) == 0
        nb = seq // (dil * bk)
        cols = nqkv // hd
        view = qkv.reshape(batch, seq // dil, dil * nqkv)

        def spec(kind, prev, gi=gi, cols=cols):
            def index(b, r, n):
                return (b, jnp.maximum(n - 1, 0) if prev else n, r * cols + gi * 3 + kind)
            return pl.BlockSpec((1, bk, hd), index)

        out_spec = pl.BlockSpec((1, bk, hd), lambda b, r, n: (b, n, r))
        o, lse = pl.pallas_call(
            _dil_attn_kernel,
            grid=(batch, dil, nb),
            in_specs=[spec(0, False), spec(1, True), spec(1, False), spec(2, True), spec(2, False)],
            out_specs=[out_spec, out_spec],
            out_shape=[jax.ShapeDtypeStruct((batch, seq // dil, dil * hd), F32)] * 2,
            compiler_params=_params("parallel", "parallel", "arbitrary"),
            name=f"dil_attn_d{dil}",
        )(view, view, view, view, view)
        outs.append(o.reshape(t, hd))
        lses.append(lse.reshape(t, hd))

    return pl.pallas_call(
        _dil_out_kernel,
        grid=(t // tm,),
        in_specs=[_rows(tm, hd)] * 6 + [_resident((hd, d)), _rows(tm, d), _resident((1, d)), _resident((1, d))],
        out_specs=[_rows(tm, d), _rows(tm, d)],
        out_shape=[jax.ShapeDtypeStruct((t, d), F32), jax.ShapeDtypeStruct((t, d), BF16)],
        compiler_params=_params("parallel"),
        name="dil_out",
    )(*outs, *lses, w_o.astype(BF16), x, ln_g.reshape(1, d), ln_b.reshape(1, d))


def _rwkv_pre_kernel(x_ref, halo_ref, mu_ref, wrkv_ref, w0_ref, w1_ref, w2_ref, a0_ref, a1_ref, a2_ref,
                     g1_ref, g2_ref, kk_ref, ka_ref, bd_ref,
                     r_o, k_o, v_o, kkn_o, a_o, lw_o, g_o, *, tiles_per_seq):
    first = (pl.program_id(0) % tiles_per_seq) == 0
    x = x_ref[...]
    tm = x.shape[0]
    prev_row = jnp.where(first, 0.0, halo_ref[7:8, :])
    row = lax.broadcasted_iota(jnp.int32, (tm, 1), 0)
    xx = jnp.where(row == 0, prev_row, pltpu.roll(x, 1, axis=0)) - x
    mu = mu_ref[...]

    def mix(j):
        return (x + xx * mu[j:j + 1, :]).astype(BF16)

    r = _dot(mix(0), wrkv_ref[0])
    k = _dot(mix(1), wrkv_ref[1])
    v = _dot(mix(2), wrkv_ref[2])
    z = w0_ref[...] + _dot(jnp.tanh(_dot(mix(3), w1_ref[...])).astype(BF16), w2_ref[...])
    nz = -z
    softplus = jnp.maximum(nz, 0.0) + jnp.log(1.0 + jnp.exp(-jnp.abs(nz)))
    lw = -jnp.exp(-softplus - 0.5)
    a = jax.nn.sigmoid(a0_ref[...] + _dot(_dot(mix(4), a1_ref[...]).astype(BF16), a2_ref[...]))
    g = _dot(jax.nn.sigmoid(_dot(mix(5), g1_ref[...])).astype(BF16), g2_ref[...])
    kk = k * kk_ref[...]
    norm = jnp.sqrt(_head_sums(kk * kk, bd_ref[...]))
    r_o[...] = r
    k_o[...] = k * (1.0 + (a - 1.0) * ka_ref[...])
    v_o[...] = v
    kkn_o[...] = kk / jnp.maximum(norm, 1e-12)
    a_o[...] = a
    lw_o[...] = lw
    g_o[...] = g


def _unit_lower_inverse(lmat, eye, blk):
    def mm(a, b):
        return _dot(a.astype(BF16), b.astype(BF16))

    base = RWKV_INV_BASE
    dg = jnp.where(blk[base], lmat, 0.0)
    d2 = mm(dg, dg)
    d4 = mm(d2, d2)
    tinv = eye + dg
    tinv = tinv + mm(tinv, d2)
    tinv = tinv + mm(tinv, d4)
    s = base
    while s < RWKV_CHUNK:
        off = jnp.where(blk[2 * s] & jnp.logical_not(blk[s]), lmat, 0.0)
        tinv = tinv + mm(mm(tinv, off), tinv)
        s *= 2
    return tinv


def _rwkv_chunk_kernel(r_ref, k_ref, v_ref, kk_ref, a_ref, lw_ref, o_ref, s_ref, *, n_chunks):
    c = RWKV_CHUNK

    @pl.when(pl.program_id(2) == 0)
    def _():
        s_ref[...] = jnp.zeros_like(s_ref)

    ri = lax.broadcasted_iota(jnp.int32, (2 * c, 2 * c), 0)
    ci = lax.broadcasted_iota(jnp.int32, (2 * c, 2 * c), 1)
    blk = {}
    s = RWKV_INV_BASE
    while s <= c:
        blk[s] = (ri // s) == (ci // s)
        s *= 2
    strict = blk[c] & (ci < ri)
    incl = blk[c] & (ci <= ri)
    eye = jnp.where(ri == ci, 1.0, 0.0).astype(F32)
    tri = jnp.where(lax.broadcasted_iota(jnp.int32, (c, c), 1) <= lax.broadcasted_iota(jnp.int32, (c, c), 0),
                    1.0, 0.0).astype(BF16)
    lane_lo = lax.broadcasted_iota(jnp.int32, (1, LANES), 1) < RWKV_HEAD

    def stack(x):
        return jnp.concatenate([jnp.where(lane_lo, x, 0.0), jnp.where(lane_lo, 0.0, x)], axis=0)

    def bf(x):
        return x.astype(BF16)

    for ch in range(n_chunks):
        rows = slice(ch * c, (ch + 1) * c)
        rc, kc, vc = r_ref[rows, :], k_ref[rows, :], v_ref[rows, :]
        kkc, ac, lwc = kk_ref[rows, :], a_ref[rows, :], lw_ref[rows, :]
        cum = sum(_dot(tri, part) for part in _split_bf16(lwc, 3))
        p_incl = jnp.exp(cum)
        p_inv = jnp.exp(-cum)
        at = -kkc * jnp.exp(cum - lwc)
        bt = kkc * ac * p_inv
        kt = kc * p_inv
        rt = rc * p_incl
        p_end = p_incl[c - 1:c, :]

        lhs = bf(jnp.concatenate([stack(at), stack(rt)], axis=0))
        rhs = bf(jnp.concatenate([bt, bt, kt, kt], axis=0))
        prod = _dot_nt(lhs, rhs)
        l_ab = jnp.where(strict, prod[:2 * c, :2 * c], 0.0)
        l_ak = jnp.where(strict, prod[:2 * c, 2 * c:], 0.0)
        m_rb = jnp.where(incl, prod[2 * c:, :2 * c], 0.0)
        m_rk = jnp.where(incl, prod[2 * c:, 2 * c:], 0.0)
        tinv = _unit_lower_inverse(l_ab, eye, blk)

        s0 = s_ref[...]
        from_state = _dot_nt(lhs, bf(s0))
        v_st = bf(stack(vc))
        w_st = from_state[:2 * c] + _dot(bf(l_ak), v_st)
        u_st = _dot(bf(tinv), bf(w_st))
        o_st = from_state[2 * c:] + _dot(bf(m_rb), bf(u_st)) + _dot(bf(m_rk), v_st)
        o_ref[rows, :] = o_st[:c] + o_st[c:]
        u_nat = u_st[:c] + u_st[c:]
        uv = jnp.concatenate([u_nat, vc], axis=0)
        bk_rows = bf(jnp.concatenate([bt, kt], axis=0))
        upd = _dot(bf(uv.T), bk_rows)
        s_ref[...] = (s0 + jnp.where(blk[c], upd, 0.0)) * p_end


def _rwkv_post_kernel(o_ref, r_ref, k_ref, v_ref, g_ref, x_ref, lnw_ref, lnb_ref, rk_ref, bd_ref, w_ref,
                      gl_ref, bl_ref, xo_ref, xbo_ref):
    bd = bd_ref[...]
    o = o_ref[...]
    mean = _head_sums(o, bd) * (1.0 / RWKV_HEAD)
    dlt = o - mean
    var = _head_sums(dlt * dlt, bd) * (1.0 / RWKV_HEAD)
    y = dlt * lax.rsqrt(var + RWKV_GN_EPS) * lnw_ref[...] + lnb_ref[...]
    bonus = _head_sums(r_ref[...] * k_ref[...] * rk_ref[...], bd) * v_ref[...]
    out = ((y + bonus) * g_ref[...]).astype(BF16)
    res = _post_ln(x_ref[...], _dot(out, w_ref[...]), gl_ref[...], bl_ref[...])
    xo_ref[...] = res
    xbo_ref[...] = res.astype(BF16)


def _rwkv_layer(x, batch, mu, w_rkv, w0, w1, w2, a0, a1, a2, g1, g2, k_k, k_a, r_k, ln_w, ln_b, w_o,
                ln_g_post, ln_b_post, tm):
    t, d = x.shape
    seq = t // batch
    tmr = min(256, seq)
    vec = lambda p: p.reshape(1, d)
    idx = jnp.arange(LANES) // RWKV_HEAD
    ones_bd = (idx[:, None] == idx[None, :]).astype(BF16)
    tok = [jax.ShapeDtypeStruct((t, d), F32)] * 7
    halo = pl.BlockSpec((8, d), lambda i: (jnp.maximum(i * (tmr // 8) - 1, 0), 0))
    r, k, v, kkn, a, lw, g = pl.pallas_call(
        functools.partial(_rwkv_pre_kernel, tiles_per_seq=seq // tmr),
        grid=(t // tmr,),
        in_specs=[_rows(tmr, d), halo, _resident((6, d)), _resident((3, d, d)),
                  _resident((1, d)), _resident(w1.shape), _resident(w2.shape),
                  _resident((1, d)), _resident(a1.shape), _resident(a2.shape),
                  _resident(g1.shape), _resident(g2.shape), _resident((1, d)), _resident((1, d)),
                  _resident((LANES, LANES))],
        out_specs=[_rows(tmr, d)] * 7,
        out_shape=tok,
        compiler_params=_params("parallel"),
        name="rwkv_pre",
    )(x, x, mu, w_rkv.astype(BF16), vec(w0), w1.astype(BF16), w2.astype(BF16), vec(a0), a1.astype(BF16),
      a2.astype(BF16), g1.astype(BF16), g2.astype(BF16), vec(k_k), vec(k_a), ones_bd)

    tc = min(256, seq)
    nt = seq // tc
    pair = pl.BlockSpec((tc, LANES), lambda b, p, i: (b * nt + i, p))
    o = pl.pallas_call(
        functools.partial(_rwkv_chunk_kernel, n_chunks=tc // RWKV_CHUNK),
        grid=(batch, d // LANES, nt),
        in_specs=[pair] * 6,
        out_specs=pair,
        out_shape=jax.ShapeDtypeStruct((t, d), F32),
        scratch_shapes=[pltpu.VMEM((LANES, LANES), F32)],
        compiler_params=_params("parallel", "parallel", "arbitrary"),
        name="rwkv_chunk",
    )(r, k, v, kkn, a, lw)

    return pl.pallas_call(
        _rwkv_post_kernel,
        grid=(t // tm,),
        in_specs=[_rows(tm, d)] * 6 + [_resident((1, d))] * 3 + [_resident((LANES, LANES)), _resident((d, d)),
                                                                _resident((1, d)), _resident((1, d))],
        out_specs=[_rows(tm, d), _rows(tm, d)],
        out_shape=[jax.ShapeDtypeStruct((t, d), F32), jax.ShapeDtypeStruct((t, d), BF16)],
        compiler_params=_params("parallel"),
        name="rwkv_post",
    )(o, r, k, v, g, x, vec(ln_w), vec(ln_b), vec(r_k), ones_bd, w_o.astype(BF16), vec(ln_g_post), vec(ln_b_post))


def _ffn_kernel(x_ref, xb_ref, halo_ref, wa_ref, wb_ref, cw_ref, cb_ref, wo_ref, g_ref, b_ref,
                xo_ref, xbo_ref, acc_ref, *, tiles_per_seq, n_chunks, halo_rows):
    first = (pl.program_id(0) % tiles_per_seq) == 0
    xb = xb_ref[...]
    tm = xb.shape[0]
    halo = halo_ref[...]
    halo = jnp.where(first, jnp.zeros_like(halo), halo)
    xe = jnp.concatenate([halo, xb], axis=0)
    acc_ref[...] = jnp.zeros_like(acc_ref)

    def chunk(c, carry):
        ha = _dot(xe, wa_ref[c])
        hb = _dot(xb, wb_ref[c])
        cw = cw_ref[c]
        conv = (cw[0:1, :] * ha[halo_rows - 2:halo_rows - 2 + tm, :]
                + cw[1:2, :] * ha[halo_rows - 1:halo_rows - 1 + tm, :]
                + cw[2:3, :] * ha[halo_rows:, :] + cb_ref[c])
        act = conv * jax.nn.sigmoid(conv) * hb
        acc_ref[...] += _dot(act.astype(BF16), wo_ref[c])
        return carry

    lax.fori_loop(0, n_chunks, chunk, 0)
    y = _post_ln(x_ref[...], acc_ref[...], g_ref[...], b_ref[...])
    xo_ref[...] = y
    xbo_ref[...] = y.astype(BF16)


def _ffn_layer(x, xb, batch, w_in, conv_w, conv_b, w_out, ln_g, ln_b, tm):
    t, d = x.shape
    seq = t // batch
    nch = D_FF_PAD // FF_CHUNK
    padc = D_FF_PAD - D_FF
    chunks = lambda w: jnp.pad(w, ((0, 0), (0, padc))).reshape(w.shape[0], nch, FF_CHUNK).transpose(1, 0, 2)
    wa = chunks(w_in[:, :D_FF]).astype(BF16)
    wb = chunks(w_in[:, D_FF:]).astype(BF16)
    cw = chunks(conv_w)
    cb = chunks(conv_b.reshape(1, D_FF))
    wo = jnp.pad(w_out, ((0, padc), (0, 0))).reshape(nch, FF_CHUNK, d).astype(BF16)
    halo_rows = 16
    halo = pl.BlockSpec((halo_rows, d), lambda i: (jnp.maximum(i * (tm // halo_rows) - 1, 0), 0))
    return pl.pallas_call(
        functools.partial(_ffn_kernel, tiles_per_seq=seq // tm, n_chunks=nch, halo_rows=halo_rows),
        grid=(t // tm,),
        in_specs=[_rows(tm, d), _rows(tm, d), halo, _resident(wa.shape), _resident(wb.shape), _resident(cw.shape),
                  _resident(cb.shape), _resident(wo.shape), _resident((1, d)), _resident((1, d))],
        out_specs=[_rows(tm, d), _rows(tm, d)],
        out_shape=[jax.ShapeDtypeStruct((t, d), F32), jax.ShapeDtypeStruct((t, d), BF16)],
        scratch_shapes=[pltpu.VMEM((tm, d), F32)],
        compiler_params=_params("parallel"),
        name="conv_ffn",
    )(x, xb, xb, wa, wb, cw, cb, wo, ln_g.reshape(1, d), ln_b.reshape(1, d))


def kernel(x, positions, ln_g, ln_b, mla_w_down, mla_q_norm, mla_kv_norm, mla_w_uq, mla_w_ukv, mla_w_o,
           dil_w_qkv, dil_w_o, rwkv_mu, rwkv_w_rkv, rwkv_w0, rwkv_w1, rwkv_w2, rwkv_a0, rwkv_a1, rwkv_a2,
           rwkv_g1, rwkv_g2, rwkv_k_k, rwkv_k_a, rwkv_r_k, rwkv_ln_w, rwkv_ln_b, rwkv_w_o,
           ffn_w_in, ffn_conv_w, ffn_conv_b, ffn_w_out):
    batch, seq, d = x.shape
    t = batch * seq
    tm = min(512, seq)
    xf = x.reshape(t, d)
    xb = xf.astype(BF16)
    cos, sin = _rope_tables(positions, tm)
    ia = ib = ic = 0
    for i in range(DEPTH):
        kind = i % 3
        if kind == 0:
            xf, xb = _mla_layer(xf, xb, cos, sin, batch, mla_w_down[ia], mla_q_norm[ia], mla_kv_norm[ia],
                                mla_w_uq[ia], mla_w_ukv[ia], mla_w_o[ia], ln_g[i, 0], ln_b[i, 0], tm)
            ia += 1
        elif kind == 1:
            xf, xb = _dil_layer(xf, xb, batch, dil_w_qkv[ib], dil_w_o[ib], ln_g[i, 0], ln_b[i, 0], tm)
            ib += 1
        else:
            xf, xb = _rwkv_layer(xf, batch, rwkv_mu[ic], rwkv_w_rkv[ic], rwkv_w0[ic], rwkv_w1[ic], rwkv_w2[ic],
                                 rwkv_a0[ic], rwkv_a1[ic], rwkv_a2[ic], rwkv_g1[ic], rwkv_g2[ic],
                                 rwkv_k_k[ic], rwkv_k_a[ic], rwkv_r_k[ic], rwkv_ln_w[ic], rwkv_ln_b[ic],
                                 rwkv_w_o[ic], ln_g[i, 0], ln_b[i, 0], tm)
            ic += 1
        xf, xb = _ffn_layer(xf, xb, batch, ffn_w_in[i], ffn_conv_w[i], ffn_conv_b[i], ffn_w_out[i],
                            ln_g[i, 1], ln_b[i, 1], tm)
    return xf.reshape(batch, seq, d)
```

```python
import functools

import jax
import jax.numpy as jnp
from jax import lax
from jax.experimental import pallas as pl
from jax.experimental.pallas import tpu as pltpu

F32 = jnp.float32
BF16 = jnp.bfloat16

D_MODEL = 1024
DEPTH = 4
MLA_HEADS = 16
MLA_Q_LORA = 384
MLA_KV_LORA = 256
MLA_NOPE = 64
MLA_ROPE = 32
MLA_V = 64
ROPE_THETA = 10000.0
DIL_GROUPS = ((128, 1), (512, 4), (2048, 16))
DIL_HEADS = 16
DIL_HEAD_DIM = 64
DIL_BLOCK = 128
RWKV_HEAD = 64
RWKV_GN_EPS = 64e-5
D_FF = 2752
ALPHA = (2 * DEPTH) ** 0.25
LN_EPS = 1e-5
RMS_EPS = 1e-6
NEG_INF = -1e30

LANES = 128
VMEM_LIMIT_BYTES = 56 * 2**20

SLOT = LANES
ROPE_LO = MLA_NOPE
ROPE_HALF = MLA_ROPE // 2
FF_CHUNK = 256
D_FF_PAD = -(-D_FF // FF_CHUNK) * FF_CHUNK
RWKV_CHUNK = 64
RWKV_INV_BASE = 8
RWKV_STEP_PAIRS = 4
RWKV_STEP_CHUNKS = 4


def _params(*sem):
    return pltpu.CompilerParams(dimension_semantics=sem, vmem_limit_bytes=VMEM_LIMIT_BYTES)


def _dot(a, b):
    return jnp.dot(a, b, preferred_element_type=F32)


def _dot_nt(a, b):
    return lax.dot_general(a, b, (((1,), (1,)), ((), ())), preferred_element_type=F32)


def _resident(shape):
    nd = len(shape)
    return pl.BlockSpec(shape, lambda *_: (0,) * nd, pipeline_mode=pl.Buffered(1))


def _rows(tm, width):
    return pl.BlockSpec((tm, width), lambda i: (i, 0))


def _post_ln(x, h, g, b):
    y = ALPHA * x + h
    mu = jnp.mean(y, axis=-1, keepdims=True)
    d = y - mu
    var = jnp.mean(d * d, axis=-1, keepdims=True)
    return d * lax.rsqrt(var + LN_EPS) * g + b


def _rms(x, g):
    return x * lax.rsqrt(jnp.mean(x * x, axis=-1, keepdims=True) + RMS_EPS) * g


def _split_bf16(x, parts):
    out = []
    for _ in range(parts - 1):
        hi = x.astype(BF16)
        out.append(hi)
        x = x - hi.astype(F32)
    out.append(x.astype(BF16))
    return out


def _head_sums(x, ones_bd):
    cols = []
    for p in range(x.shape[1] // LANES):
        xs = x[:, p * LANES:(p + 1) * LANES]
        hi, lo = _split_bf16(xs, 2)
        cols.append(_dot(hi, ones_bd) + _dot(lo, ones_bd))
    return jnp.concatenate(cols, axis=1)


def _rope_table_kernel(pos_ref, freq_ref, cos_ref, sin_ref):
    ang = pos_ref[...].astype(F32) * freq_ref[...]
    cos_ref[...] = jnp.cos(ang)
    sin_ref[...] = jnp.sin(ang)


def _rope_tables(positions, tm):
    t = positions.size
    inv_freq = ROPE_THETA ** (-jnp.arange(0, MLA_ROPE, 2, dtype=F32) / MLA_ROPE)
    freq = jnp.zeros((1, SLOT), F32)
    freq = freq.at[0, ROPE_LO:ROPE_LO + ROPE_HALF].set(inv_freq)
    freq = freq.at[0, ROPE_LO + ROPE_HALF:ROPE_LO + MLA_ROPE].set(inv_freq)
    return pl.pallas_call(
        _rope_table_kernel,
        grid=(t // tm,),
        in_specs=[_rows(tm, 1), _resident((1, SLOT))],
        out_specs=[_rows(tm, SLOT), _rows(tm, SLOT)],
        out_shape=[jax.ShapeDtypeStruct((t, SLOT), F32)] * 2,
        compiler_params=_params("parallel"),
        name="rope_tables",
    )(positions.reshape(t, 1), freq)


def _rope_slots(z, cos, sin):
    width = z.shape[1]
    lane = lax.broadcasted_iota(jnp.int32, (1, width), 1) % SLOT
    first = (lane >= ROPE_LO) & (lane < ROPE_LO + ROPE_HALF)
    second = (lane >= ROPE_LO + ROPE_HALF) & (lane < ROPE_LO + MLA_ROPE)
    from_left = pltpu.roll(z, ROPE_HALF, axis=1)
    from_right = pltpu.roll(z, width - ROPE_HALF, axis=1)
    partner = jnp.where(second, from_left, jnp.where(first, -from_right, 0.0))
    return z * cos + partner * sin


def _mla_down_kernel(xb_ref, w_ref, qn_ref, kvn_ref, cos_ref, sin_ref, cq_ref, ckv_ref, kpe_ref):
    lat = _dot(xb_ref[...], w_ref[...])
    cq_ref[...] = _rms(lat[:, :MLA_Q_LORA], qn_ref[...]).astype(BF16)
    ckv_ref[...] = _rms(lat[:, MLA_Q_LORA:MLA_Q_LORA + MLA_KV_LORA], kvn_ref[...]).astype(BF16)
    kpe_ref[...] = _rope_slots(lat[:, MLA_Q_LORA + MLA_KV_LORA:], cos_ref[...], sin_ref[...])


def _mla_q_kernel(cq_ref, w_ref, cos_ref, sin_ref, q_ref, *, scale):
    q = _dot(cq_ref[...], w_ref[...])
    cos = jnp.tile(cos_ref[...], (1, MLA_HEADS))
    sin = jnp.tile(sin_ref[...], (1, MLA_HEADS))
    q_ref[...] = (_rope_slots(q, cos, sin) * scale).astype(BF16)


def _mla_kv_kernel(ckv_ref, w_ref, kpe_ref, k_ref, v_ref):
    kv = _dot(ckv_ref[...], w_ref[...])
    kw = MLA_HEADS * SLOT
    k_ref[...] = (kv[:, :kw] + jnp.tile(kpe_ref[...], (1, MLA_HEADS))).astype(BF16)
    v_ref[...] = kv[:, kw:].astype(BF16)


def _mla_attn_kernel(q_ref, k_ref, v_ref, o_ref, *, tq, tk):
    assert tq == tk
    n_full = pl.program_id(2)
    row = lax.broadcasted_iota(jnp.int32, (tq, tk), 0)
    col = lax.broadcasted_iota(jnp.int32, (tq, tk), 1)
    lane = lax.broadcasted_iota(jnp.int32, (1, LANES), 1)
    heads = (0, 1)

    def scores(j):
        start = pl.multiple_of(j * tk, tk)
        return tuple(_dot_nt(q_ref[:, hh * SLOT:(hh + 1) * SLOT], k_ref[pl.ds(start, tk), hh * SLOT:(hh + 1) * SLOT])
                     for hh in heads)

    def update(j, s, state, masked):
        m, l, acc = state
        start = pl.multiple_of(j * tk, tk)
        if masked:
            s = jnp.where(col <= row, s, NEG_INF)
        m_new = jnp.maximum(m, jnp.max(s, axis=-1, keepdims=True))
        alpha = jnp.exp(m - m_new)
        p = jnp.exp(s - m_new)
        l = alpha * l + jnp.sum(p, axis=-1, keepdims=True)
        acc = alpha * acc + _dot(p.astype(BF16), v_ref[pl.ds(start, tk), :])
        return m_new, l, acc

    def body(j, states):
        return tuple(update(j, s, st, masked=False) for s, st in zip(scores(j), states))

    init = (jnp.full((tq, 1), NEG_INF, F32), jnp.zeros((tq, 1), F32), jnp.zeros((tq, LANES), F32))
    states = lax.fori_loop(0, n_full, body, (init, init))
    outs = []
    for s, st in zip(scores(n_full), states):
        _, l, acc = update(n_full, s, st, masked=True)
        outs.append(acc / l)
    o_ref[...] = jnp.where(lane < MLA_V, outs[0], outs[1]).astype(o_ref.dtype)


def _proj_ln_kernel(a_ref, w_ref, x_ref, g_ref, b_ref, xo_ref, xbo_ref):
    y = _post_ln(x_ref[...], _dot(a_ref[...], w_ref[...]), g_ref[...], b_ref[...])
    xo_ref[...] = y
    xbo_ref[...] = y.astype(BF16)


def _proj_ln(a, w, x, g, b, tm):
    t, d = x.shape
    return pl.pallas_call(
        _proj_ln_kernel,
        grid=(t // tm,),
        in_specs=[_rows(tm, a.shape[1]), _resident(w.shape), _rows(tm, d), _resident((1, d)), _resident((1, d))],
        out_specs=[_rows(tm, d), _rows(tm, d)],
        out_shape=[jax.ShapeDtypeStruct((t, d), F32), jax.ShapeDtypeStruct((t, d), BF16)],
        compiler_params=_params("parallel"),
        name="proj_ln",
    )(a, w, x, g.reshape(1, d), b.reshape(1, d))


def _mla_layer(x, xb, cos, sin, batch, w_down, q_norm, kv_norm, w_uq, w_ukv, w_o, ln_g, ln_b, tm):
    t, d = x.shape
    seq = t // batch
    h = MLA_HEADS
    pad = jnp.zeros((d, ROPE_LO), F32)
    w_down_s = jnp.concatenate(
        [w_down[:, :MLA_Q_LORA + MLA_KV_LORA], pad, w_down[:, MLA_Q_LORA + MLA_KV_LORA:],
         jnp.zeros((d, SLOT - ROPE_LO - MLA_ROPE), F32)], axis=1).astype(BF16)
    w_uq_s = jnp.pad(w_uq.reshape(MLA_Q_LORA, h, MLA_NOPE + MLA_ROPE),
                     ((0, 0), (0, 0), (0, SLOT - MLA_NOPE - MLA_ROPE))).reshape(MLA_Q_LORA, h * SLOT).astype(BF16)
    w_ukv3 = w_ukv.reshape(MLA_KV_LORA, h, MLA_NOPE + MLA_V)
    w_k_s = jnp.pad(w_ukv3[:, :, :MLA_NOPE], ((0, 0), (0, 0), (0, SLOT - MLA_NOPE))).reshape(MLA_KV_LORA, h * SLOT)
    w_v_s = w_ukv3[:, :, MLA_NOPE:].reshape(MLA_KV_LORA, h * MLA_V)
    w_kv_s = jnp.concatenate([w_k_s, w_v_s], axis=1).astype(BF16)

    lat_w = MLA_Q_LORA + MLA_KV_LORA + SLOT
    cq, ckv, kpe = pl.pallas_call(
        _mla_down_kernel,
        grid=(t // tm,),
        in_specs=[_rows(tm, d), _resident((d, lat_w)), _resident((1, MLA_Q_LORA)), _resident((1, MLA_KV_LORA)),
                  _rows(tm, SLOT), _rows(tm, SLOT)],
        out_specs=[_rows(tm, MLA_Q_LORA), _rows(tm, MLA_KV_LORA), _rows(tm, SLOT)],
        out_shape=[jax.ShapeDtypeStruct((t, MLA_Q_LORA), BF16), jax.ShapeDtypeStruct((t, MLA_KV_LORA), BF16),
                   jax.ShapeDtypeStruct((t, SLOT), F32)],
        compiler_params=_params("parallel"),
        name="mla_down",
    )(xb, w_down_s, q_norm.reshape(1, -1), kv_norm.reshape(1, -1), cos, sin)

    q = pl.pallas_call(
        functools.partial(_mla_q_kernel, scale=(MLA_NOPE + MLA_ROPE) ** -0.5),
        grid=(t // tm,),
        in_specs=[_rows(tm, MLA_Q_LORA), _resident((MLA_Q_LORA, h * SLOT)), _rows(tm, SLOT), _rows(tm, SLOT)],
        out_specs=_rows(tm, h * SLOT),
        out_shape=jax.ShapeDtypeStruct((t, h * SLOT), BF16),
        compiler_params=_params("parallel"),
        name="mla_q",
    )(cq, w_uq_s, cos, sin)

    k, v = pl.pallas_call(
        _mla_kv_kernel,
        grid=(t // tm,),
        in_specs=[_rows(tm, MLA_KV_LORA), _resident((MLA_KV_LORA, h * SLOT + h * MLA_V)), _rows(tm, SLOT)],
        out_specs=[_rows(tm, h * SLOT), _rows(tm, h * MLA_V)],
        out_shape=[jax.ShapeDtypeStruct((t, h * SLOT), BF16), jax.ShapeDtypeStruct((t, h * MLA_V), BF16)],
        compiler_params=_params("parallel"),
        name="mla_kv",
    )(ckv, w_kv_s, kpe)

    tq = tk = min(512, seq)
    nq = seq // tq
    o = pl.pallas_call(
        functools.partial(_mla_attn_kernel, tq=tq, tk=tk),
        grid=(batch, h // 2, nq),
        in_specs=[pl.BlockSpec((tq, 2 * SLOT), lambda b, p, i: (b * nq + i, p)),
                  pl.BlockSpec((seq, 2 * SLOT), lambda b, p, i: (b, p)),
                  pl.BlockSpec((seq, 2 * MLA_V), lambda b, p, i: (b, p))],
        out_specs=pl.BlockSpec((tq, 2 * MLA_V), lambda b, p, i: (b * nq + i, p)),
        out_shape=jax.ShapeDtypeStruct((t, h * MLA_V), BF16),
        compiler_params=_params("parallel", "parallel", "arbitrary"),
        name="mla_attn",
    )(q, k, v)

    return _proj_ln(o, w_o.astype(BF16), x, ln_g, ln_b, tm)


def _dil_qkv_kernel(x_ref, w_ref, o_ref, res_ref, *, dil):
    res = _dot(x_ref[...], w_ref[...])
    if dil == 1:
        o_ref[0, 0] = res.astype(o_ref.dtype)
        return
    rows = res.shape[0] // dil
    for p in range(res.shape[1] // LANES):
        sl = slice(p * LANES, (p + 1) * LANES)
        res_ref[p] = res[:, sl]
        for r in range(dil):
            o_ref[0, r, :, sl] = res_ref[p, pl.ds(r, rows, stride=dil), :].astype(o_ref.dtype)


def _dil_attn_kernel(q_ref, kp_ref, kc_ref, vp_ref, vc_ref, o_ref, lse_ref):
    q_ref, kp_ref, kc_ref, vp_ref, vc_ref, o_ref, lse_ref = (
        ref.at[0] for ref in (q_ref, kp_ref, kc_ref, vp_ref, vc_ref, o_ref, lse_ref))
    n = pl.program_id(2)
    bk = DIL_BLOCK
    i = lax.broadcasted_iota(jnp.int32, (bk, 2 * bk), 0)
    j = lax.broadcasted_iota(jnp.int32, (bk, 2 * bk), 1)
    dist = i + bk - j
    valid = (dist >= 0) & (dist <= bk) & ((n > 0) | (j >= bk))
    lo = lax.broadcasted_iota(jnp.int32, (1, LANES), 1) < DIL_HEAD_DIM
    pairs = [slice(p * LANES, (p + 1) * LANES) for p in range(DIL_HEADS // 2)]
    scores = []
    for sl in pairs:
        q = q_ref[0, :, sl]
        k = jnp.concatenate([kp_ref[0, :, sl], kc_ref[0, :, sl]], axis=0)
        zero = jnp.zeros_like(q)
        scores.append([_dot_nt(qh, k) for qh in (jnp.where(lo, q, zero), jnp.where(lo, zero, q))])
    for sl, pair_scores in zip(pairs, scores):
        v = jnp.concatenate([vp_ref[0, :, sl], vc_ref[0, :, sl]], axis=0)
        res = []
        for s in pair_scores:
            s = jnp.where(valid, s, NEG_INF)
            m = jnp.max(s, axis=-1, keepdims=True)
            e = jnp.exp(s - m)
            l = jnp.sum(e, axis=-1, keepdims=True)
            res.append((_dot(e.astype(BF16), v) / l, m + jnp.log(l)))
        o_ref[0, :, sl] = jnp.where(lo, res[0][0], res[1][0])
        lse_ref[0, :, sl] = jnp.where(lo, res[0][1], res[1][1])


def _dil_out_kernel(o1_ref, o2_ref, o3_ref, l1_ref, l2_ref, l3_ref, w_ref, x_ref, g_ref, b_ref, xo_ref, xbo_ref,
                    *bufs):
    bufs = list(bufs)

    def token_order(ref, dil):
        if dil == 1:
            return ref[0, 0]
        buf = bufs.pop()
        for p in range(buf.shape[0]):
            for r in range(dil):
                buf[p, pl.ds(r, ref.shape[2], stride=dil), :] = ref[0, r, :, p * LANES:(p + 1) * LANES]
        return jnp.concatenate([buf[p] for p in range(buf.shape[0])], axis=1)

    dils = [dil for _, dil in DIL_GROUPS]
    l1, l2, l3 = (token_order(ref, dil) for ref, dil in zip((l1_ref, l2_ref, l3_ref), dils))
    o1, o2, o3 = (token_order(ref, dil) for ref, dil in zip((o1_ref, o2_ref, o3_ref), dils))
    m = jnp.maximum(jnp.maximum(l1, l2), l3)
    e1, e2, e3 = jnp.exp(l1 - m), jnp.exp(l2 - m), jnp.exp(l3 - m)
    o = (e1 * o1 + e2 * o2 + e3 * o3) / (e1 + e2 + e3)
    y = _post_ln(x_ref[...], _dot(o.astype(BF16), w_ref[...]), g_ref[...], b_ref[...])
    xo_ref[...] = y
    xbo_ref[...] = y.astype(BF16)


def _dil_layer(x, xb, batch, w_qkv, w_o, ln_g, ln_b, tm):
    t, d = x.shape
    seq = t // batch
    ng = len(DIL_GROUPS)
    hd = DIL_HEADS * DIL_HEAD_DIM
    nqkv = ng * 3 * hd
    col_kind = (jnp.arange(nqkv) // hd) % 3
    w_s = (w_qkv * jnp.where(col_kind == 0, DIL_HEAD_DIM ** -0.5, 1.0)[None, :]).astype(BF16)
    tps = seq // tm
    bk = DIL_BLOCK
    outs, lses = [], []
    for gi, (window, dil) in enumerate(DIL_GROUPS):
        assert window // dil == bk and seq % (dil * bk) == 0 and tm % (16 * dil) == 0
        qkv = pl.pallas_call(
            functools.partial(_dil_qkv_kernel, dil=dil),
            grid=(3, t // tm),
            in_specs=[pl.BlockSpec((tm, d), lambda c, i: (i, 0)),
                      pl.BlockSpec((d, hd), lambda c, i, gi=gi: (0, gi * 3 + c))],
            out_specs=pl.BlockSpec((1, dil, tm // dil, hd), lambda c, i: (i // tps, 0, i % tps, c)),
            out_shape=jax.ShapeDtypeStruct((batch, dil, seq // dil, 3 * hd), BF16),
            scratch_shapes=[pltpu.VMEM((hd // LANES, tm, LANES), F32)],
            compiler_params=_params("parallel", "parallel"),
            name=f"dil_qkv_d{dil}",
        )(xb, w_s)

        def spec(kind, prev):
            def index(b, r, n):
                return (b, r, jnp.maximum(n - 1, 0) if prev else n, kind)
            return pl.BlockSpec((1, 1, bk, hd), index)

        out_spec = pl.BlockSpec((1, 1, bk, hd), lambda b, r, n: (b, r, n, 0))
        o, lse = pl.pallas_call(
            _dil_attn_kernel,
            grid=(batch, dil, seq // (dil * bk)),
            in_specs=[spec(0, False), spec(1, True), spec(1, False), spec(2, True), spec(2, False)],
            out_specs=[out_spec, out_spec],
            out_shape=[jax.ShapeDtypeStruct((batch, dil, seq // dil, hd), F32)] * 2,
            compiler_params=_params("parallel", "parallel", "arbitrary"),
            name=f"dil_attn_d{dil}",
        )(qkv, qkv, qkv, qkv, qkv)
        outs.append(o)
        lses.append(lse)

    res_specs = [pl.BlockSpec((1, dil, tm // dil, hd), lambda i: (i // tps, 0, i % tps, 0)) for _, dil in DIL_GROUPS]
    n_bufs = 2 * sum(dil > 1 for _, dil in DIL_GROUPS)
    return pl.pallas_call(
        _dil_out_kernel,
        grid=(t // tm,),
        in_specs=res_specs * 2 + [_resident((hd, d)), _rows(tm, d), _resident((1, d)), _resident((1, d))],
        out_specs=[_rows(tm, d), _rows(tm, d)],
        out_shape=[jax.ShapeDtypeStruct((t, d), F32), jax.ShapeDtypeStruct((t, d), BF16)],
        scratch_shapes=[pltpu.VMEM((hd // LANES, tm, LANES), F32)] * n_bufs,
        compiler_params=_params("parallel"),
        name="dil_out",
    )(*outs, *lses, w_o.astype(BF16), x, ln_g.reshape(1, d), ln_b.reshape(1, d))


def _rwkv_pre_kernel(x_ref, halo_ref, mu_ref, wrkv_ref, w0_ref, w1_ref, w2_ref, a0_ref, a1_ref, a2_ref,
                     g1_ref, g2_ref, kk_ref, ka_ref, bd_ref,
                     r_o, k_o, v_o, kkn_o, a_o, lw_o, g_o, *, tiles_per_seq):
    first = (pl.program_id(0) % tiles_per_seq) == 0
    x = x_ref[...]
    tm = x.shape[0]
    prev_row = jnp.where(first, 0.0, halo_ref[7:8, :])
    row = lax.broadcasted_iota(jnp.int32, (tm, 1), 0)
    xx = jnp.where(row == 0, prev_row, pltpu.roll(x, 1, axis=0)) - x
    mu = mu_ref[...]

    def mix(j):
        return (x + xx * mu[j:j + 1, :]).astype(BF16)

    r = _dot(mix(0), wrkv_ref[0])
    k = _dot(mix(1), wrkv_ref[1])
    v = _dot(mix(2), wrkv_ref[2])
    z = w0_ref[...] + _dot(jnp.tanh(_dot(mix(3), w1_ref[...])).astype(BF16), w2_ref[...])
    nz = -z
    softplus = jnp.maximum(nz, 0.0) + jnp.log(1.0 + jnp.exp(-jnp.abs(nz)))
    lw = -jnp.exp(-softplus - 0.5)
    a = jax.nn.sigmoid(a0_ref[...] + _dot(_dot(mix(4), a1_ref[...]).astype(BF16), a2_ref[...]))
    g = _dot(jax.nn.sigmoid(_dot(mix(5), g1_ref[...])).astype(BF16), g2_ref[...])
    kk = k * kk_ref[...]
    norm = jnp.sqrt(_head_sums(kk * kk, bd_ref[...]))
    r_o[...] = r
    k_o[...] = k * (1.0 + (a - 1.0) * ka_ref[...])
    v_o[...] = v
    kkn_o[...] = kk / jnp.maximum(norm, 1e-12)
    a_o[...] = a
    lw_o[...] = lw
    g_o[...] = g


def _rwkv_chunk_kernel(r_ref, k_ref, v_ref, kk_ref, a_ref, lw_ref, o_ref, s_ref, *, n_pairs, n_chunks):
    c = RWKV_CHUNK

    @pl.when(pl.program_id(2) == 0)
    def _():
        s_ref[...] = jnp.zeros_like(s_ref)

    ri = lax.broadcasted_iota(jnp.int32, (2 * c, 2 * c), 0)
    ci = lax.broadcasted_iota(jnp.int32, (2 * c, 2 * c), 1)
    blk = {}
    s = RWKV_INV_BASE
    while s <= c:
        blk[s] = (ri // s) == (ci // s)
        s *= 2
    strict = blk[c] & (ci < ri)
    incl = blk[c] & (ci <= ri)
    eye = jnp.where(ri == ci, 1.0, 0.0).astype(F32)
    tri = jnp.where(lax.broadcasted_iota(jnp.int32, (c, c), 1) <= lax.broadcasted_iota(jnp.int32, (c, c), 0),
                    1.0, 0.0).astype(BF16)
    lane_lo = lax.broadcasted_iota(jnp.int32, (1, LANES), 1) < RWKV_HEAD

    def stack(x):
        return jnp.concatenate([jnp.where(lane_lo, x, 0.0), jnp.where(lane_lo, 0.0, x)], axis=0)

    def bf(x):
        return x.astype(BF16)

    streams = [(ch, p) for ch in range(n_chunks) for p in range(n_pairs)]

    def load(ref):
        return [ref[ch * c:(ch + 1) * c, p * LANES:(p + 1) * LANES] for ch, p in streams]

    rc, kc, vc, kkc, ac, lwc = (load(ref) for ref in (r_ref, k_ref, v_ref, kk_ref, a_ref, lw_ref))

    cum = []
    for x in lwc:
        cs = _dot(tri, jnp.concatenate(_split_bf16(x, 3), axis=1))
        cum.append(cs[:, :LANES] + cs[:, LANES:2 * LANES] + cs[:, 2 * LANES:])
    p_incl = [jnp.exp(x) for x in cum]
    p_inv = [jnp.exp(-x) for x in cum]
    p_end = [x[c - 1:c, :] for x in p_incl]
    at = [-kk * jnp.exp(cm - lw) for kk, cm, lw in zip(kkc, cum, lwc)]
    bt = [kk * a * pi for kk, a, pi in zip(kkc, ac, p_inv)]
    kt = [k * pi for k, pi in zip(kc, p_inv)]
    rt = [r * pf for r, pf in zip(rc, p_incl)]
    a_st = [bf(stack(x)) for x in at]
    r_st = [stack(x) for x in rt]
    b_st = [bf(stack(x)) for x in bt]
    k_st = [bf(stack(x)) for x in kt]
    v_st = [bf(stack(x)) for x in vc]

    prod = [_dot_nt(jnp.concatenate([a, bf(r)], axis=0), bf(jnp.concatenate([b, b, k, k], axis=0)))
            for a, r, b, k in zip(a_st, r_st, bt, kt)]
    l_ab = [jnp.where(strict, x[:2 * c, :2 * c], 0.0) for x in prod]
    l_ak = [bf(jnp.where(strict, x[:2 * c, 2 * c:], 0.0)) for x in prod]
    m_rb = [bf(jnp.where(incl, x[2 * c:, :2 * c], 0.0)) for x in prod]
    m_rk = [bf(jnp.where(incl, x[2 * c:, 2 * c:], 0.0)) for x in prod]

    dg = [bf(jnp.where(blk[RWKV_INV_BASE], x, 0.0)) for x in l_ab]
    d2 = [bf(_dot(x, x)) for x in dg]
    d4 = [bf(_dot(x, x)) for x in d2]
    tinv = [eye + x.astype(F32) for x in dg]
    tinv = [t + _dot(bf(t), y) for t, y in zip(tinv, d2)]
    tinv = [t + _dot(bf(t), y) for t, y in zip(tinv, d4)]
    size = RWKV_INV_BASE
    while size < c:
        off = [bf(jnp.where(blk[2 * size] & jnp.logical_not(blk[size]), x, 0.0)) for x in l_ab]
        tb = [bf(t) for t in tinv]
        te = [bf(_dot(t, e)) for t, e in zip(tb, off)]
        tinv = [t + _dot(x, y) for t, x, y in zip(tinv, te, tb)]
        size *= 2
    tb = [bf(t) for t in tinv]

    lakv = [bf(_dot(x, v)) for x, v in zip(l_ak, v_st)]
    gu = [_dot(t, jnp.concatenate([a, y], axis=1)) for t, a, y in zip(tb, a_st, lakv)]
    mg = [_dot(m, bf(x)) for m, x in zip(m_rb, gu)]
    rg = [bf(r + x[:, :LANES]) for r, x in zip(r_st, mg)]
    o_v = [x[:, LANES:] + _dot(m, v) for x, m, v in zip(mg, m_rk, v_st)]
    q_mat = [bf(_dot(bf(x[:, :LANES].T), b) * pe) for x, b, pe in zip(gu, b_st, p_end)]
    n_mat = [(_dot(bf(x[:, LANES:].T), b) + _dot(bf(stack(v).T), k)) * pe
             for x, b, v, k, pe in zip(gu, b_st, vc, k_st, p_end)]

    state = [s_ref[p] for p in range(n_pairs)]
    for i, (ch, p) in enumerate(streams):
        sb = bf(state[p])
        o_st = _dot_nt(rg[i], sb) + o_v[i]
        o_ref[ch * c:(ch + 1) * c, p * LANES:(p + 1) * LANES] = o_st[:c] + o_st[c:]
        state[p] = state[p] * p_end[i] + _dot(sb, q_mat[i]) + n_mat[i]
    for p in range(n_pairs):
        s_ref[p] = state[p]


def _rwkv_post_kernel(o_ref, r_ref, k_ref, v_ref, g_ref, x_ref, lnw_ref, lnb_ref, rk_ref, bd_ref, w_ref,
                      gl_ref, bl_ref, xo_ref, xbo_ref):
    bd = bd_ref[...]
    o = o_ref[...]
    mean = _head_sums(o, bd) * (1.0 / RWKV_HEAD)
    dlt = o - mean
    var = _head_sums(dlt * dlt, bd) * (1.0 / RWKV_HEAD)
    y = dlt * lax.rsqrt(var + RWKV_GN_EPS) * lnw_ref[...] + lnb_ref[...]
    bonus = _head_sums(r_ref[...] * k_ref[...] * rk_ref[...], bd) * v_ref[...]
    out = ((y + bonus) * g_ref[...]).astype(BF16)
    res = _post_ln(x_ref[...], _dot(out, w_ref[...]), gl_ref[...], bl_ref[...])
    xo_ref[...] = res
    xbo_ref[...] = res.astype(BF16)


def _rwkv_layer(x, batch, mu, w_rkv, w0, w1, w2, a0, a1, a2, g1, g2, k_k, k_a, r_k, ln_w, ln_b, w_o,
                ln_g_post, ln_b_post, tm):
    t, d = x.shape
    seq = t // batch
    tmr = min(256, seq)
    vec = lambda p: p.reshape(1, d)
    idx = jnp.arange(LANES) // RWKV_HEAD
    ones_bd = (idx[:, None] == idx[None, :]).astype(BF16)
    tok = [jax.ShapeDtypeStruct((t, d), F32)] * 7
    halo = pl.BlockSpec((8, d), lambda i: (jnp.maximum(i * (tmr // 8) - 1, 0), 0))
    r, k, v, kkn, a, lw, g = pl.pallas_call(
        functools.partial(_rwkv_pre_kernel, tiles_per_seq=seq // tmr),
        grid=(t // tmr,),
        in_specs=[_rows(tmr, d), halo, _resident((6, d)), _resident((3, d, d)),
                  _resident((1, d)), _resident(w1.shape), _resident(w2.shape),
                  _resident((1, d)), _resident(a1.shape), _resident(a2.shape),
                  _resident(g1.shape), _resident(g2.shape), _resident((1, d)), _resident((1, d)),
                  _resident((LANES, LANES))],
        out_specs=[_rows(tmr, d)] * 7,
        out_shape=tok,
        compiler_params=_params("parallel"),
        name="rwkv_pre",
    )(x, x, mu, w_rkv.astype(BF16), vec(w0), w1.astype(BF16), w2.astype(BF16), vec(a0), a1.astype(BF16),
      a2.astype(BF16), g1.astype(BF16), g2.astype(BF16), vec(k_k), vec(k_a), ones_bd)

    tc = min(RWKV_STEP_CHUNKS * RWKV_CHUNK, seq)
    nt = seq // tc
    width = RWKV_STEP_PAIRS * LANES
    pair = pl.BlockSpec((tc, width), lambda b, p, i: (b * nt + i, p))
    o = pl.pallas_call(
        functools.partial(_rwkv_chunk_kernel, n_pairs=RWKV_STEP_PAIRS, n_chunks=tc // RWKV_CHUNK),
        grid=(batch, d // width, nt),
        in_specs=[pair] * 6,
        out_specs=pair,
        out_shape=jax.ShapeDtypeStruct((t, d), F32),
        scratch_shapes=[pltpu.VMEM((RWKV_STEP_PAIRS, LANES, LANES), F32)],
        compiler_params=_params("parallel", "parallel", "arbitrary"),
        name="rwkv_chunk",
    )(r, k, v, kkn, a, lw)

    return pl.pallas_call(
        _rwkv_post_kernel,
        grid=(t // tm,),
        in_specs=[_rows(tm, d)] * 6 + [_resident((1, d))] * 3 + [_resident((LANES, LANES)), _resident((d, d)),
                                                                _resident((1, d)), _resident((1, d))],
        out_specs=[_rows(tm, d), _rows(tm, d)],
        out_shape=[jax.ShapeDtypeStruct((t, d), F32), jax.ShapeDtypeStruct((t, d), BF16)],
        compiler_params=_params("parallel"),
        name="rwkv_post",
    )(o, r, k, v, g, x, vec(ln_w), vec(ln_b), vec(r_k), ones_bd, w_o.astype(BF16), vec(ln_g_post), vec(ln_b_post))


def _ffn_kernel(x_ref, xb_ref, halo_ref, wa_ref, wb_ref, cw_ref, cb_ref, wo_ref, g_ref, b_ref,
                xo_ref, xbo_ref, acc_ref, *, tiles_per_seq, n_chunks, halo_rows):
    first = (pl.program_id(0) % tiles_per_seq) == 0
    xb = xb_ref[...]
    tm = xb.shape[0]
    halo = halo_ref[...]
    halo = jnp.where(first, jnp.zeros_like(halo), halo)
    xe = jnp.concatenate([halo, xb], axis=0)
    acc_ref[...] = jnp.zeros_like(acc_ref)

    def up(c):
        return _dot(xe, wa_ref[c]), _dot(xb, wb_ref[c])

    def down(c, ha, hb):
        cw = cw_ref[c]
        conv = (cw[0:1, :] * ha[halo_rows - 2:halo_rows - 2 + tm, :]
                + cw[1:2, :] * ha[halo_rows - 1:halo_rows - 1 + tm, :]
                + cw[2:3, :] * ha[halo_rows:, :] + cb_ref[c])
        act = conv * jax.nn.sigmoid(conv) * hb
        acc_ref[...] += _dot(act.astype(BF16), wo_ref[c])

    h = up(0)
    for c in range(n_chunks):
        h_next = up(c + 1) if c + 1 < n_chunks else None
        down(c, *h)
        h = h_next
    y = _post_ln(x_ref[...], acc_ref[...], g_ref[...], b_ref[...])
    xo_ref[...] = y
    xbo_ref[...] = y.astype(BF16)


def _ffn_layer(x, xb, batch, w_in, conv_w, conv_b, w_out, ln_g, ln_b, tm):
    t, d = x.shape
    seq = t // batch
    nch = D_FF_PAD // FF_CHUNK
    padc = D_FF_PAD - D_FF
    chunks = lambda w: jnp.pad(w, ((0, 0), (0, padc))).reshape(w.shape[0], nch, FF_CHUNK).transpose(1, 0, 2)
    wa = chunks(w_in[:, :D_FF]).astype(BF16)
    wb = chunks(w_in[:, D_FF:]).astype(BF16)
    cw = chunks(conv_w)
    cb = chunks(conv_b.reshape(1, D_FF))
    wo = jnp.pad(w_out, ((0, padc), (0, 0))).reshape(nch, FF_CHUNK, d).astype(BF16)
    halo_rows = 16
    halo = pl.BlockSpec((halo_rows, d), lambda i: (jnp.maximum(i * (tm // halo_rows) - 1, 0), 0))
    return pl.pallas_call(
        functools.partial(_ffn_kernel, tiles_per_seq=seq // tm, n_chunks=nch, halo_rows=halo_rows),
        grid=(t // tm,),
        in_specs=[_rows(tm, d), _rows(tm, d), halo, _resident(wa.shape), _resident(wb.shape), _resident(cw.shape),
                  _resident(cb.shape), _resident(wo.shape), _resident((1, d)), _resident((1, d))],
        out_specs=[_rows(tm, d), _rows(tm, d)],
        out_shape=[jax.ShapeDtypeStruct((t, d), F32), jax.ShapeDtypeStruct((t, d), BF16)],
        scratch_shapes=[pltpu.VMEM((tm, d), F32)],
        compiler_params=_params("parallel"),
        name="conv_ffn",
    )(x, xb, xb, wa, wb, cw, cb, wo, ln_g.reshape(1, d), ln_b.reshape(1, d))


def kernel(x, positions, ln_g, ln_b, mla_w_down, mla_q_norm, mla_kv_norm, mla_w_uq, mla_w_ukv, mla_w_o,
           dil_w_qkv, dil_w_o, rwkv_mu, rwkv_w_rkv, rwkv_w0, rwkv_w1, rwkv_w2, rwkv_a0, rwkv_a1, rwkv_a2,
           rwkv_g1, rwkv_g2, rwkv_k_k, rwkv_k_a, rwkv_r_k, rwkv_ln_w, rwkv_ln_b, rwkv_w_o,
           ffn_w_in, ffn_conv_w, ffn_conv_b, ffn_w_out):
    batch, seq, d = x.shape
    t = batch * seq
    tm = min(512, seq)
    xf = x.reshape(t, d)
    xb = xf.astype(BF16)
    cos, sin = _rope_tables(positions, tm)
    ia = ib = ic = 0
    for i in range(DEPTH):
        kind = i % 3
        if kind == 0:
            xf, xb = _mla_layer(xf, xb, cos, sin, batch, mla_w_down[ia], mla_q_norm[ia], mla_kv_norm[ia],
                                mla_w_uq[ia], mla_w_ukv[ia], mla_w_o[ia], ln_g[i, 0], ln_b[i, 0], tm)
            ia += 1
        elif kind == 1:
            xf, xb = _dil_layer(xf, xb, batch, dil_w_qkv[ib], dil_w_o[ib], ln_g[i, 0], ln_b[i, 0], tm)
            ib += 1
        else:
            xf, xb = _rwkv_layer(xf, batch, rwkv_mu[ic], rwkv_w_rkv[ic], rwkv_w0[ic], rwkv_w1[ic], rwkv_w2[ic],
                                 rwkv_a0[ic], rwkv_a1[ic], rwkv_a2[ic], rwkv_g1[ic], rwkv_g2[ic],
                                 rwkv_k_k[ic], rwkv_k_a[ic], rwkv_r_k[ic], rwkv_ln_w[ic], rwkv_ln_b[ic],
                                 rwkv_w_o[ic], ln_g[i, 0], ln_b[i, 0], tm)
            ic += 1
        xf, xb = _ffn_layer(xf, xb, batch, ffn_w_in[i], ffn_conv_w[i], ffn_conv_b[i], ffn_w_out[i],
                            ln_g[i, 1], ln_b[i, 1], tm)
    return xf.reshape(batch, seq, d)
```

```python
import functools

import jax
import jax.numpy as jnp
from jax import lax
from jax.experimental import pallas as pl
from jax.experimental.pallas import tpu as pltpu

F32 = jnp.float32
BF16 = jnp.bfloat16

D_MODEL = 1024
DEPTH = 4
MLA_HEADS = 16
MLA_Q_LORA = 384
MLA_KV_LORA = 256
MLA_NOPE = 64
MLA_ROPE = 32
MLA_V = 64
ROPE_THETA = 10000.0
DIL_GROUPS = ((128, 1), (512, 4), (2048, 16))
DIL_HEADS = 16
DIL_HEAD_DIM = 64
DIL_BLOCK = 128
RWKV_HEAD = 64
RWKV_GN_EPS = 64e-5
D_FF = 2752
ALPHA = (2 * DEPTH) ** 0.25
LN_EPS = 1e-5
RMS_EPS = 1e-6
NEG_INF = -1e30
LOG2_E = 1.4426950408889634

LANES = 128
VMEM_LIMIT_BYTES = 56 * 2**20

SLOT = LANES
ROPE_LO = MLA_NOPE
ROPE_HALF = MLA_ROPE // 2
FF_CHUNK = 256
D_FF_PAD = -(-D_FF // FF_CHUNK) * FF_CHUNK
RWKV_CHUNK = 64
RWKV_INV_BASE = 8
RWKV_STEP_PAIRS = 4
RWKV_STEP_CHUNKS = 4


def _params(*sem):
    return pltpu.CompilerParams(dimension_semantics=sem, vmem_limit_bytes=VMEM_LIMIT_BYTES)


def _dot(a, b):
    return jnp.dot(a, b, preferred_element_type=F32)


def _dot_nt(a, b):
    return lax.dot_general(a, b, (((1,), (1,)), ((), ())), preferred_element_type=F32)


def _resident(shape):
    nd = len(shape)
    return pl.BlockSpec(shape, lambda *_: (0,) * nd, pipeline_mode=pl.Buffered(1))


def _rows(tm, width):
    return pl.BlockSpec((tm, width), lambda i: (i, 0))


def _post_ln(x, h, g, b):
    y = ALPHA * x + h
    mu = jnp.mean(y, axis=-1, keepdims=True)
    d = y - mu
    var = jnp.mean(d * d, axis=-1, keepdims=True)
    return d * lax.rsqrt(var + LN_EPS) * g + b


def _rms(x, g):
    return x * lax.rsqrt(jnp.mean(x * x, axis=-1, keepdims=True) + RMS_EPS) * g


def _split_bf16(x, parts):
    out = []
    for _ in range(parts - 1):
        hi = x.astype(BF16)
        out.append(hi)
        x = x - hi.astype(F32)
    out.append(x.astype(BF16))
    return out


def _head_sums(x, ones_bd):
    cols = []
    for p in range(x.shape[1] // LANES):
        xs = x[:, p * LANES:(p + 1) * LANES]
        hi, lo = _split_bf16(xs, 2)
        cols.append(_dot(hi, ones_bd) + _dot(lo, ones_bd))
    return jnp.concatenate(cols, axis=1)


def _rope_table_kernel(pos_ref, freq_ref, cos_ref, sin_ref):
    ang = pos_ref[...].astype(F32) * freq_ref[...]
    cos_ref[...] = jnp.cos(ang)
    sin_ref[...] = jnp.sin(ang)


def _rope_tables(positions, tm):
    t = positions.size
    inv_freq = ROPE_THETA ** (-jnp.arange(0, MLA_ROPE, 2, dtype=F32) / MLA_ROPE)
    freq = jnp.zeros((1, SLOT), F32)
    freq = freq.at[0, ROPE_LO:ROPE_LO + ROPE_HALF].set(inv_freq)
    freq = freq.at[0, ROPE_LO + ROPE_HALF:ROPE_LO + MLA_ROPE].set(inv_freq)
    return pl.pallas_call(
        _rope_table_kernel,
        grid=(t // tm,),
        in_specs=[_rows(tm, 1), _resident((1, SLOT))],
        out_specs=[_rows(tm, SLOT), _rows(tm, SLOT)],
        out_shape=[jax.ShapeDtypeStruct((t, SLOT), F32)] * 2,
        compiler_params=_params("parallel"),
        name="rope_tables",
    )(positions.reshape(t, 1), freq)


def _rope_slots(z, cos, sin):
    width = z.shape[1]
    lane = lax.broadcasted_iota(jnp.int32, (1, width), 1) % SLOT
    first = (lane >= ROPE_LO) & (lane < ROPE_LO + ROPE_HALF)
    second = (lane >= ROPE_LO + ROPE_HALF) & (lane < ROPE_LO + MLA_ROPE)
    from_left = pltpu.roll(z, ROPE_HALF, axis=1)
    from_right = pltpu.roll(z, width - ROPE_HALF, axis=1)
    partner = jnp.where(second, from_left, jnp.where(first, -from_right, 0.0))
    return z * cos + partner * sin


def _mla_down_kernel(xb_ref, w_ref, qn_ref, kvn_ref, cos_ref, sin_ref, cq_ref, ckv_ref, kpe_ref):
    lat = _dot(xb_ref[...], w_ref[...])
    cq_ref[...] = _rms(lat[:, :MLA_Q_LORA], qn_ref[...]).astype(BF16)
    ckv_ref[...] = _rms(lat[:, MLA_Q_LORA:MLA_Q_LORA + MLA_KV_LORA], kvn_ref[...]).astype(BF16)
    kpe_ref[...] = _rope_slots(lat[:, MLA_Q_LORA + MLA_KV_LORA:], cos_ref[...], sin_ref[...])


def _mla_q_kernel(cq_ref, w_ref, cos_ref, sin_ref, q_ref, *, scale):
    q = _dot(cq_ref[...], w_ref[...])
    cos = jnp.tile(cos_ref[...], (1, MLA_HEADS))
    sin = jnp.tile(sin_ref[...], (1, MLA_HEADS))
    q_ref[...] = (_rope_slots(q, cos, sin) * scale).astype(BF16)


def _mla_kv_kernel(ckv_ref, w_ref, kpe_ref, k_ref, vt_ref):
    kv = _dot(ckv_ref[...], w_ref[...])
    kw = MLA_HEADS * SLOT
    k_ref[...] = (kv[:, :kw] + jnp.tile(kpe_ref[...], (1, MLA_HEADS))).astype(BF16)
    vt_ref[0] = kv[:, kw:].T.astype(BF16)


def _mla_attn_kernel(q_ref, k_ref, vt_ref, o_ref, *, tq, tk):
    assert tq == tk
    n_full = pl.program_id(2)
    key = lax.broadcasted_iota(jnp.int32, (tk, tq), 0)
    qry = lax.broadcasted_iota(jnp.int32, (tk, tq), 1)
    sub = lax.broadcasted_iota(jnp.int32, (LANES, tk), 0)
    own_rows = (sub < MLA_V, sub >= MLA_V)
    heads = (0, 1)

    def scores(j):
        start = pl.multiple_of(j * tk, tk)
        return tuple(_dot_nt(k_ref[pl.ds(start, tk), hh * SLOT:(hh + 1) * SLOT], q_ref[:, hh * SLOT:(hh + 1) * SLOT])
                     for hh in heads)

    def update(j, s, state, hh, masked):
        m, acc = state
        if masked:
            s = jnp.where(key <= qry, s, NEG_INF)
        m_new = jnp.maximum(m, jnp.max(s, axis=0, keepdims=True))
        alpha = jnp.exp2(m - m_new)
        p = jnp.exp2(s - m_new).astype(BF16)
        vt = jnp.where(own_rows[hh], vt_ref[j], jnp.ones((LANES, tk), BF16))
        return m_new, alpha * acc + _dot(vt, p)

    def body(j, states):
        return tuple(update(j, s, st, hh, masked=False) for hh, s, st in zip(heads, scores(j), states))

    def body2(j2, states):
        s_even, s_odd = scores(2 * j2), scores(2 * j2 + 1)
        states = tuple(update(2 * j2, s, st, hh, masked=False) for hh, s, st in zip(heads, s_even, states))
        return tuple(update(2 * j2 + 1, s, st, hh, masked=False) for hh, s, st in zip(heads, s_odd, states))

    init = (jnp.full((1, tq), NEG_INF, F32), jnp.zeros((LANES, tq), F32))
    states = lax.fori_loop(0, n_full // 2, body2, (init, init))
    states = lax.fori_loop(n_full - n_full % 2, n_full, body, states)
    (_, acc0), (_, acc1) = (update(n_full, s, st, hh, masked=True)
                            for hh, s, st in zip(heads, scores(n_full), states))
    out_t = jnp.where(own_rows[0], acc0 / acc0[MLA_V:MLA_V + 1, :], acc1 / acc1[0:1, :])
    o_ref[...] = out_t.T.astype(o_ref.dtype)


def _proj_ln_kernel(a_ref, w_ref, x_ref, g_ref, b_ref, xo_ref, xbo_ref):
    y = _post_ln(x_ref[...], _dot(a_ref[...], w_ref[...]), g_ref[...], b_ref[...])
    xo_ref[...] = y
    xbo_ref[...] = y.astype(BF16)


def _proj_ln(a, w, x, g, b, tm):
    t, d = x.shape
    return pl.pallas_call(
        _proj_ln_kernel,
        grid=(t // tm,),
        in_specs=[_rows(tm, a.shape[1]), _resident(w.shape), _rows(tm, d), _resident((1, d)), _resident((1, d))],
        out_specs=[_rows(tm, d), _rows(tm, d)],
        out_shape=[jax.ShapeDtypeStruct((t, d), F32), jax.ShapeDtypeStruct((t, d), BF16)],
        compiler_params=_params("parallel"),
        name="proj_ln",
    )(a, w, x, g.reshape(1, d), b.reshape(1, d))


def _mla_layer(x, xb, cos, sin, batch, w_down, q_norm, kv_norm, w_uq, w_ukv, w_o, ln_g, ln_b, tm):
    t, d = x.shape
    seq = t // batch
    h = MLA_HEADS
    pad = jnp.zeros((d, ROPE_LO), F32)
    w_down_s = jnp.concatenate(
        [w_down[:, :MLA_Q_LORA + MLA_KV_LORA], pad, w_down[:, MLA_Q_LORA + MLA_KV_LORA:],
         jnp.zeros((d, SLOT - ROPE_LO - MLA_ROPE), F32)], axis=1).astype(BF16)
    w_uq_s = jnp.pad(w_uq.reshape(MLA_Q_LORA, h, MLA_NOPE + MLA_ROPE),
                     ((0, 0), (0, 0), (0, SLOT - MLA_NOPE - MLA_ROPE))).reshape(MLA_Q_LORA, h * SLOT).astype(BF16)
    w_ukv3 = w_ukv.reshape(MLA_KV_LORA, h, MLA_NOPE + MLA_V)
    w_k_s = jnp.pad(w_ukv3[:, :, :MLA_NOPE], ((0, 0), (0, 0), (0, SLOT - MLA_NOPE))).reshape(MLA_KV_LORA, h * SLOT)
    w_v_s = w_ukv3[:, :, MLA_NOPE:].reshape(MLA_KV_LORA, h * MLA_V)
    w_kv_s = jnp.concatenate([w_k_s, w_v_s], axis=1).astype(BF16)

    lat_w = MLA_Q_LORA + MLA_KV_LORA + SLOT
    cq, ckv, kpe = pl.pallas_call(
        _mla_down_kernel,
        grid=(t // tm,),
        in_specs=[_rows(tm, d), _resident((d, lat_w)), _resident((1, MLA_Q_LORA)), _resident((1, MLA_KV_LORA)),
                  _rows(tm, SLOT), _rows(tm, SLOT)],
        out_specs=[_rows(tm, MLA_Q_LORA), _rows(tm, MLA_KV_LORA), _rows(tm, SLOT)],
        out_shape=[jax.ShapeDtypeStruct((t, MLA_Q_LORA), BF16), jax.ShapeDtypeStruct((t, MLA_KV_LORA), BF16),
                   jax.ShapeDtypeStruct((t, SLOT), F32)],
        compiler_params=_params("parallel"),
        name="mla_down",
    )(xb, w_down_s, q_norm.reshape(1, -1), kv_norm.reshape(1, -1), cos, sin)

    q = pl.pallas_call(
        functools.partial(_mla_q_kernel, scale=(MLA_NOPE + MLA_ROPE) ** -0.5 * LOG2_E),
        grid=(t // tm,),
        in_specs=[_rows(tm, MLA_Q_LORA), _resident((MLA_Q_LORA, h * SLOT)), _rows(tm, SLOT), _rows(tm, SLOT)],
        out_specs=_rows(tm, h * SLOT),
        out_shape=jax.ShapeDtypeStruct((t, h * SLOT), BF16),
        compiler_params=_params("parallel"),
        name="mla_q",
    )(cq, w_uq_s, cos, sin)

    tq = tk = tm
    nq = seq // tq
    k, vt = pl.pallas_call(
        _mla_kv_kernel,
        grid=(t // tm,),
        in_specs=[_rows(tm, MLA_KV_LORA), _resident((MLA_KV_LORA, h * SLOT + h * MLA_V)), _rows(tm, SLOT)],
        out_specs=[_rows(tm, h * SLOT), pl.BlockSpec((1, h * MLA_V, tm), lambda i: (i, 0, 0))],
        out_shape=[jax.ShapeDtypeStruct((t, h * SLOT), BF16), jax.ShapeDtypeStruct((t // tm, h * MLA_V, tm), BF16)],
        compiler_params=_params("parallel"),
        name="mla_kv",
    )(ckv, w_kv_s, kpe)

    o = pl.pallas_call(
        functools.partial(_mla_attn_kernel, tq=tq, tk=tk),
        grid=(batch, h // 2, nq),
        in_specs=[pl.BlockSpec((tq, 2 * SLOT), lambda b, p, i: (b * nq + i, p)),
                  pl.BlockSpec((seq, 2 * SLOT), lambda b, p, i: (b, p)),
                  pl.BlockSpec((nq, 2 * MLA_V, tk), lambda b, p, i: (b, p, 0))],
        out_specs=pl.BlockSpec((tq, 2 * MLA_V), lambda b, p, i: (b * nq + i, p)),
        out_shape=jax.ShapeDtypeStruct((t, h * MLA_V), BF16),
        compiler_params=_params("parallel", "parallel", "arbitrary"),
        name="mla_attn",
    )(q, k, vt)

    return _proj_ln(o, w_o.astype(BF16), x, ln_g, ln_b, tm)


def _dil_qkv_kernel(x_ref, w_ref, o_ref, res_ref, *, dil):
    res = _dot(x_ref[...], w_ref[...])
    if dil == 1:
        o_ref[0, 0] = res.astype(o_ref.dtype)
        return
    rows = res.shape[0] // dil
    for p in range(res.shape[1] // LANES):
        sl = slice(p * LANES, (p + 1) * LANES)
        res_ref[p] = res[:, sl]
        for r in range(dil):
            o_ref[0, r, :, sl] = res_ref[p, pl.ds(r, rows, stride=dil), :].astype(o_ref.dtype)


def _dil_attn_kernel(q_ref, kp_ref, kc_ref, vp_ref, vc_ref, o_ref, lse_ref):
    q_ref, kp_ref, kc_ref, vp_ref, vc_ref, o_ref, lse_ref = (
        ref.at[0] for ref in (q_ref, kp_ref, kc_ref, vp_ref, vc_ref, o_ref, lse_ref))
    n = pl.program_id(2)
    bk = DIL_BLOCK
    i = lax.broadcasted_iota(jnp.int32, (bk, 2 * bk), 0)
    j = lax.broadcasted_iota(jnp.int32, (bk, 2 * bk), 1)
    dist = i + bk - j
    valid = (dist >= 0) & (dist <= bk) & ((n > 0) | (j >= bk))
    lo = lax.broadcasted_iota(jnp.int32, (1, LANES), 1) < DIL_HEAD_DIM
    pairs = [slice(p * LANES, (p + 1) * LANES) for p in range(DIL_HEADS // 2)]
    scores = []
    for sl in pairs:
        q = q_ref[0, :, sl]
        k = jnp.concatenate([kp_ref[0, :, sl], kc_ref[0, :, sl]], axis=0)
        zero = jnp.zeros_like(q)
        scores.append([_dot_nt(qh, k) for qh in (jnp.where(lo, q, zero), jnp.where(lo, zero, q))])
    for sl, pair_scores in zip(pairs, scores):
        v = jnp.concatenate([vp_ref[0, :, sl], vc_ref[0, :, sl]], axis=0)
        res = []
        for s in pair_scores:
            s = jnp.where(valid, s, NEG_INF)
            m = jnp.max(s, axis=-1, keepdims=True)
            e = jnp.exp(s - m)
            l = jnp.sum(e, axis=-1, keepdims=True)
            res.append((_dot(e.astype(BF16), v) / l, m + jnp.log(l)))
        o_ref[0, :, sl] = jnp.where(lo, res[0][0], res[1][0]).astype(o_ref.dtype)
        lse_ref[0, :, sl] = jnp.where(lo, res[0][1], res[1][1])


def _dil_out_kernel(o1_ref, o2_ref, o3_ref, l1_ref, l2_ref, l3_ref, w_ref, x_ref, g_ref, b_ref, xo_ref, xbo_ref,
                    *bufs):
    bufs = list(bufs)

    def token_order(ref, dil):
        if dil == 1:
            return ref[0, 0]
        buf = bufs.pop()
        for p in range(buf.shape[0]):
            for r in range(dil):
                buf[p, pl.ds(r, ref.shape[2], stride=dil), :] = ref[0, r, :, p * LANES:(p + 1) * LANES].astype(F32)
        return jnp.concatenate([buf[p] for p in range(buf.shape[0])], axis=1)

    dils = [dil for _, dil in DIL_GROUPS]
    l1, l2, l3 = (token_order(ref, dil) for ref, dil in zip((l1_ref, l2_ref, l3_ref), dils))
    o1, o2, o3 = (token_order(ref, dil) for ref, dil in zip((o1_ref, o2_ref, o3_ref), dils))
    m = jnp.maximum(jnp.maximum(l1, l2), l3)
    e1, e2, e3 = jnp.exp(l1 - m), jnp.exp(l2 - m), jnp.exp(l3 - m)
    o = (e1 * o1 + e2 * o2 + e3 * o3) / (e1 + e2 + e3)
    y = _post_ln(x_ref[...], _dot(o.astype(BF16), w_ref[...]), g_ref[...], b_ref[...])
    xo_ref[...] = y
    xbo_ref[...] = y.astype(BF16)


def _dil_layer(x, xb, batch, w_qkv, w_o, ln_g, ln_b, tm):
    t, d = x.shape
    seq = t // batch
    ng = len(DIL_GROUPS)
    hd = DIL_HEADS * DIL_HEAD_DIM
    nqkv = ng * 3 * hd
    col_kind = (jnp.arange(nqkv) // hd) % 3
    w_s = (w_qkv * jnp.where(col_kind == 0, DIL_HEAD_DIM ** -0.5, 1.0)[None, :]).astype(BF16)
    tps = seq // tm
    bk = DIL_BLOCK
    outs, lses = [], []
    for gi, (window, dil) in enumerate(DIL_GROUPS):
        assert window // dil == bk and seq % (dil * bk) == 0 and tm % (16 * dil) == 0
        qkv = pl.pallas_call(
            functools.partial(_dil_qkv_kernel, dil=dil),
            grid=(3, t // tm),
            in_specs=[pl.BlockSpec((tm, d), lambda c, i: (i, 0)),
                      pl.BlockSpec((d, hd), lambda c, i, gi=gi: (0, gi * 3 + c))],
            out_specs=pl.BlockSpec((1, dil, tm // dil, hd), lambda c, i: (i // tps, 0, i % tps, c)),
            out_shape=jax.ShapeDtypeStruct((batch, dil, seq // dil, 3 * hd), BF16),
            scratch_shapes=[pltpu.VMEM((hd // LANES, tm, LANES), F32)],
            compiler_params=_params("parallel", "parallel"),
            name=f"dil_qkv_d{dil}",
        )(xb, w_s)

        def spec(kind, prev):
            def index(b, r, n):
                return (b, r, jnp.maximum(n - 1, 0) if prev else n, kind)
            return pl.BlockSpec((1, 1, bk, hd), index)

        out_spec = pl.BlockSpec((1, 1, bk, hd), lambda b, r, n: (b, r, n, 0))
        o, lse = pl.pallas_call(
            _dil_attn_kernel,
            grid=(batch, dil, seq // (dil * bk)),
            in_specs=[spec(0, False), spec(1, True), spec(1, False), spec(2, True), spec(2, False)],
            out_specs=[out_spec, out_spec],
            out_shape=[jax.ShapeDtypeStruct((batch, dil, seq // dil, hd), BF16),
                       jax.ShapeDtypeStruct((batch, dil, seq // dil, hd), F32)],
            compiler_params=_params("parallel", "parallel", "arbitrary"),
            name=f"dil_attn_d{dil}",
        )(qkv, qkv, qkv, qkv, qkv)
        outs.append(o)
        lses.append(lse)

    res_specs = [pl.BlockSpec((1, dil, tm // dil, hd), lambda i: (i // tps, 0, i % tps, 0)) for _, dil in DIL_GROUPS]
    n_bufs = 2 * sum(dil > 1 for _, dil in DIL_GROUPS)
    return pl.pallas_call(
        _dil_out_kernel,
        grid=(t // tm,),
        in_specs=res_specs * 2 + [_resident((hd, d)), _rows(tm, d), _resident((1, d)), _resident((1, d))],
        out_specs=[_rows(tm, d), _rows(tm, d)],
        out_shape=[jax.ShapeDtypeStruct((t, d), F32), jax.ShapeDtypeStruct((t, d), BF16)],
        scratch_shapes=[pltpu.VMEM((hd // LANES, tm, LANES), F32)] * n_bufs,
        compiler_params=_params("parallel"),
        name="dil_out",
    )(*outs, *lses, w_o.astype(BF16), x, ln_g.reshape(1, d), ln_b.reshape(1, d))


def _rwkv_pre_kernel(x_ref, halo_ref, mu_ref, wrkv_ref, w0_ref, w1_ref, w2_ref, a0_ref, a1_ref, a2_ref,
                     g1_ref, g2_ref, kk_ref, ka_ref, bd_ref,
                     r_o, k_o, v_o, kkn_o, a_o, lw_o, g_o, *, tiles_per_seq):
    first = (pl.program_id(0) % tiles_per_seq) == 0
    x = x_ref[...]
    tm = x.shape[0]
    prev_row = jnp.where(first, 0.0, halo_ref[7:8, :])
    row = lax.broadcasted_iota(jnp.int32, (tm, 1), 0)
    xx = jnp.where(row == 0, prev_row, pltpu.roll(x, 1, axis=0)) - x
    mu = mu_ref[...]

    def mix(j):
        return (x + xx * mu[j:j + 1, :]).astype(BF16)

    r = _dot(mix(0), wrkv_ref[0])
    k = _dot(mix(1), wrkv_ref[1])
    v = _dot(mix(2), wrkv_ref[2])
    z = w0_ref[...] + _dot(jnp.tanh(_dot(mix(3), w1_ref[...])).astype(BF16), w2_ref[...])
    nz = -z
    softplus = jnp.maximum(nz, 0.0) + jnp.log(1.0 + jnp.exp(-jnp.abs(nz)))
    lw = -jnp.exp(-softplus - 0.5)
    a = jax.nn.sigmoid(a0_ref[...] + _dot(_dot(mix(4), a1_ref[...]).astype(BF16), a2_ref[...]))
    g = _dot(jax.nn.sigmoid(_dot(mix(5), g1_ref[...])).astype(BF16), g2_ref[...])
    kk = k * kk_ref[...]
    norm = jnp.sqrt(_head_sums(kk * kk, bd_ref[...]))
    r_o[...] = r
    k_o[...] = k * (1.0 + (a - 1.0) * ka_ref[...])
    v_o[...] = v.astype(v_o.dtype)
    kkn_o[...] = kk / jnp.maximum(norm, 1e-12)
    a_o[...] = a
    lw_o[...] = lw
    g_o[...] = g.astype(g_o.dtype)


def _rwkv_chunk_kernel(r_ref, k_ref, v_ref, kk_ref, a_ref, lw_ref, o_ref, s_ref, *, n_pairs, n_chunks):
    c = RWKV_CHUNK

    @pl.when(pl.program_id(2) == 0)
    def _():
        s_ref[...] = jnp.zeros_like(s_ref)

    ri = lax.broadcasted_iota(jnp.int32, (2 * c, 2 * c), 0)
    ci = lax.broadcasted_iota(jnp.int32, (2 * c, 2 * c), 1)
    blk = {}
    s = RWKV_INV_BASE
    while s <= c:
        blk[s] = (ri // s) == (ci // s)
        s *= 2
    strict = blk[c] & (ci < ri)
    incl = blk[c] & (ci <= ri)
    eye = jnp.where(ri == ci, 1.0, 0.0).astype(F32)
    tri = jnp.where(lax.broadcasted_iota(jnp.int32, (c, c), 1) <= lax.broadcasted_iota(jnp.int32, (c, c), 0),
                    1.0, 0.0).astype(BF16)
    lane_lo = lax.broadcasted_iota(jnp.int32, (1, LANES), 1) < RWKV_HEAD

    def stack(x):
        return jnp.concatenate([jnp.where(lane_lo, x, 0.0), jnp.where(lane_lo, 0.0, x)], axis=0)

    def bf(x):
        return x.astype(BF16)

    streams = [(ch, p) for ch in range(n_chunks) for p in range(n_pairs)]

    def load(ref):
        return [ref[ch * c:(ch + 1) * c, p * LANES:(p + 1) * LANES] for ch, p in streams]

    rc, kc, vc, kkc, ac, lwc = (load(ref) for ref in (r_ref, k_ref, v_ref, kk_ref, a_ref, lw_ref))
    vc = [x.astype(F32) for x in vc]

    cum = []
    for x in lwc:
        cs = _dot(tri, jnp.concatenate(_split_bf16(x, 3), axis=1))
        cum.append(cs[:, :LANES] + cs[:, LANES:2 * LANES] + cs[:, 2 * LANES:])
    p_incl = [jnp.exp(x) for x in cum]
    p_inv = [jnp.exp(-x) for x in cum]
    p_end = [x[c - 1:c, :] for x in p_incl]
    at = [-kk * jnp.exp(cm - lw) for kk, cm, lw in zip(kkc, cum, lwc)]
    bt = [kk * a * pi for kk, a, pi in zip(kkc, ac, p_inv)]
    kt = [k * pi for k, pi in zip(kc, p_inv)]
    rt = [r * pf for r, pf in zip(rc, p_incl)]
    a_st = [bf(stack(x)) for x in at]
    r_st = [stack(x) for x in rt]
    b_st = [bf(stack(x)) for x in bt]
    k_st = [bf(stack(x)) for x in kt]
    v_st = [bf(stack(x)) for x in vc]

    prod = [_dot_nt(jnp.concatenate([a, bf(r)], axis=0), bf(jnp.concatenate([b, b, k, k], axis=0)))
            for a, r, b, k in zip(a_st, r_st, bt, kt)]
    l_ab = [jnp.where(strict, x[:2 * c, :2 * c], 0.0) for x in prod]
    l_ak = [bf(jnp.where(strict, x[:2 * c, 2 * c:], 0.0)) for x in prod]
    m_rb = [bf(jnp.where(incl, x[2 * c:, :2 * c], 0.0)) for x in prod]
    m_rk = [bf(jnp.where(incl, x[2 * c:, 2 * c:], 0.0)) for x in prod]

    dg = [bf(jnp.where(blk[RWKV_INV_BASE], x, 0.0)) for x in l_ab]
    d2 = [bf(_dot(x, x)) for x in dg]
    d4 = [bf(_dot(x, x)) for x in d2]
    tinv = [eye + x.astype(F32) for x in dg]
    tinv = [t + _dot(bf(t), y) for t, y in zip(tinv, d2)]
    tinv = [t + _dot(bf(t), y) for t, y in zip(tinv, d4)]
    size = RWKV_INV_BASE
    while size < c:
        off = [bf(jnp.where(blk[2 * size] & jnp.logical_not(blk[size]), x, 0.0)) for x in l_ab]
        tb = [bf(t) for t in tinv]
        te = [bf(_dot(t, e)) for t, e in zip(tb, off)]
        tinv = [t + _dot(x, y) for t, x, y in zip(tinv, te, tb)]
        size *= 2
    tb = [bf(t) for t in tinv]

    lakv = [bf(_dot(x, v)) for x, v in zip(l_ak, v_st)]
    gu = [_dot(t, jnp.concatenate([a, y], axis=1)) for t, a, y in zip(tb, a_st, lakv)]
    mg = [_dot(m, bf(x)) for m, x in zip(m_rb, gu)]
    rg = [bf(r + x[:, :LANES]) for r, x in zip(r_st, mg)]
    o_v = [x[:, LANES:] + _dot(m, v) for x, m, v in zip(mg, m_rk, v_st)]
    q_mat = [bf(_dot(bf(x[:, :LANES].T), b) * pe) for x, b, pe in zip(gu, b_st, p_end)]
    n_mat = [(_dot(bf(x[:, LANES:].T), b) + _dot(bf(stack(v).T), k)) * pe
             for x, b, v, k, pe in zip(gu, b_st, vc, k_st, p_end)]

    state = [s_ref[p] for p in range(n_pairs)]
    for i, (ch, p) in enumerate(streams):
        sb = bf(state[p])
        o_st = _dot_nt(rg[i], sb) + o_v[i]
        o_ref[ch * c:(ch + 1) * c, p * LANES:(p + 1) * LANES] = o_st[:c] + o_st[c:]
        state[p] = state[p] * p_end[i] + _dot(sb, q_mat[i]) + n_mat[i]
    for p in range(n_pairs):
        s_ref[p] = state[p]


def _rwkv_post_kernel(o_ref, r_ref, k_ref, v_ref, g_ref, x_ref, lnw_ref, lnb_ref, rk_ref, bd_ref, w_ref,
                      gl_ref, bl_ref, xo_ref, xbo_ref):
    bd = bd_ref[...]
    o = o_ref[...]
    mean = _head_sums(o, bd) * (1.0 / RWKV_HEAD)
    dlt = o - mean
    var = _head_sums(dlt * dlt, bd) * (1.0 / RWKV_HEAD)
    y = dlt * lax.rsqrt(var + RWKV_GN_EPS) * lnw_ref[...] + lnb_ref[...]
    bonus = _head_sums(r_ref[...] * k_ref[...] * rk_ref[...], bd) * v_ref[...]
    out = ((y + bonus) * g_ref[...]).astype(BF16)
    res = _post_ln(x_ref[...], _dot(out, w_ref[...]), gl_ref[...], bl_ref[...])
    xo_ref[...] = res
    xbo_ref[...] = res.astype(BF16)


def _rwkv_layer(x, batch, mu, w_rkv, w0, w1, w2, a0, a1, a2, g1, g2, k_k, k_a, r_k, ln_w, ln_b, w_o,
                ln_g_post, ln_b_post, tm):
    t, d = x.shape
    seq = t // batch
    tmr = min(256, seq)
    vec = lambda p: p.reshape(1, d)
    idx = jnp.arange(LANES) // RWKV_HEAD
    ones_bd = (idx[:, None] == idx[None, :]).astype(BF16)
    tok = [jax.ShapeDtypeStruct((t, d), dt) for dt in (F32, F32, BF16, F32, F32, F32, BF16)]
    halo = pl.BlockSpec((8, d), lambda i: (jnp.maximum(i * (tmr // 8) - 1, 0), 0))
    r, k, v, kkn, a, lw, g = pl.pallas_call(
        functools.partial(_rwkv_pre_kernel, tiles_per_seq=seq // tmr),
        grid=(t // tmr,),
        in_specs=[_rows(tmr, d), halo, _resident((6, d)), _resident((3, d, d)),
                  _resident((1, d)), _resident(w1.shape), _resident(w2.shape),
                  _resident((1, d)), _resident(a1.shape), _resident(a2.shape),
                  _resident(g1.shape), _resident(g2.shape), _resident((1, d)), _resident((1, d)),
                  _resident((LANES, LANES))],
        out_specs=[_rows(tmr, d)] * 7,
        out_shape=tok,
        compiler_params=_params("parallel"),
        name="rwkv_pre",
    )(x, x, mu, w_rkv.astype(BF16), vec(w0), w1.astype(BF16), w2.astype(BF16), vec(a0), a1.astype(BF16),
      a2.astype(BF16), g1.astype(BF16), g2.astype(BF16), vec(k_k), vec(k_a), ones_bd)

    tc = min(RWKV_STEP_CHUNKS * RWKV_CHUNK, seq)
    nt = seq // tc
    width = RWKV_STEP_PAIRS * LANES
    pair = pl.BlockSpec((tc, width), lambda b, p, i: (b * nt + i, p))
    o = pl.pallas_call(
        functools.partial(_rwkv_chunk_kernel, n_pairs=RWKV_STEP_PAIRS, n_chunks=tc // RWKV_CHUNK),
        grid=(batch, d // width, nt),
        in_specs=[pair] * 6,
        out_specs=pair,
        out_shape=jax.ShapeDtypeStruct((t, d), F32),
        scratch_shapes=[pltpu.VMEM((RWKV_STEP_PAIRS, LANES, LANES), F32)],
        compiler_params=_params("parallel", "parallel", "arbitrary"),
        name="rwkv_chunk",
    )(r, k, v, kkn, a, lw)

    return pl.pallas_call(
        _rwkv_post_kernel,
        grid=(t // tm,),
        in_specs=[_rows(tm, d)] * 6 + [_resident((1, d))] * 3 + [_resident((LANES, LANES)), _resident((d, d)),
                                                                _resident((1, d)), _resident((1, d))],
        out_specs=[_rows(tm, d), _rows(tm, d)],
        out_shape=[jax.ShapeDtypeStruct((t, d), F32), jax.ShapeDtypeStruct((t, d), BF16)],
        compiler_params=_params("parallel"),
        name="rwkv_post",
    )(o, r, k, v, g, x, vec(ln_w), vec(ln_b), vec(r_k), ones_bd, w_o.astype(BF16), vec(ln_g_post), vec(ln_b_post))


def _ffn_kernel(x_ref, xb_ref, halo_ref, wa_ref, wb_ref, cw_ref, cb_ref, wo_ref, g_ref, b_ref,
                xo_ref, xbo_ref, acc_ref, *, tiles_per_seq, n_chunks, halo_rows):
    first = (pl.program_id(0) % tiles_per_seq) == 0
    xb = xb_ref[...]
    tm = xb.shape[0]
    halo = halo_ref[...]
    halo = jnp.where(first, jnp.zeros_like(halo), halo)
    xe = jnp.concatenate([halo, xb], axis=0)
    acc_ref[...] = jnp.zeros_like(acc_ref)

    def up(c):
        return _dot(xe, wa_ref[c]), _dot(xb, wb_ref[c])

    def down(c, ha, hb):
        cw = cw_ref[c]
        conv = (cw[0:1, :] * ha[halo_rows - 2:halo_rows - 2 + tm, :]
                + cw[1:2, :] * ha[halo_rows - 1:halo_rows - 1 + tm, :]
                + cw[2:3, :] * ha[halo_rows:, :] + cb_ref[c])
        act = conv * jax.nn.sigmoid(conv) * hb
        acc_ref[...] += _dot(act.astype(BF16), wo_ref[c])

    h = up(0)
    for c in range(n_chunks):
        h_next = up(c + 1) if c + 1 < n_chunks else None
        down(c, *h)
        h = h_next
    y = _post_ln(x_ref[...], acc_ref[...], g_ref[...], b_ref[...])
    xo_ref[...] = y
    xbo_ref[...] = y.astype(BF16)


def _ffn_layer(x, xb, batch, w_in, conv_w, conv_b, w_out, ln_g, ln_b, tm):
    t, d = x.shape
    seq = t // batch
    nch = D_FF_PAD // FF_CHUNK
    padc = D_FF_PAD - D_FF
    chunks = lambda w: jnp.pad(w, ((0, 0), (0, padc))).reshape(w.shape[0], nch, FF_CHUNK).transpose(1, 0, 2)
    wa = chunks(w_in[:, :D_FF]).astype(BF16)
    wb = chunks(w_in[:, D_FF:]).astype(BF16)
    cw = chunks(conv_w)
    cb = chunks(conv_b.reshape(1, D_FF))
    wo = jnp.pad(w_out, ((0, padc), (0, 0))).reshape(nch, FF_CHUNK, d).astype(BF16)
    halo_rows = 16
    halo = pl.BlockSpec((halo_rows, d), lambda i: (jnp.maximum(i * (tm // halo_rows) - 1, 0), 0))
    return pl.pallas_call(
        functools.partial(_ffn_kernel, tiles_per_seq=seq // tm, n_chunks=nch, halo_rows=halo_rows),
        grid=(t // tm,),
        in_specs=[_rows(tm, d), _rows(tm, d), halo, _resident(wa.shape), _resident(wb.shape), _resident(cw.shape),
                  _resident(cb.shape), _resident(wo.shape), _resident((1, d)), _resident((1, d))],
        out_specs=[_rows(tm, d), _rows(tm, d)],
        out_shape=[jax.ShapeDtypeStruct((t, d), F32), jax.ShapeDtypeStruct((t, d), BF16)],
        scratch_shapes=[pltpu.VMEM((tm, d), F32)],
        compiler_params=_params("parallel"),
        name="conv_ffn",
    )(x, xb, xb, wa, wb, cw, cb, wo, ln_g.reshape(1, d), ln_b.reshape(1, d))


def kernel(x, positions, ln_g, ln_b, mla_w_down, mla_q_norm, mla_kv_norm, mla_w_uq, mla_w_ukv, mla_w_o,
           dil_w_qkv, dil_w_o, rwkv_mu, rwkv_w_rkv, rwkv_w0, rwkv_w1, rwkv_w2, rwkv_a0, rwkv_a1, rwkv_a2,
           rwkv_g1, rwkv_g2, rwkv_k_k, rwkv_k_a, rwkv_r_k, rwkv_ln_w, rwkv_ln_b, rwkv_w_o,
           ffn_w_in, ffn_conv_w, ffn_conv_b, ffn_w_out):
    batch, seq, d = x.shape
    t = batch * seq
    tm = min(512, seq)
    xf = x.reshape(t, d)
    xb = xf.astype(BF16)
    cos, sin = _rope_tables(positions, tm)
    ia = ib = ic = 0
    for i in range(DEPTH):
        kind = i % 3
        if kind == 0:
            xf, xb = _mla_layer(xf, xb, cos, sin, batch, mla_w_down[ia], mla_q_norm[ia], mla_kv_norm[ia],
                                mla_w_uq[ia], mla_w_ukv[ia], mla_w_o[ia], ln_g[i, 0], ln_b[i, 0], tm)
            ia += 1
        elif kind == 1:
            xf, xb = _dil_layer(xf, xb, batch, dil_w_qkv[ib], dil_w_o[ib], ln_g[i, 0], ln_b[i, 0], tm)
            ib += 1
        else:
            xf, xb = _rwkv_layer(xf, batch, rwkv_mu[ic], rwkv_w_rkv[ic], rwkv_w0[ic], rwkv_w1[ic], rwkv_w2[ic],
                                 rwkv_a0[ic], rwkv_a1[ic], rwkv_a2[ic], rwkv_g1[ic], rwkv_g2[ic],
                                 rwkv_k_k[ic], rwkv_k_a[ic], rwkv_r_k[ic], rwkv_ln_w[ic], rwkv_ln_b[ic],
                                 rwkv_w_o[ic], ln_g[i, 0], ln_b[i, 0], tm)
            ic += 1
        xf, xb = _ffn_layer(xf, xb, batch, ffn_w_in[i], ffn_conv_w[i], ffn_conv_b[i], ffn_w_out[i],
                            ln_g[i, 1], ln_b[i, 1], tm)
    return xf.reshape(batch, seq, d)
```

```python
import functools

import jax
import jax.numpy as jnp
from jax import lax
from jax.experimental import pallas as pl
from jax.experimental.pallas import tpu as pltpu

F32 = jnp.float32
BF16 = jnp.bfloat16

D_MODEL = 1024
DEPTH = 4
MLA_HEADS = 16
MLA_Q_LORA = 384
MLA_KV_LORA = 256
MLA_NOPE = 64
MLA_ROPE = 32
MLA_V = 64
ROPE_THETA = 10000.0
DIL_GROUPS = ((128, 1), (512, 4), (2048, 16))
DIL_HEADS = 16
DIL_HEAD_DIM = 64
DIL_BLOCK = 128
DIL_STEP_BLOCKS = 2
DIL_LSE_LANES = 128 // DIL_HEADS
RWKV_HEAD = 64
RWKV_GN_EPS = 64e-5
D_FF = 2752
ALPHA = (2 * DEPTH) ** 0.25
LN_EPS = 1e-5
RMS_EPS = 1e-6
NEG_INF = -1e30
LOG2_E = 1.4426950408889634

LANES = 128
VMEM_LIMIT_BYTES = 56 * 2**20

SLOT = LANES
ROPE_LO = MLA_NOPE
ROPE_HALF = MLA_ROPE // 2
FF_CHUNK = 256
D_FF_PAD = -(-D_FF // FF_CHUNK) * FF_CHUNK
RWKV_CHUNK = 64
RWKV_INV_BASE = 8
RWKV_STEP_PAIRS = 4
RWKV_STEP_CHUNKS = 4


def _params(*sem):
    return pltpu.CompilerParams(dimension_semantics=sem, vmem_limit_bytes=VMEM_LIMIT_BYTES)


def _dot(a, b):
    return jnp.dot(a, b, preferred_element_type=F32)


def _dot_nt(a, b):
    return lax.dot_general(a, b, (((1,), (1,)), ((), ())), preferred_element_type=F32)


def _resident(shape):
    nd = len(shape)
    return pl.BlockSpec(shape, lambda *_: (0,) * nd, pipeline_mode=pl.Buffered(1))


def _rows(tm, width):
    return pl.BlockSpec((tm, width), lambda i: (i, 0))


def _post_ln(x, h, g, b):
    y = ALPHA * x + h
    mu = jnp.mean(y, axis=-1, keepdims=True)
    d = y - mu
    var = jnp.mean(d * d, axis=-1, keepdims=True)
    return d * lax.rsqrt(var + LN_EPS) * g + b


def _rms(x, g):
    return x * lax.rsqrt(jnp.mean(x * x, axis=-1, keepdims=True) + RMS_EPS) * g


def _split_bf16(x, parts):
    out = []
    for _ in range(parts - 1):
        hi = x.astype(BF16)
        out.append(hi)
        x = x - hi.astype(F32)
    out.append(x.astype(BF16))
    return out


def _head_sums(x, ones_bd):
    cols = []
    for p in range(x.shape[1] // LANES):
        xs = x[:, p * LANES:(p + 1) * LANES]
        hi, lo = _split_bf16(xs, 2)
        cols.append(_dot(hi, ones_bd) + _dot(lo, ones_bd))
    return jnp.concatenate(cols, axis=1)


def _rope_table_kernel(pos_ref, freq_ref, cos_ref, sin_ref):
    ang = pos_ref[...].astype(F32) * freq_ref[...]
    cos_ref[...] = jnp.cos(ang)
    sin_ref[...] = jnp.sin(ang)


def _rope_tables(positions, tm):
    t = positions.size
    inv_freq = ROPE_THETA ** (-jnp.arange(0, MLA_ROPE, 2, dtype=F32) / MLA_ROPE)
    freq = jnp.zeros((1, SLOT), F32)
    freq = freq.at[0, ROPE_LO:ROPE_LO + ROPE_HALF].set(inv_freq)
    freq = freq.at[0, ROPE_LO + ROPE_HALF:ROPE_LO + MLA_ROPE].set(inv_freq)
    return pl.pallas_call(
        _rope_table_kernel,
        grid=(t // tm,),
        in_specs=[_rows(tm, 1), _resident((1, SLOT))],
        out_specs=[_rows(tm, SLOT), _rows(tm, SLOT)],
        out_shape=[jax.ShapeDtypeStruct((t, SLOT), F32)] * 2,
        compiler_params=_params("parallel"),
        name="rope_tables",
    )(positions.reshape(t, 1), freq)


def _rope_slots(z, cos, sin):
    width = z.shape[1]
    lane = lax.broadcasted_iota(jnp.int32, (1, width), 1) % SLOT
    first = (lane >= ROPE_LO) & (lane < ROPE_LO + ROPE_HALF)
    second = (lane >= ROPE_LO + ROPE_HALF) & (lane < ROPE_LO + MLA_ROPE)
    from_left = pltpu.roll(z, ROPE_HALF, axis=1)
    from_right = pltpu.roll(z, width - ROPE_HALF, axis=1)
    partner = jnp.where(second, from_left, jnp.where(first, -from_right, 0.0))
    return z * cos + partner * sin


def _rope_slots_partnered(z, cos, sin):
    width = z.shape[1]
    return z * cos + pltpu.roll(z, width - MLA_ROPE, axis=1) * sin


def _mla_qkv_kernel(xb_ref, wd_ref, qn_ref, kvn_ref, wq_ref, wkv_ref, cos_ref, sin_ref, q_ref, k_ref, vt_ref, *, scale):
    cos, sin = cos_ref[...], sin_ref[...]
    lat = _dot(xb_ref[...], wd_ref[...])
    cq = _rms(lat[:, :MLA_Q_LORA], qn_ref[...]).astype(BF16)
    ckv = _rms(lat[:, MLA_Q_LORA:MLA_Q_LORA + MLA_KV_LORA], kvn_ref[...]).astype(BF16)
    kpe = _rope_slots(lat[:, MLA_Q_LORA + MLA_KV_LORA:], cos, sin)
    q = _dot(cq, wq_ref[...])
    q_ref[...] = (_rope_slots_partnered(q, jnp.tile(cos, (1, MLA_HEADS)), jnp.tile(sin, (1, MLA_HEADS)))
                  * scale).astype(BF16)
    kv = _dot(ckv, wkv_ref[...])
    kw = MLA_HEADS * SLOT
    k_ref[...] = (kv[:, :kw] + jnp.tile(kpe, (1, MLA_HEADS))).astype(BF16)
    vt_ref[0] = kv[:, kw:].T.astype(BF16)


def _mla_attn_kernel(q_ref, k_ref, vt_ref, o_ref, *, tq, tk):
    assert tq == tk
    n_full = pl.program_id(2)
    heads = (0, 1)
    slots = tuple(slice(hh * SLOT, (hh + 1) * SLOT) for hh in heads)

    def own_rows(hh, width):
        sub = lax.broadcasted_iota(jnp.int32, (LANES, width), 0)
        return sub < MLA_V if hh == 0 else sub >= MLA_V

    def causal(keys, queries):
        return (lax.broadcasted_iota(jnp.int32, (keys, queries), 0)
                <= lax.broadcasted_iota(jnp.int32, (keys, queries), 1))

    def scores(j):
        start = pl.multiple_of(j * tk, tk)
        return tuple(_dot_nt(k_ref[pl.ds(start, tk), slot], q_ref[:, slot]) for slot in slots)

    def accumulate(s, state, vt, hh):
        m, acc = state
        m_new = jnp.maximum(m, jnp.max(s, axis=0, keepdims=True))
        alpha = jnp.exp2(m - m_new)
        p = jnp.exp2(s - m_new).astype(BF16)
        vt = jnp.where(own_rows(hh, vt.shape[1]), vt, jnp.ones(vt.shape, BF16))
        return m_new, alpha * acc + _dot(vt, p)

    def update(j, s_pair, states):
        return tuple(accumulate(s, st, vt_ref[j], hh) for hh, s, st in zip(heads, s_pair, states))

    def body(j, states):
        return update(j, scores(j), states)

    def body2(j2, states):
        s_even, s_odd = scores(2 * j2), scores(2 * j2 + 1)
        return update(2 * j2 + 1, s_odd, update(2 * j2, s_even, states))

    init = (jnp.full((1, tq), NEG_INF, F32), jnp.zeros((LANES, tq), F32))
    states = lax.fori_loop(0, n_full // 2, body2, (init, init))
    states = lax.fori_loop(n_full - n_full % 2, n_full, body, states)
    s_diag = tuple(jnp.where(causal(tk, tq), s, NEG_INF) for s in scores(n_full))
    (_, acc0), (_, acc1) = update(n_full, s_diag, states)
    out_t = jnp.where(own_rows(0, tq), acc0 / acc0[MLA_V:MLA_V + 1, :], acc1 / acc1[0:1, :])
    o_ref[...] = out_t.T.astype(o_ref.dtype)


def _proj_ln_kernel(a_ref, w_ref, x_ref, g_ref, b_ref, xo_ref, xbo_ref):
    y = _post_ln(x_ref[...], _dot(a_ref[...], w_ref[...]), g_ref[...], b_ref[...])
    xo_ref[...] = y
    xbo_ref[...] = y.astype(BF16)


def _proj_ln(a, w, x, g, b, tm):
    t, d = x.shape
    return pl.pallas_call(
        _proj_ln_kernel,
        grid=(t // tm,),
        in_specs=[_rows(tm, a.shape[1]), _resident(w.shape), _rows(tm, d), _resident((1, d)), _resident((1, d))],
        out_specs=[_rows(tm, d), _rows(tm, d)],
        out_shape=[jax.ShapeDtypeStruct((t, d), F32), jax.ShapeDtypeStruct((t, d), BF16)],
        compiler_params=_params("parallel"),
        name="proj_ln",
    )(a, w, x, g.reshape(1, d), b.reshape(1, d))


def _mla_layer(x, xb, cos, sin, batch, w_down, q_norm, kv_norm, w_uq, w_ukv, w_o, ln_g, ln_b, tm):
    t, d = x.shape
    seq = t // batch
    h = MLA_HEADS
    pad = jnp.zeros((d, ROPE_LO), F32)
    w_down_s = jnp.concatenate(
        [w_down[:, :MLA_Q_LORA + MLA_KV_LORA], pad, w_down[:, MLA_Q_LORA + MLA_KV_LORA:],
         jnp.zeros((d, SLOT - ROPE_LO - MLA_ROPE), F32)], axis=1).astype(BF16)
    w_uq3 = w_uq.reshape(MLA_Q_LORA, h, MLA_NOPE + MLA_ROPE)
    w_partner = jnp.concatenate([-w_uq3[:, :, MLA_NOPE + ROPE_HALF:], w_uq3[:, :, MLA_NOPE:MLA_NOPE + ROPE_HALF]], axis=2)
    w_uq_s = jnp.concatenate([w_uq3, w_partner], axis=2).reshape(MLA_Q_LORA, h * SLOT).astype(BF16)
    w_ukv3 = w_ukv.reshape(MLA_KV_LORA, h, MLA_NOPE + MLA_V)
    w_k_s = jnp.pad(w_ukv3[:, :, :MLA_NOPE], ((0, 0), (0, 0), (0, SLOT - MLA_NOPE))).reshape(MLA_KV_LORA, h * SLOT)
    w_v_s = w_ukv3[:, :, MLA_NOPE:].reshape(MLA_KV_LORA, h * MLA_V)
    w_kv_s = jnp.concatenate([w_k_s, w_v_s], axis=1).astype(BF16)

    assert SLOT == MLA_NOPE + 2 * MLA_ROPE
    tq = tk = tm
    nq = seq // tq
    q, k, vt = pl.pallas_call(
        functools.partial(_mla_qkv_kernel, scale=(MLA_NOPE + MLA_ROPE) ** -0.5 * LOG2_E),
        grid=(t // tm,),
        in_specs=[_rows(tm, d), _resident(w_down_s.shape), _resident((1, MLA_Q_LORA)), _resident((1, MLA_KV_LORA)),
                  _resident(w_uq_s.shape), _resident(w_kv_s.shape), _rows(tm, SLOT), _rows(tm, SLOT)],
        out_specs=[_rows(tm, h * SLOT), _rows(tm, h * SLOT), pl.BlockSpec((1, h * MLA_V, tm), lambda i: (i, 0, 0))],
        out_shape=[jax.ShapeDtypeStruct((t, h * SLOT), BF16), jax.ShapeDtypeStruct((t, h * SLOT), BF16),
                   jax.ShapeDtypeStruct((t // tm, h * MLA_V, tm), BF16)],
        compiler_params=_params("parallel"),
        name="mla_qkv",
    )(xb, w_down_s, q_norm.reshape(1, -1), kv_norm.reshape(1, -1), w_uq_s, w_kv_s, cos, sin)

    o = pl.pallas_call(
        functools.partial(_mla_attn_kernel, tq=tq, tk=tk),
        grid=(batch, h // 2, nq),
        in_specs=[pl.BlockSpec((tq, 2 * SLOT), lambda b, p, i: (b * nq + i, p)),
                  pl.BlockSpec((seq, 2 * SLOT), lambda b, p, i: (b, p)),
                  pl.BlockSpec((nq, 2 * MLA_V, tk), lambda b, p, i: (b, p, 0))],
        out_specs=pl.BlockSpec((tq, 2 * MLA_V), lambda b, p, i: (b * nq + i, p)),
        out_shape=jax.ShapeDtypeStruct((t, h * MLA_V), BF16),
        compiler_params=_params("parallel", "parallel", "arbitrary"),
        name="mla_attn",
    )(q, k, vt)

    return _proj_ln(o, w_o.astype(BF16), x, ln_g, ln_b, tm)


def _dil_qkv_kernel(x_ref, w_ref, o_ref, res_ref, *, dil):
    res = _dot(x_ref[...], w_ref[...])
    if dil == 1:
        o_ref[0, 0] = res.astype(o_ref.dtype)
        return
    rows = res.shape[0] // dil
    for p in range(res.shape[1] // LANES):
        sl = slice(p * LANES, (p + 1) * LANES)
        res_ref[p] = res[:, sl]
        for r in range(dil):
            o_ref[0, r, :, sl] = res_ref[p, pl.ds(r, rows, stride=dil), :].astype(o_ref.dtype)


def _dil_attn_kernel(q_ref, kp_ref, kc_ref, vp_ref, vc_ref, o_ref, lse_ref, *, n_sub):
    q_ref, kp_ref, kc_ref, vp_ref, vc_ref, o_ref, lse_ref = (
        ref.at[0, 0] for ref in (q_ref, kp_ref, kc_ref, vp_ref, vc_ref, o_ref, lse_ref))
    bk = DIL_BLOCK
    i = lax.broadcasted_iota(jnp.int32, (bk, 2 * bk), 0)
    j = lax.broadcasted_iota(jnp.int32, (bk, 2 * bk), 1)
    dist = i + bk - j
    band = (dist >= 0) & (dist <= bk)
    prev_exists = (pl.program_id(2) > 0) | (j >= bk)
    lse_lane = lax.broadcasted_iota(jnp.int32, (1, LANES), 1)
    lo = lse_lane < DIL_HEAD_DIM
    pairs = [slice(p * LANES, (p + 1) * LANES) for p in range(DIL_HEADS // 2)]
    for sb in range(n_sub):
        rows = slice(sb * bk, (sb + 1) * bk)
        valid = band & prev_exists if sb == 0 else band

        def window(prev_ref, cur_ref, sl, sb=sb):
            if sb == 0:
                return jnp.concatenate([prev_ref[:, sl], cur_ref[:bk, sl]], axis=0)
            return cur_ref[(sb - 1) * bk:(sb + 1) * bk, sl]

        scores = []
        for sl in pairs:
            q = q_ref[rows, sl]
            k = window(kp_ref, kc_ref, sl)
            zero = jnp.zeros_like(q)
            scores.append([_dot_nt(qh, k) for qh in (jnp.where(lo, q, zero), jnp.where(lo, zero, q))])
        lse_blk = jnp.zeros((bk, LANES), F32)
        for p, (sl, pair_scores) in enumerate(zip(pairs, scores)):
            v = window(vp_ref, vc_ref, sl)
            res = []
            for s in pair_scores:
                s = jnp.where(valid, s, NEG_INF)
                m = jnp.max(s, axis=-1, keepdims=True)
                e = jnp.exp(s - m)
                l = jnp.sum(e, axis=-1, keepdims=True)
                res.append((_dot(e.astype(BF16), v) / l, m + jnp.log(l)))
            o_ref[rows, sl] = jnp.where(lo, res[0][0], res[1][0]).astype(o_ref.dtype)
            lse_blk = jnp.where(lse_lane // DIL_LSE_LANES == 2 * p, res[0][1],
                                jnp.where(lse_lane // DIL_LSE_LANES == 2 * p + 1, res[1][1], lse_blk))
        lse_ref[rows, :] = lse_blk


def _dil_out_kernel(o1_ref, o2_ref, o3_ref, l1_ref, l2_ref, l3_ref, expand_ref, w_ref, x_ref, g_ref, b_ref,
                    xo_ref, xbo_ref, *bufs):
    bufs = list(bufs)

    def token_order(ref, dil):
        if dil == 1:
            return ref[0, 0].astype(F32)
        n_blocks = ref.shape[3] // LANES
        buf = bufs.pop()
        for p in range(n_blocks):
            for r in range(dil):
                buf[p, pl.ds(r, ref.shape[2], stride=dil), :] = ref[0, r, :, p * LANES:(p + 1) * LANES].astype(F32)
        return jnp.concatenate([buf[p] for p in range(n_blocks)], axis=1)

    dils = [dil for _, dil in DIL_GROUPS]
    l1, l2, l3 = (token_order(ref, dil) for ref, dil in zip((l1_ref, l2_ref, l3_ref), dils))
    m = jnp.maximum(jnp.maximum(l1, l2), l3)
    e1, e2, e3 = jnp.exp(l1 - m), jnp.exp(l2 - m), jnp.exp(l3 - m)
    inv = 1.0 / (e1 + e2 + e3)
    w1, w2, w3 = (_dot((e * inv).astype(BF16), expand_ref[...]) for e in (e1, e2, e3))
    o1, o2, o3 = (token_order(ref, dil) for ref, dil in zip((o1_ref, o2_ref, o3_ref), dils))
    o = w1 * o1 + w2 * o2 + w3 * o3
    y = _post_ln(x_ref[...], _dot(o.astype(BF16), w_ref[...]), g_ref[...], b_ref[...])
    xo_ref[...] = y
    xbo_ref[...] = y.astype(BF16)


def _dil_layer(x, xb, batch, w_qkv, w_o, ln_g, ln_b, tm):
    t, d = x.shape
    seq = t // batch
    ng = len(DIL_GROUPS)
    hd = DIL_HEADS * DIL_HEAD_DIM
    nqkv = ng * 3 * hd
    col_kind = (jnp.arange(nqkv) // hd) % 3
    w_s = (w_qkv * jnp.where(col_kind == 0, DIL_HEAD_DIM ** -0.5, 1.0)[None, :]).astype(BF16)
    tps = seq // tm
    bk = DIL_BLOCK
    outs, lses = [], []
    for gi, (window, dil) in enumerate(DIL_GROUPS):
        assert window // dil == bk and seq % (dil * bk * DIL_STEP_BLOCKS) == 0 and tm % (16 * dil) == 0
        qkv = pl.pallas_call(
            functools.partial(_dil_qkv_kernel, dil=dil),
            grid=(3, t // tm),
            in_specs=[pl.BlockSpec((tm, d), lambda c, i: (i, 0)),
                      pl.BlockSpec((d, hd), lambda c, i, gi=gi: (0, gi * 3 + c))],
            out_specs=pl.BlockSpec((1, dil, tm // dil, hd), lambda c, i: (i // tps, 0, i % tps, c)),
            out_shape=jax.ShapeDtypeStruct((batch, dil, seq // dil, 3 * hd), BF16),
            scratch_shapes=[pltpu.VMEM((hd // LANES, tm, LANES), F32)],
            compiler_params=_params("parallel", "parallel"),
            name=f"dil_qkv_d{dil}",
        )(xb, w_s)

        n_sub = DIL_STEP_BLOCKS

        def spec(kind, prev, n_sub=n_sub):
            if prev:
                return pl.BlockSpec((1, 1, bk, hd), lambda b, r, n: (b, r, jnp.maximum(n * n_sub - 1, 0), kind))
            return pl.BlockSpec((1, 1, n_sub * bk, hd), lambda b, r, n: (b, r, n, kind))

        out_specs = [pl.BlockSpec((1, 1, n_sub * bk, width), lambda b, r, n: (b, r, n, 0)) for width in (hd, LANES)]
        o, lse = pl.pallas_call(
            functools.partial(_dil_attn_kernel, n_sub=n_sub),
            grid=(batch, dil, seq // (dil * bk * n_sub)),
            in_specs=[spec(0, False), spec(1, True), spec(1, False), spec(2, True), spec(2, False)],
            out_specs=out_specs,
            out_shape=[jax.ShapeDtypeStruct((batch, dil, seq // dil, hd), BF16),
                       jax.ShapeDtypeStruct((batch, dil, seq // dil, LANES), F32)],
            compiler_params=_params("parallel", "parallel", "arbitrary"),
            name=f"dil_attn_d{dil}",
        )(qkv, qkv, qkv, qkv, qkv)
        outs.append(o)
        lses.append(lse)

    def res_specs(width):
        return [pl.BlockSpec((1, dil, tm // dil, width), lambda i: (i // tps, 0, i % tps, 0)) for _, dil in DIL_GROUPS]

    expand = (jnp.arange(LANES)[:, None] == DIL_LSE_LANES * (jnp.arange(hd) // DIL_HEAD_DIM)[None, :]).astype(BF16)
    n_strided = sum(dil > 1 for _, dil in DIL_GROUPS)
    bufs = [pltpu.VMEM((hd // LANES, tm, LANES), F32)] * n_strided + [pltpu.VMEM((1, tm, LANES), F32)] * n_strided
    return pl.pallas_call(
        _dil_out_kernel,
        grid=(t // tm,),
        in_specs=res_specs(hd) + res_specs(LANES) + [_resident((LANES, hd)), _resident((hd, d)), _rows(tm, d),
                                                     _resident((1, d)), _resident((1, d))],
        out_specs=[_rows(tm, d), _rows(tm, d)],
        out_shape=[jax.ShapeDtypeStruct((t, d), F32), jax.ShapeDtypeStruct((t, d), BF16)],
        scratch_shapes=bufs,
        compiler_params=_params("parallel"),
        name="dil_out",
    )(*outs, *lses, expand, w_o.astype(BF16), x, ln_g.reshape(1, d), ln_b.reshape(1, d))


def _rwkv_pre_kernel(x_ref, halo_ref, mu_ref, wrkv_ref, w0_ref, w1_ref, w2_ref, a0_ref, a1_ref, a2_ref,
                     g1_ref, g2_ref, kk_ref, ka_ref, bd_ref,
                     r_o, k_o, v_o, kkn_o, a_o, lw_o, g_o, *, tiles_per_seq):
    first = (pl.program_id(0) % tiles_per_seq) == 0
    x = x_ref[...]
    tm = x.shape[0]
    prev_row = jnp.where(first, 0.0, halo_ref[7:8, :])
    row = lax.broadcasted_iota(jnp.int32, (tm, 1), 0)
    xx = jnp.where(row == 0, prev_row, pltpu.roll(x, 1, axis=0)) - x
    mu = mu_ref[...]

    def mix(j):
        return (x + xx * mu[j:j + 1, :]).astype(BF16)

    r = _dot(mix(0), wrkv_ref[0])
    k = _dot(mix(1), wrkv_ref[1])
    v = _dot(mix(2), wrkv_ref[2])
    z = w0_ref[...] + _dot(jnp.tanh(_dot(mix(3), w1_ref[...])).astype(BF16), w2_ref[...])
    nz = -z
    softplus = jnp.maximum(nz, 0.0) + jnp.log(1.0 + jnp.exp(-jnp.abs(nz)))
    lw = -jnp.exp(-softplus - 0.5)
    a = jax.nn.sigmoid(a0_ref[...] + _dot(_dot(mix(4), a1_ref[...]).astype(BF16), a2_ref[...]))
    g = _dot(jax.nn.sigmoid(_dot(mix(5), g1_ref[...])).astype(BF16), g2_ref[...])
    kk = k * kk_ref[...]
    norm = jnp.sqrt(_head_sums(kk * kk, bd_ref[...]))
    r_o[...] = r
    k_o[...] = k * (1.0 + (a - 1.0) * ka_ref[...])
    v_o[...] = v.astype(v_o.dtype)
    kkn_o[...] = kk / jnp.maximum(norm, 1e-12)
    a_o[...] = a
    lw_o[...] = lw
    g_o[...] = g.astype(g_o.dtype)


def _rwkv_chunk_kernel(r_ref, k_ref, v_ref, kk_ref, a_ref, lw_ref, o_ref, s_ref, *, n_pairs, n_chunks):
    c = RWKV_CHUNK

    @pl.when(pl.program_id(2) == 0)
    def _():
        s_ref[...] = jnp.zeros_like(s_ref)

    ri = lax.broadcasted_iota(jnp.int32, (2 * c, 2 * c), 0)
    ci = lax.broadcasted_iota(jnp.int32, (2 * c, 2 * c), 1)
    blk = {}
    s = RWKV_INV_BASE
    while s <= c:
        blk[s] = (ri // s) == (ci // s)
        s *= 2
    strict = blk[c] & (ci < ri)
    incl = blk[c] & (ci <= ri)
    eye = jnp.where(ri == ci, 1.0, 0.0).astype(F32)
    tri = jnp.where(lax.broadcasted_iota(jnp.int32, (c, c), 1) <= lax.broadcasted_iota(jnp.int32, (c, c), 0),
                    1.0, 0.0).astype(BF16)
    lane_lo = lax.broadcasted_iota(jnp.int32, (1, LANES), 1) < RWKV_HEAD

    def stack(x):
        return jnp.concatenate([jnp.where(lane_lo, x, 0.0), jnp.where(lane_lo, 0.0, x)], axis=0)

    def bf(x):
        return x.astype(BF16)

    streams = [(ch, p) for ch in range(n_chunks) for p in range(n_pairs)]

    def load(ref):
        return [ref[ch * c:(ch + 1) * c, p * LANES:(p + 1) * LANES] for ch, p in streams]

    rc, kc, vc, kkc, ac, lwc = (load(ref) for ref in (r_ref, k_ref, v_ref, kk_ref, a_ref, lw_ref))
    vc = [x.astype(F32) for x in vc]

    cum = []
    for x in lwc:
        cs = _dot(tri, jnp.concatenate(_split_bf16(x, 3), axis=1))
        cum.append(cs[:, :LANES] + cs[:, LANES:2 * LANES] + cs[:, 2 * LANES:])
    p_incl = [jnp.exp(x) for x in cum]
    p_inv = [jnp.exp(-x) for x in cum]
    p_end = [x[c - 1:c, :] for x in p_incl]
    at = [-kk * jnp.exp(cm - lw) for kk, cm, lw in zip(kkc, cum, lwc)]
    bt = [kk * a * pi for kk, a, pi in zip(kkc, ac, p_inv)]
    kt = [k * pi for k, pi in zip(kc, p_inv)]
    rt = [r * pf for r, pf in zip(rc, p_incl)]
    a_st = [bf(stack(x)) for x in at]
    r_st = [stack(x) for x in rt]
    b_st = [bf(stack(x)) for x in bt]
    k_st = [bf(stack(x)) for x in kt]
    v_st = [bf(stack(x)) for x in vc]

    prod = [_dot_nt(jnp.concatenate([a, bf(r)], axis=0), bf(jnp.concatenate([b, b, k, k], axis=0)))
            for a, r, b, k in zip(a_st, r_st, bt, kt)]
    l_ab = [jnp.where(strict, x[:2 * c, :2 * c], 0.0) for x in prod]
    l_ak = [bf(jnp.where(strict, x[:2 * c, 2 * c:], 0.0)) for x in prod]
    m_rb = [bf(jnp.where(incl, x[2 * c:, :2 * c], 0.0)) for x in prod]
    m_rk = [bf(jnp.where(incl, x[2 * c:, 2 * c:], 0.0)) for x in prod]

    dg = [bf(jnp.where(blk[RWKV_INV_BASE], x, 0.0)) for x in l_ab]
    d2 = [bf(_dot(x, x)) for x in dg]
    d4 = [bf(_dot(x, x)) for x in d2]
    tinv = [eye + x.astype(F32) for x in dg]
    tinv = [t + _dot(bf(t), y) for t, y in zip(tinv, d2)]
    tinv = [t + _dot(bf(t), y) for t, y in zip(tinv, d4)]
    size = RWKV_INV_BASE
    while size < c:
        off = [bf(jnp.where(blk[2 * size] & jnp.logical_not(blk[size]), x, 0.0)) for x in l_ab]
        tb = [bf(t) for t in tinv]
        te = [bf(_dot(t, e)) for t, e in zip(tb, off)]
        tinv = [t + _dot(x, y) for t, x, y in zip(tinv, te, tb)]
        size *= 2
    tb = [bf(t) for t in tinv]

    lakv = [bf(_dot(x, v)) for x, v in zip(l_ak, v_st)]
    gu = [_dot(t, jnp.concatenate([a, y], axis=1)) for t, a, y in zip(tb, a_st, lakv)]
    mg = [_dot(m, bf(x)) for m, x in zip(m_rb, gu)]
    rg = [bf(r + x[:, :LANES]) for r, x in zip(r_st, mg)]
    o_v = [x[:, LANES:] + _dot(m, v) for x, m, v in zip(mg, m_rk, v_st)]
    q_mat = [bf(_dot(bf(x[:, :LANES].T), b) * pe) for x, b, pe in zip(gu, b_st, p_end)]
    n_mat = [(_dot(bf(x[:, LANES:].T), b) + _dot(bf(stack(v).T), k)) * pe
             for x, b, v, k, pe in zip(gu, b_st, vc, k_st, p_end)]

    state = [s_ref[p] for p in range(n_pairs)]
    for i, (ch, p) in enumerate(streams):
        sb = bf(state[p])
        o_st = _dot_nt(rg[i], sb) + o_v[i]
        o_ref[ch * c:(ch + 1) * c, p * LANES:(p + 1) * LANES] = o_st[:c] + o_st[c:]
        state[p] = state[p] * p_end[i] + _dot(sb, q_mat[i]) + n_mat[i]
    for p in range(n_pairs):
        s_ref[p] = state[p]


def _rwkv_post_kernel(o_ref, r_ref, k_ref, v_ref, g_ref, x_ref, lnw_ref, lnb_ref, rk_ref, bd_ref, w_ref,
                      gl_ref, bl_ref, xo_ref, xbo_ref):
    bd = bd_ref[...]
    o = o_ref[...]
    mean = _head_sums(o, bd) * (1.0 / RWKV_HEAD)
    dlt = o - mean
    var = _head_sums(dlt * dlt, bd) * (1.0 / RWKV_HEAD)
    y = dlt * lax.rsqrt(var + RWKV_GN_EPS) * lnw_ref[...] + lnb_ref[...]
    bonus = _head_sums(r_ref[...] * k_ref[...] * rk_ref[...], bd) * v_ref[...]
    out = ((y + bonus) * g_ref[...]).astype(BF16)
    res = _post_ln(x_ref[...], _dot(out, w_ref[...]), gl_ref[...], bl_ref[...])
    xo_ref[...] = res
    xbo_ref[...] = res.astype(BF16)


def _rwkv_layer(x, batch, mu, w_rkv, w0, w1, w2, a0, a1, a2, g1, g2, k_k, k_a, r_k, ln_w, ln_b, w_o,
                ln_g_post, ln_b_post, tm):
    t, d = x.shape
    seq = t // batch
    tmr = min(256, seq)
    vec = lambda p: p.reshape(1, d)
    idx = jnp.arange(LANES) // RWKV_HEAD
    ones_bd = (idx[:, None] == idx[None, :]).astype(BF16)
    tok = [jax.ShapeDtypeStruct((t, d), dt) for dt in (F32, F32, BF16, F32, F32, F32, BF16)]
    halo = pl.BlockSpec((8, d), lambda i: (jnp.maximum(i * (tmr // 8) - 1, 0), 0))
    r, k, v, kkn, a, lw, g = pl.pallas_call(
        functools.partial(_rwkv_pre_kernel, tiles_per_seq=seq // tmr),
        grid=(t // tmr,),
        in_specs=[_rows(tmr, d), halo, _resident((6, d)), _resident((3, d, d)),
                  _resident((1, d)), _resident(w1.shape), _resident(w2.shape),
                  _resident((1, d)), _resident(a1.shape), _resident(a2.shape),
                  _resident(g1.shape), _resident(g2.shape), _resident((1, d)), _resident((1, d)),
                  _resident((LANES, LANES))],
        out_specs=[_rows(tmr, d)] * 7,
        out_shape=tok,
        compiler_params=_params("parallel"),
        name="rwkv_pre",
    )(x, x, mu, w_rkv.astype(BF16), vec(w0), w1.astype(BF16), w2.astype(BF16), vec(a0), a1.astype(BF16),
      a2.astype(BF16), g1.astype(BF16), g2.astype(BF16), vec(k_k), vec(k_a), ones_bd)

    tc = min(RWKV_STEP_CHUNKS * RWKV_CHUNK, seq)
    nt = seq // tc
    width = RWKV_STEP_PAIRS * LANES
    pair = pl.BlockSpec((tc, width), lambda b, p, i: (b * nt + i, p))
    o = pl.pallas_call(
        functools.partial(_rwkv_chunk_kernel, n_pairs=RWKV_STEP_PAIRS, n_chunks=tc // RWKV_CHUNK),
        grid=(batch, d // width, nt),
        in_specs=[pair] * 6,
        out_specs=pair,
        out_shape=jax.ShapeDtypeStruct((t, d), F32),
        scratch_shapes=[pltpu.VMEM((RWKV_STEP_PAIRS, LANES, LANES), F32)],
        compiler_params=_params("parallel", "parallel", "arbitrary"),
        name="rwkv_chunk",
    )(r, k, v, kkn, a, lw)

    return pl.pallas_call(
        _rwkv_post_kernel,
        grid=(t // tm,),
        in_specs=[_rows(tm, d)] * 6 + [_resident((1, d))] * 3 + [_resident((LANES, LANES)), _resident((d, d)),
                                                                _resident((1, d)), _resident((1, d))],
        out_specs=[_rows(tm, d), _rows(tm, d)],
        out_shape=[jax.ShapeDtypeStruct((t, d), F32), jax.ShapeDtypeStruct((t, d), BF16)],
        compiler_params=_params("parallel"),
        name="rwkv_post",
    )(o, r, k, v, g, x, vec(ln_w), vec(ln_b), vec(r_k), ones_bd, w_o.astype(BF16), vec(ln_g_post), vec(ln_b_post))


def _ffn_kernel(x_ref, xb_ref, halo_ref, wa_ref, wb_ref, cw_ref, cb_ref, wo_ref, g_ref, b_ref,
                xo_ref, xbo_ref, acc_ref, *, tiles_per_seq, n_chunks, halo_rows):
    first = (pl.program_id(0) % tiles_per_seq) == 0
    xb = xb_ref[...]
    tm = xb.shape[0]
    halo = halo_ref[...]
    halo = jnp.where(first, jnp.zeros_like(halo), halo)
    xe = jnp.concatenate([halo, xb], axis=0)
    acc_ref[...] = jnp.zeros_like(acc_ref)

    def up(c):
        return _dot(xe, wa_ref[c]), _dot(xb, wb_ref[c])

    def down(c, ha, hb):
        cw = cw_ref[c]
        conv = (cw[0:1, :] * ha[halo_rows - 2:halo_rows - 2 + tm, :]
                + cw[1:2, :] * ha[halo_rows - 1:halo_rows - 1 + tm, :]
                + cw[2:3, :] * ha[halo_rows:, :] + cb_ref[c])
        act = conv * jax.nn.sigmoid(conv) * hb
        acc_ref[...] += _dot(act.astype(BF16), wo_ref[c])

    h = up(0)
    for c in range(n_chunks):
        h_next = up(c + 1) if c + 1 < n_chunks else None
        down(c, *h)
        h = h_next
    y = _post_ln(x_ref[...], acc_ref[...], g_ref[...], b_ref[...])
    xo_ref[...] = y
    xbo_ref[...] = y.astype(BF16)


def _ffn_layer(x, xb, batch, w_in, conv_w, conv_b, w_out, ln_g, ln_b, tm):
    t, d = x.shape
    seq = t // batch
    nch = D_FF_PAD // FF_CHUNK
    padc = D_FF_PAD - D_FF
    chunks = lambda w: jnp.pad(w, ((0, 0), (0, padc))).reshape(w.shape[0], nch, FF_CHUNK).transpose(1, 0, 2)
    wa = chunks(w_in[:, :D_FF]).astype(BF16)
    wb = chunks(w_in[:, D_FF:]).astype(BF16)
    cw = chunks(conv_w)
    cb = chunks(conv_b.reshape(1, D_FF))
    wo = jnp.pad(w_out, ((0, padc), (0, 0))).reshape(nch, FF_CHUNK, d).astype(BF16)
    halo_rows = 16
    halo = pl.BlockSpec((halo_rows, d), lambda i: (jnp.maximum(i * (tm // halo_rows) - 1, 0), 0))
    return pl.pallas_call(
        functools.partial(_ffn_kernel, tiles_per_seq=seq // tm, n_chunks=nch, halo_rows=halo_rows),
        grid=(t // tm,),
        in_specs=[_rows(tm, d), _rows(tm, d), halo, _resident(wa.shape), _resident(wb.shape), _resident(cw.shape),
                  _resident(cb.shape), _resident(wo.shape), _resident((1, d)), _resident((1, d))],
        out_specs=[_rows(tm, d), _rows(tm, d)],
        out_shape=[jax.ShapeDtypeStruct((t, d), F32), jax.ShapeDtypeStruct((t, d), BF16)],
        scratch_shapes=[pltpu.VMEM((tm, d), F32)],
        compiler_params=_params("parallel"),
        name="conv_ffn",
    )(x, xb, xb, wa, wb, cw, cb, wo, ln_g.reshape(1, d), ln_b.reshape(1, d))


def kernel(x, positions, ln_g, ln_b, mla_w_down, mla_q_norm, mla_kv_norm, mla_w_uq, mla_w_ukv, mla_w_o,
           dil_w_qkv, dil_w_o, rwkv_mu, rwkv_w_rkv, rwkv_w0, rwkv_w1, rwkv_w2, rwkv_a0, rwkv_a1, rwkv_a2,
           rwkv_g1, rwkv_g2, rwkv_k_k, rwkv_k_a, rwkv_r_k, rwkv_ln_w, rwkv_ln_b, rwkv_w_o,
           ffn_w_in, ffn_conv_w, ffn_conv_b, ffn_w_out):
    batch, seq, d = x.shape
    t = batch * seq
    tm = min(512, seq)
    xf = x.reshape(t, d)
    xb = xf.astype(BF16)
    cos, sin = _rope_tables(positions, tm)
    ia = ib = ic = 0
    for i in range(DEPTH):
        kind = i % 3
        if kind == 0:
            xf, xb = _mla_layer(xf, xb, cos, sin, batch, mla_w_down[ia], mla_q_norm[ia], mla_kv_norm[ia],
                                mla_w_uq[ia], mla_w_ukv[ia], mla_w_o[ia], ln_g[i, 0], ln_b[i, 0], tm)
            ia += 1
        elif kind == 1:
            xf, xb = _dil_layer(xf, xb, batch, dil_w_qkv[ib], dil_w_o[ib], ln_g[i, 0], ln_b[i, 0], tm)
            ib += 1
        else:
            xf, xb = _rwkv_layer(xf, batch, rwkv_mu[ic], rwkv_w_rkv[ic], rwkv_w0[ic], rwkv_w1[ic], rwkv_w2[ic],
                                 rwkv_a0[ic], rwkv_a1[ic], rwkv_a2[ic], rwkv_g1[ic], rwkv_g2[ic],
                                 rwkv_k_k[ic], rwkv_k_a[ic], rwkv_r_k[ic], rwkv_ln_w[ic], rwkv_ln_b[ic],
                                 rwkv_w_o[ic], ln_g[i, 0], ln_b[i, 0], tm)
            ic += 1
        xf, xb = _ffn_layer(xf, xb, batch, ffn_w_in[i], ffn_conv_w[i], ffn_conv_b[i], ffn_w_out[i],
                            ln_g[i, 1], ln_b[i, 1], tm)
    return xf.reshape(batch, seq, d)
```

```python
import functools

import jax
import jax.numpy as jnp
from jax import lax
from jax.experimental import pallas as pl
from jax.experimental.pallas import tpu as pltpu

F32 = jnp.float32
BF16 = jnp.bfloat16

D_MODEL = 1024
DEPTH = 4
MLA_HEADS = 16
MLA_Q_LORA = 384
MLA_KV_LORA = 256
MLA_NOPE = 64
MLA_ROPE = 32
MLA_V = 64
ROPE_THETA = 10000.0
DIL_GROUPS = ((128, 1), (512, 4), (2048, 16))
DIL_HEADS = 16
DIL_HEAD_DIM = 64
DIL_BLOCK = 128
DIL_STEP_BLOCKS = 2
DIL_STAGE_PAIRS = 4
DIL_LSE_LANES = 128 // DIL_HEADS
RWKV_HEAD = 64
RWKV_GN_EPS = 64e-5
D_FF = 2752
ALPHA = (2 * DEPTH) ** 0.25
LN_EPS = 1e-5
RMS_EPS = 1e-6
NEG_INF = -1e30
LOG2_E = 1.4426950408889634

LANES = 128
VMEM_LIMIT_BYTES = 56 * 2**20

SLOT = LANES
ROPE_LO = MLA_NOPE
ROPE_HALF = MLA_ROPE // 2
FF_CHUNK = 256
D_FF_PAD = -(-D_FF // FF_CHUNK) * FF_CHUNK
RWKV_CHUNK = 64
RWKV_INV_BASE = 8
RWKV_STEP_PAIRS = 8
RWKV_STEP_CHUNKS = 2


def _params(*sem):
    return pltpu.CompilerParams(dimension_semantics=sem, vmem_limit_bytes=VMEM_LIMIT_BYTES)


def _dot(a, b):
    return jnp.dot(a, b, preferred_element_type=F32)


def _dot_nt(a, b):
    return lax.dot_general(a, b, (((1,), (1,)), ((), ())), preferred_element_type=F32)


def _resident(shape):
    nd = len(shape)
    return pl.BlockSpec(shape, lambda *_: (0,) * nd, pipeline_mode=pl.Buffered(1))


def _rows(tm, width):
    return pl.BlockSpec((tm, width), lambda i: (i, 0))


def _post_ln(x, h, g, b):
    y = ALPHA * x + h
    mu = jnp.mean(y, axis=-1, keepdims=True)
    d = y - mu
    var = jnp.mean(d * d, axis=-1, keepdims=True)
    return d * lax.rsqrt(var + LN_EPS) * g + b


def _rms(x, g):
    return x * lax.rsqrt(jnp.mean(x * x, axis=-1, keepdims=True) + RMS_EPS) * g


def _split_bf16(x, parts):
    out = []
    for _ in range(parts - 1):
        hi = x.astype(BF16)
        out.append(hi)
        x = x - hi.astype(F32)
    out.append(x.astype(BF16))
    return out


def _head_sums(x, ones_bd):
    cols = []
    for p in range(x.shape[1] // LANES):
        xs = x[:, p * LANES:(p + 1) * LANES]
        hi, lo = _split_bf16(xs, 2)
        cols.append(_dot(hi, ones_bd) + _dot(lo, ones_bd))
    return jnp.concatenate(cols, axis=1)


def _rope_table_kernel(pos_ref, freq_ref, cos_ref, sin_ref):
    ang = pos_ref[...].astype(F32) * freq_ref[...]
    cos_ref[...] = jnp.cos(ang)
    sin_ref[...] = jnp.sin(ang)


def _rope_tables(positions, tm):
    t = positions.size
    inv_freq = ROPE_THETA ** (-jnp.arange(0, MLA_ROPE, 2, dtype=F32) / MLA_ROPE)
    freq = jnp.zeros((1, SLOT), F32)
    freq = freq.at[0, ROPE_LO:ROPE_LO + ROPE_HALF].set(inv_freq)
    freq = freq.at[0, ROPE_LO + ROPE_HALF:ROPE_LO + MLA_ROPE].set(inv_freq)
    return pl.pallas_call(
        _rope_table_kernel,
        grid=(t // tm,),
        in_specs=[_rows(tm, 1), _resident((1, SLOT))],
        out_specs=[_rows(tm, SLOT), _rows(tm, SLOT)],
        out_shape=[jax.ShapeDtypeStruct((t, SLOT), F32)] * 2,
        compiler_params=_params("parallel"),
        name="rope_tables",
    )(positions.reshape(t, 1), freq)


def _rope_slots(z, cos, sin):
    width = z.shape[1]
    lane = lax.broadcasted_iota(jnp.int32, (1, width), 1) % SLOT
    first = (lane >= ROPE_LO) & (lane < ROPE_LO + ROPE_HALF)
    second = (lane >= ROPE_LO + ROPE_HALF) & (lane < ROPE_LO + MLA_ROPE)
    from_left = pltpu.roll(z, ROPE_HALF, axis=1)
    from_right = pltpu.roll(z, width - ROPE_HALF, axis=1)
    partner = jnp.where(second, from_left, jnp.where(first, -from_right, 0.0))
    return z * cos + partner * sin


def _rope_slots_partnered(z, cos, sin):
    width = z.shape[1]
    return z * cos + pltpu.roll(z, width - MLA_ROPE, axis=1) * sin


def _mla_qkv_kernel(xb_ref, wd_ref, qn_ref, kvn_ref, wq_ref, wkv_ref, cos_ref, sin_ref, q_ref, k_ref, vt_ref, *, scale):
    cos, sin = cos_ref[...], sin_ref[...]
    lat = _dot(xb_ref[...], wd_ref[...])
    cq = _rms(lat[:, :MLA_Q_LORA], qn_ref[...]).astype(BF16)
    ckv = _rms(lat[:, MLA_Q_LORA:MLA_Q_LORA + MLA_KV_LORA], kvn_ref[...]).astype(BF16)
    kpe = _rope_slots(lat[:, MLA_Q_LORA + MLA_KV_LORA:], cos, sin)
    q = _dot(cq, wq_ref[...])
    q_ref[...] = (_rope_slots_partnered(q, jnp.tile(cos, (1, MLA_HEADS)), jnp.tile(sin, (1, MLA_HEADS)))
                  * scale).astype(BF16)
    kv = _dot(ckv, wkv_ref[...])
    kw = MLA_HEADS * SLOT
    k_ref[...] = (kv[:, :kw] + jnp.tile(kpe, (1, MLA_HEADS))).astype(BF16)
    vt_ref[0] = kv[:, kw:].T.astype(BF16)


def _mla_attn_kernel(q_ref, k_ref, vt_ref, o_ref, *, tq, tk):
    assert tq == tk
    n_full = pl.program_id(2)
    heads = (0, 1)
    slots = tuple(slice(hh * SLOT, (hh + 1) * SLOT) for hh in heads)

    def own_rows(hh, width):
        sub = lax.broadcasted_iota(jnp.int32, (LANES, width), 0)
        return sub < MLA_V if hh == 0 else sub >= MLA_V

    def causal(keys, queries):
        return (lax.broadcasted_iota(jnp.int32, (keys, queries), 0)
                <= lax.broadcasted_iota(jnp.int32, (keys, queries), 1))

    def scores(j):
        start = pl.multiple_of(j * tk, tk)
        return tuple(_dot_nt(k_ref[pl.ds(start, tk), slot], q_ref[:, slot]) for slot in slots)

    def accumulate(s, state, vt, hh):
        m, acc = state
        m_new = jnp.maximum(m, jnp.max(s, axis=0, keepdims=True))
        alpha = jnp.exp2(m - m_new)
        p = jnp.exp2(s - m_new).astype(BF16)
        vt = jnp.where(own_rows(hh, vt.shape[1]), vt, jnp.ones(vt.shape, BF16))
        return m_new, alpha * acc + _dot(vt, p)

    def update(j, s_pair, states):
        return tuple(accumulate(s, st, vt_ref[j], hh) for hh, s, st in zip(heads, s_pair, states))

    def body(j, states):
        return update(j, scores(j), states)

    def body2(j2, states):
        s_even, s_odd = scores(2 * j2), scores(2 * j2 + 1)
        return update(2 * j2 + 1, s_odd, update(2 * j2, s_even, states))

    init = (jnp.full((1, tq), NEG_INF, F32), jnp.zeros((LANES, tq), F32))
    states = lax.fori_loop(0, n_full // 2, body2, (init, init))
    states = lax.fori_loop(n_full - n_full % 2, n_full, body, states)
    s_diag = tuple(jnp.where(causal(tk, tq), s, NEG_INF) for s in scores(n_full))
    (_, acc0), (_, acc1) = update(n_full, s_diag, states)
    out_t = jnp.where(own_rows(0, tq), acc0 / acc0[MLA_V:MLA_V + 1, :], acc1 / acc1[0:1, :])
    o_ref[...] = out_t.T.astype(o_ref.dtype)


def _proj_ln_kernel(a_ref, w_ref, x_ref, g_ref, b_ref, xo_ref, xbo_ref):
    y = _post_ln(x_ref[...], _dot(a_ref[...], w_ref[...]), g_ref[...], b_ref[...])
    xo_ref[...] = y
    xbo_ref[...] = y.astype(BF16)


def _proj_ln(a, w, x, g, b, tm):
    t, d = x.shape
    return pl.pallas_call(
        _proj_ln_kernel,
        grid=(t // tm,),
        in_specs=[_rows(tm, a.shape[1]), _resident(w.shape), _rows(tm, d), _resident((1, d)), _resident((1, d))],
        out_specs=[_rows(tm, d), _rows(tm, d)],
        out_shape=[jax.ShapeDtypeStruct((t, d), F32), jax.ShapeDtypeStruct((t, d), BF16)],
        compiler_params=_params("parallel"),
        name="proj_ln",
    )(a, w, x, g.reshape(1, d), b.reshape(1, d))


def _mla_layer(x, xb, cos, sin, batch, w_down, q_norm, kv_norm, w_uq, w_ukv, w_o, ln_g, ln_b, tm):
    t, d = x.shape
    seq = t // batch
    h = MLA_HEADS
    pad = jnp.zeros((d, ROPE_LO), F32)
    w_down_s = jnp.concatenate(
        [w_down[:, :MLA_Q_LORA + MLA_KV_LORA], pad, w_down[:, MLA_Q_LORA + MLA_KV_LORA:],
         jnp.zeros((d, SLOT - ROPE_LO - MLA_ROPE), F32)], axis=1).astype(BF16)
    w_uq3 = w_uq.reshape(MLA_Q_LORA, h, MLA_NOPE + MLA_ROPE)
    w_partner = jnp.concatenate([-w_uq3[:, :, MLA_NOPE + ROPE_HALF:], w_uq3[:, :, MLA_NOPE:MLA_NOPE + ROPE_HALF]], axis=2)
    w_uq_s = jnp.concatenate([w_uq3, w_partner], axis=2).reshape(MLA_Q_LORA, h * SLOT).astype(BF16)
    w_ukv3 = w_ukv.reshape(MLA_KV_LORA, h, MLA_NOPE + MLA_V)
    w_k_s = jnp.pad(w_ukv3[:, :, :MLA_NOPE], ((0, 0), (0, 0), (0, SLOT - MLA_NOPE))).reshape(MLA_KV_LORA, h * SLOT)
    w_v_s = w_ukv3[:, :, MLA_NOPE:].reshape(MLA_KV_LORA, h * MLA_V)
    w_kv_s = jnp.concatenate([w_k_s, w_v_s], axis=1).astype(BF16)

    assert SLOT == MLA_NOPE + 2 * MLA_ROPE
    tq = tk = tm
    nq = seq // tq
    q, k, vt = pl.pallas_call(
        functools.partial(_mla_qkv_kernel, scale=(MLA_NOPE + MLA_ROPE) ** -0.5 * LOG2_E),
        grid=(t // tm,),
        in_specs=[_rows(tm, d), _resident(w_down_s.shape), _resident((1, MLA_Q_LORA)), _resident((1, MLA_KV_LORA)),
                  _resident(w_uq_s.shape), _resident(w_kv_s.shape), _rows(tm, SLOT), _rows(tm, SLOT)],
        out_specs=[_rows(tm, h * SLOT), _rows(tm, h * SLOT), pl.BlockSpec((1, h * MLA_V, tm), lambda i: (i, 0, 0))],
        out_shape=[jax.ShapeDtypeStruct((t, h * SLOT), BF16), jax.ShapeDtypeStruct((t, h * SLOT), BF16),
                   jax.ShapeDtypeStruct((t // tm, h * MLA_V, tm), BF16)],
        compiler_params=_params("parallel"),
        name="mla_qkv",
    )(xb, w_down_s, q_norm.reshape(1, -1), kv_norm.reshape(1, -1), w_uq_s, w_kv_s, cos, sin)

    o = pl.pallas_call(
        functools.partial(_mla_attn_kernel, tq=tq, tk=tk),
        grid=(batch, h // 2, nq),
        in_specs=[pl.BlockSpec((tq, 2 * SLOT), lambda b, p, i: (b * nq + i, p)),
                  pl.BlockSpec((seq, 2 * SLOT), lambda b, p, i: (b, p)),
                  pl.BlockSpec((nq, 2 * MLA_V, tk), lambda b, p, i: (b, p, 0))],
        out_specs=pl.BlockSpec((tq, 2 * MLA_V), lambda b, p, i: (b * nq + i, p)),
        out_shape=jax.ShapeDtypeStruct((t, h * MLA_V), BF16),
        compiler_params=_params("parallel", "parallel", "arbitrary"),
        name="mla_attn",
    )(q, k, vt)

    return _proj_ln(o, w_o.astype(BF16), x, ln_g, ln_b, tm)


def _dil_qkv_kernel(x_ref, w_ref, o_ref, res_ref, *, dil):
    res = _dot(x_ref[...], w_ref[...])
    if dil == 1:
        o_ref[0, 0] = res.astype(o_ref.dtype)
        return
    rows = res.shape[0] // dil
    for p in range(res.shape[1] // LANES):
        sl = slice(p * LANES, (p + 1) * LANES)
        res_ref[p] = res[:, sl]
        for r in range(dil):
            o_ref[0, r, :, sl] = res_ref[p, pl.ds(r, rows, stride=dil), :].astype(o_ref.dtype)


def _dil_attn_kernel(q_ref, kp_ref, kc_ref, vp_ref, vc_ref, o_ref, lse_ref, *, n_sub):
    q_ref, kp_ref, kc_ref, vp_ref, vc_ref, o_ref, lse_ref = (
        ref.at[0, 0] for ref in (q_ref, kp_ref, kc_ref, vp_ref, vc_ref, o_ref, lse_ref))
    bk = DIL_BLOCK
    i = lax.broadcasted_iota(jnp.int32, (bk, 2 * bk), 0)
    j = lax.broadcasted_iota(jnp.int32, (bk, 2 * bk), 1)
    dist = i + bk - j
    band = (dist >= 0) & (dist <= bk)
    prev_exists = (pl.program_id(2) > 0) | (j >= bk)
    lse_lane = lax.broadcasted_iota(jnp.int32, (1, LANES), 1)
    lo = lse_lane < DIL_HEAD_DIM
    pairs = [slice(p * LANES, (p + 1) * LANES) for p in range(DIL_HEADS // 2)]
    for sb in range(n_sub):
        rows = slice(sb * bk, (sb + 1) * bk)
        valid = band & prev_exists if sb == 0 else band

        def window(prev_ref, cur_ref, sl, sb=sb):
            if sb == 0:
                return jnp.concatenate([prev_ref[:, sl], cur_ref[:bk, sl]], axis=0)
            return cur_ref[(sb - 1) * bk:(sb + 1) * bk, sl]

        def pair_scores_of(sl):
            q = q_ref[rows, sl]
            k = window(kp_ref, kc_ref, sl)
            zero = jnp.zeros_like(q)
            return [_dot_nt(qh, k) for qh in (jnp.where(lo, q, zero), jnp.where(lo, zero, q))]

        lse_blk = jnp.zeros((bk, LANES), F32)
        scores = {}
        for p, sl in enumerate(pairs):
            if p % DIL_STAGE_PAIRS == 0:
                scores = {pp: pair_scores_of(pairs[pp]) for pp in range(p, min(p + DIL_STAGE_PAIRS, len(pairs)))}
            pair_scores = scores[p]
            v = window(vp_ref, vc_ref, sl)
            res = []
            for s in pair_scores:
                s = jnp.where(valid, s, NEG_INF)
                m = jnp.max(s, axis=-1, keepdims=True)
                e = jnp.exp(s - m)
                l = jnp.sum(e, axis=-1, keepdims=True)
                res.append((_dot(e.astype(BF16), v) / l, m + jnp.log(l)))
            o_ref[rows, sl] = jnp.where(lo, res[0][0], res[1][0]).astype(o_ref.dtype)
            lse_blk = jnp.where(lse_lane // DIL_LSE_LANES == 2 * p, res[0][1],
                                jnp.where(lse_lane // DIL_LSE_LANES == 2 * p + 1, res[1][1], lse_blk))
        lse_ref[rows, :] = lse_blk


def _dil_out_kernel(o1_ref, o2_ref, o3_ref, l1_ref, l2_ref, l3_ref, expand_ref, w_ref, x_ref, g_ref, b_ref,
                    xo_ref, xbo_ref, *bufs):
    bufs = list(bufs)

    def token_order(ref, dil):
        if dil == 1:
            return ref[0, 0].astype(F32)
        n_blocks = ref.shape[3] // LANES
        buf = bufs.pop()
        for p in range(n_blocks):
            for r in range(dil):
                buf[p, pl.ds(r, ref.shape[2], stride=dil), :] = ref[0, r, :, p * LANES:(p + 1) * LANES].astype(F32)
        return jnp.concatenate([buf[p] for p in range(n_blocks)], axis=1)

    dils = [dil for _, dil in DIL_GROUPS]
    l1, l2, l3 = (token_order(ref, dil) for ref, dil in zip((l1_ref, l2_ref, l3_ref), dils))
    m = jnp.maximum(jnp.maximum(l1, l2), l3)
    e1, e2, e3 = jnp.exp(l1 - m), jnp.exp(l2 - m), jnp.exp(l3 - m)
    inv = 1.0 / (e1 + e2 + e3)
    w1, w2, w3 = (_dot((e * inv).astype(BF16), expand_ref[...]) for e in (e1, e2, e3))
    o1, o2, o3 = (token_order(ref, dil) for ref, dil in zip((o1_ref, o2_ref, o3_ref), dils))
    o = w1 * o1 + w2 * o2 + w3 * o3
    y = _post_ln(x_ref[...], _dot(o.astype(BF16), w_ref[...]), g_ref[...], b_ref[...])
    xo_ref[...] = y
    xbo_ref[...] = y.astype(BF16)


def _dil_layer(x, xb, batch, w_qkv, w_o, ln_g, ln_b, tm):
    t, d = x.shape
    seq = t // batch
    ng = len(DIL_GROUPS)
    hd = DIL_HEADS * DIL_HEAD_DIM
    nqkv = ng * 3 * hd
    col_kind = (jnp.arange(nqkv) // hd) % 3
    w_s = (w_qkv * jnp.where(col_kind == 0, DIL_HEAD_DIM ** -0.5, 1.0)[None, :]).astype(BF16)
    tps = seq // tm
    bk = DIL_BLOCK
    outs, lses = [], []
    for gi, (window, dil) in enumerate(DIL_GROUPS):
        assert window // dil == bk and seq % (dil * bk * DIL_STEP_BLOCKS) == 0 and tm % (16 * dil) == 0
        qkv = pl.pallas_call(
            functools.partial(_dil_qkv_kernel, dil=dil),
            grid=(3, t // tm),
            in_specs=[pl.BlockSpec((tm, d), lambda c, i: (i, 0)),
                      pl.BlockSpec((d, hd), lambda c, i, gi=gi: (0, gi * 3 + c))],
            out_specs=pl.BlockSpec((1, dil, tm // dil, hd), lambda c, i: (i // tps, 0, i % tps, c)),
            out_shape=jax.ShapeDtypeStruct((batch, dil, seq // dil, 3 * hd), BF16),
            scratch_shapes=[pltpu.VMEM((hd // LANES, tm, LANES), F32)],
            compiler_params=_params("parallel", "parallel"),
            name=f"dil_qkv_d{dil}",
        )(xb, w_s)

        n_sub = DIL_STEP_BLOCKS

        def spec(kind, prev, n_sub=n_sub):
            if prev:
                return pl.BlockSpec((1, 1, bk, hd), lambda b, r, n: (b, r, jnp.maximum(n * n_sub - 1, 0), kind))
            return pl.BlockSpec((1, 1, n_sub * bk, hd), lambda b, r, n: (b, r, n, kind))

        out_specs = [pl.BlockSpec((1, 1, n_sub * bk, width), lambda b, r, n: (b, r, n, 0)) for width in (hd, LANES)]
        o, lse = pl.pallas_call(
            functools.partial(_dil_attn_kernel, n_sub=n_sub),
            grid=(batch, dil, seq // (dil * bk * n_sub)),
            in_specs=[spec(0, False), spec(1, True), spec(1, False), spec(2, True), spec(2, False)],
            out_specs=out_specs,
            out_shape=[jax.ShapeDtypeStruct((batch, dil, seq // dil, hd), BF16),
                       jax.ShapeDtypeStruct((batch, dil, seq // dil, LANES), F32)],
            compiler_params=_params("parallel", "parallel", "arbitrary"),
            name=f"dil_attn_d{dil}",
        )(qkv, qkv, qkv, qkv, qkv)
        outs.append(o)
        lses.append(lse)

    def res_specs(width):
        return [pl.BlockSpec((1, dil, tm // dil, width), lambda i: (i // tps, 0, i % tps, 0)) for _, dil in DIL_GROUPS]

    expand = (jnp.arange(LANES)[:, None] == DIL_LSE_LANES * (jnp.arange(hd) // DIL_HEAD_DIM)[None, :]).astype(BF16)
    n_strided = sum(dil > 1 for _, dil in DIL_GROUPS)
    bufs = [pltpu.VMEM((hd // LANES, tm, LANES), F32)] * n_strided + [pltpu.VMEM((1, tm, LANES), F32)] * n_strided
    return pl.pallas_call(
        _dil_out_kernel,
        grid=(t // tm,),
        in_specs=res_specs(hd) + res_specs(LANES) + [_resident((LANES, hd)), _resident((hd, d)), _rows(tm, d),
                                                     _resident((1, d)), _resident((1, d))],
        out_specs=[_rows(tm, d), _rows(tm, d)],
        out_shape=[jax.ShapeDtypeStruct((t, d), F32), jax.ShapeDtypeStruct((t, d), BF16)],
        scratch_shapes=bufs,
        compiler_params=_params("parallel"),
        name="dil_out",
    )(*outs, *lses, expand, w_o.astype(BF16), x, ln_g.reshape(1, d), ln_b.reshape(1, d))


def _rwkv_pre_kernel(x_ref, halo_ref, mu_ref, wrkv_ref, w0_ref, w1_ref, w2_ref, a0_ref, a1_ref, a2_ref,
                     g1_ref, g2_ref, kk_ref, ka_ref, bd_ref,
                     r_o, k_o, v_o, kkn_o, a_o, lw_o, g_o, *, tiles_per_seq):
    first = (pl.program_id(0) % tiles_per_seq) == 0
    x = x_ref[...]
    tm = x.shape[0]
    prev_row = jnp.where(first, 0.0, halo_ref[7:8, :])
    row = lax.broadcasted_iota(jnp.int32, (tm, 1), 0)
    xx = jnp.where(row == 0, prev_row, pltpu.roll(x, 1, axis=0)) - x
    mu = mu_ref[...]

    def mix(j):
        return (x + xx * mu[j:j + 1, :]).astype(BF16)

    r = _dot(mix(0), wrkv_ref[0])
    k = _dot(mix(1), wrkv_ref[1])
    v = _dot(mix(2), wrkv_ref[2])
    z = w0_ref[...] + _dot(jnp.tanh(_dot(mix(3), w1_ref[...])).astype(BF16), w2_ref[...])
    nz = -z
    softplus = jnp.maximum(nz, 0.0) + jnp.log(1.0 + jnp.exp(-jnp.abs(nz)))
    lw = -jnp.exp(-softplus - 0.5)
    a = jax.nn.sigmoid(a0_ref[...] + _dot(_dot(mix(4), a1_ref[...]).astype(BF16), a2_ref[...]))
    g = _dot(jax.nn.sigmoid(_dot(mix(5), g1_ref[...])).astype(BF16), g2_ref[...])
    kk = k * kk_ref[...]
    norm = jnp.sqrt(_head_sums(kk * kk, bd_ref[...]))
    r_o[...] = r
    k_o[...] = k * (1.0 + (a - 1.0) * ka_ref[...])
    v_o[...] = v.astype(v_o.dtype)
    kkn_o[...] = kk / jnp.maximum(norm, 1e-12)
    a_o[...] = a
    lw_o[...] = lw
    g_o[...] = g.astype(g_o.dtype)


def _rwkv_chunk_kernel(r_ref, k_ref, v_ref, kk_ref, a_ref, lw_ref, o_ref, s_ref, *, n_pairs, n_chunks):
    c = RWKV_CHUNK

    @pl.when(pl.program_id(2) == 0)
    def _():
        s_ref[...] = jnp.zeros_like(s_ref)

    row = lax.broadcasted_iota(jnp.int32, (c, LANES), 0)
    col = lax.broadcasted_iota(jnp.int32, (c, LANES), 1) % RWKV_HEAD
    blk = {}
    size = RWKV_INV_BASE
    while size < c:
        blk[size] = (row // size) == (col // size)
        size *= 2
    strict = col < row
    incl = col <= row
    eye = jnp.where(row == col, 1.0, 0.0).astype(F32)
    same_head = (lax.broadcasted_iota(jnp.int32, (LANES, LANES), 0) // RWKV_HEAD
                 == lax.broadcasted_iota(jnp.int32, (LANES, LANES), 1) // RWKV_HEAD)
    tri = jnp.where(lax.broadcasted_iota(jnp.int32, (c, c), 1) <= lax.broadcasted_iota(jnp.int32, (c, c), 0),
                    1.0, 0.0).astype(BF16)
    lane_lo = lax.broadcasted_iota(jnp.int32, (1, LANES), 1) < RWKV_HEAD

    def stack(x):
        x = x.astype(BF16)
        zero = jnp.zeros_like(x)
        return jnp.concatenate([jnp.where(lane_lo, x, zero), jnp.where(lane_lo, zero, x)], axis=0)

    def bf(x):
        return x.astype(BF16)

    streams = [(ch, p) for ch in range(n_chunks) for p in range(n_pairs)]

    def load(ref):
        return [ref[ch * c:(ch + 1) * c, p * LANES:(p + 1) * LANES] for ch, p in streams]

    rc, kc, vc, kkc, ac, lwc = (load(ref) for ref in (r_ref, k_ref, v_ref, kk_ref, a_ref, lw_ref))

    cum = []
    for x in lwc:
        cs = _dot(tri, jnp.concatenate(_split_bf16(x, 3), axis=1))
        cum.append(cs[:, :LANES] + cs[:, LANES:2 * LANES] + cs[:, 2 * LANES:])
    p_incl = [jnp.exp(x) for x in cum]
    p_inv = [jnp.exp(-x) for x in cum]
    p_end = [x[c - 1:c, :] for x in p_incl]
    at = [bf(-kk * jnp.exp(cm - lw)) for kk, cm, lw in zip(kkc, cum, lwc)]
    bt = [bf(kk * a * pi) for kk, a, pi in zip(kkc, ac, p_inv)]
    kt = [bf(k * pi) for k, pi in zip(kc, p_inv)]
    rt = [r * pf for r, pf in zip(rc, p_incl)]
    v_st = [stack(x) for x in vc]

    prod = [_dot_nt(jnp.concatenate([a, bf(r)], axis=0), jnp.concatenate([stack(b), stack(k)], axis=0))
            for a, r, b, k in zip(at, rt, bt, kt)]
    l_ab = [jnp.where(strict, x[:c, :LANES], 0.0) for x in prod]
    l_ak = [bf(jnp.where(strict, x[:c, LANES:], 0.0)) for x in prod]
    m_rb = [bf(jnp.where(incl, x[c:, :LANES], 0.0)) for x in prod]
    m_rk = [bf(jnp.where(incl, x[c:, LANES:], 0.0)) for x in prod]

    dg = [bf(jnp.where(blk[RWKV_INV_BASE], x, 0.0)) for x in l_ab]
    d2 = [bf(_dot(x, stack(x))) for x in dg]
    d4 = [_dot(x, stack(x)) for x in d2]
    tinv = [eye + x.astype(F32) for x in dg]
    tinv = [t + _dot(bf(t), stack(y)) for t, y in zip(tinv, d2)]
    tinv = [t + _dot(bf(t), stack(y)) for t, y in zip(tinv, d4)]
    size = RWKV_INV_BASE
    while size < c:
        joins = jnp.logical_not(blk[size]) if 2 * size == c else blk[2 * size] & jnp.logical_not(blk[size])
        off = [jnp.where(joins, x, 0.0) for x in l_ab]
        te = [_dot(bf(t), stack(e)) for t, e in zip(tinv, off)]
        tinv = [t + _dot(bf(x), stack(t)) for t, x in zip(tinv, te)]
        size *= 2

    lakv = [_dot(x, v) for x, v in zip(l_ak, v_st)]
    gu = [_dot(bf(t), jnp.concatenate([stack(a), stack(y)], axis=1)) for t, a, y in zip(tinv, at, lakv)]
    g_nat = [x[:, :LANES] for x in gu]
    uv_nat = [x[:, LANES:] for x in gu]
    mg = [_dot(m, jnp.concatenate([stack(g), stack(u)], axis=1)) for m, g, u in zip(m_rb, g_nat, uv_nat)]
    rg = [bf(r + x[:, :LANES]) for r, x in zip(rt, mg)]
    o_v = [x[:, LANES:] + _dot(m, v) for x, m, v in zip(mg, m_rk, v_st)]
    q_mat = [bf(jnp.where(same_head, _dot(bf(g.T), b), 0.0) * pe) for g, b, pe in zip(g_nat, bt, p_end)]
    n_mat = [jnp.where(same_head, _dot(bf(jnp.concatenate([u, v.astype(F32)], axis=0).T),
                                       jnp.concatenate([b, k], axis=0)), 0.0) * pe
             for u, v, b, k, pe in zip(uv_nat, vc, bt, kt, p_end)]

    state = [s_ref[p] for p in range(n_pairs)]
    for i, (ch, p) in enumerate(streams):
        sb = bf(state[p])
        o_ref[ch * c:(ch + 1) * c, p * LANES:(p + 1) * LANES] = _dot_nt(rg[i], sb) + o_v[i]
        state[p] = state[p] * p_end[i] + _dot(sb, q_mat[i]) + n_mat[i]
    for p in range(n_pairs):
        s_ref[p] = state[p]


def _rwkv_post_kernel(o_ref, r_ref, k_ref, v_ref, g_ref, x_ref, lnw_ref, lnb_ref, rk_ref, bd_ref, w_ref,
                      gl_ref, bl_ref, xo_ref, xbo_ref):
    bd = bd_ref[...]
    o = o_ref[...]
    mean = _head_sums(o, bd) * (1.0 / RWKV_HEAD)
    dlt = o - mean
    var = _head_sums(dlt * dlt, bd) * (1.0 / RWKV_HEAD)
    y = dlt * lax.rsqrt(var + RWKV_GN_EPS) * lnw_ref[...] + lnb_ref[...]
    bonus = _head_sums(r_ref[...] * k_ref[...] * rk_ref[...], bd) * v_ref[...]
    out = ((y + bonus) * g_ref[...]).astype(BF16)
    res = _post_ln(x_ref[...], _dot(out, w_ref[...]), gl_ref[...], bl_ref[...])
    xo_ref[...] = res
    xbo_ref[...] = res.astype(BF16)


def _rwkv_layer(x, batch, mu, w_rkv, w0, w1, w2, a0, a1, a2, g1, g2, k_k, k_a, r_k, ln_w, ln_b, w_o,
                ln_g_post, ln_b_post, tm):
    t, d = x.shape
    seq = t // batch
    tmr = min(256, seq)
    vec = lambda p: p.reshape(1, d)
    idx = jnp.arange(LANES) // RWKV_HEAD
    ones_bd = (idx[:, None] == idx[None, :]).astype(BF16)
    tok = [jax.ShapeDtypeStruct((t, d), dt) for dt in (F32, F32, BF16, F32, F32, F32, BF16)]
    halo = pl.BlockSpec((8, d), lambda i: (jnp.maximum(i * (tmr // 8) - 1, 0), 0))
    r, k, v, kkn, a, lw, g = pl.pallas_call(
        functools.partial(_rwkv_pre_kernel, tiles_per_seq=seq // tmr),
        grid=(t // tmr,),
        in_specs=[_rows(tmr, d), halo, _resident((6, d)), _resident((3, d, d)),
                  _resident((1, d)), _resident(w1.shape), _resident(w2.shape),
                  _resident((1, d)), _resident(a1.shape), _resident(a2.shape),
                  _resident(g1.shape), _resident(g2.shape), _resident((1, d)), _resident((1, d)),
                  _resident((LANES, LANES))],
        out_specs=[_rows(tmr, d)] * 7,
        out_shape=tok,
        compiler_params=_params("parallel"),
        name="rwkv_pre",
    )(x, x, mu, w_rkv.astype(BF16), vec(w0), w1.astype(BF16), w2.astype(BF16), vec(a0), a1.astype(BF16),
      a2.astype(BF16), g1.astype(BF16), g2.astype(BF16), vec(k_k), vec(k_a), ones_bd)

    tc = min(RWKV_STEP_CHUNKS * RWKV_CHUNK, seq)
    nt = seq // tc
    width = RWKV_STEP_PAIRS * LANES
    pair = pl.BlockSpec((tc, width), lambda b, p, i: (b * nt + i, p))
    o = pl.pallas_call(
        functools.partial(_rwkv_chunk_kernel, n_pairs=RWKV_STEP_PAIRS, n_chunks=tc // RWKV_CHUNK),
        grid=(batch, d // width, nt),
        in_specs=[pair] * 6,
        out_specs=pair,
        out_shape=jax.ShapeDtypeStruct((t, d), F32),
        scratch_shapes=[pltpu.VMEM((RWKV_STEP_PAIRS, LANES, LANES), F32)],
        compiler_params=_params("parallel", "parallel", "arbitrary"),
        name="rwkv_chunk",
    )(r, k, v, kkn, a, lw)

    return pl.pallas_call(
        _rwkv_post_kernel,
        grid=(t // tm,),
        in_specs=[_rows(tm, d)] * 6 + [_resident((1, d))] * 3 + [_resident((LANES, LANES)), _resident((d, d)),
                                                                _resident((1, d)), _resident((1, d))],
        out_specs=[_rows(tm, d), _rows(tm, d)],
        out_shape=[jax.ShapeDtypeStruct((t, d), F32), jax.ShapeDtypeStruct((t, d), BF16)],
        compiler_params=_params("parallel"),
        name="rwkv_post",
    )(o, r, k, v, g, x, vec(ln_w), vec(ln_b), vec(r_k), ones_bd, w_o.astype(BF16), vec(ln_g_post), vec(ln_b_post))


def _ffn_kernel(x_ref, xb_ref, halo_ref, wa_ref, wb_ref, cw_ref, cb_ref, wo_ref, g_ref, b_ref,
                xo_ref, xbo_ref, acc_ref, *, tiles_per_seq, n_chunks, halo_rows):
    first = (pl.program_id(0) % tiles_per_seq) == 0
    xb = xb_ref[...]
    tm = xb.shape[0]
    halo = halo_ref[...]
    halo = jnp.where(first, jnp.zeros_like(halo), halo)
    xe = jnp.concatenate([halo, xb], axis=0)
    acc_ref[...] = jnp.zeros_like(acc_ref)

    def up(c):
        return _dot(xe, wa_ref[c]), _dot(xb, wb_ref[c])

    def down(c, ha, hb):
        cw = cw_ref[c]
        conv = (cw[0:1, :] * ha[halo_rows - 2:halo_rows - 2 + tm, :]
                + cw[1:2, :] * ha[halo_rows - 1:halo_rows - 1 + tm, :]
                + cw[2:3, :] * ha[halo_rows:, :] + cb_ref[c])
        act = conv * jax.nn.sigmoid(conv) * hb
        acc_ref[...] += _dot(act.astype(BF16), wo_ref[c])

    h = up(0)
    for c in range(n_chunks):
        h_next = up(c + 1) if c + 1 < n_chunks else None
        down(c, *h)
        h = h_next
    y = _post_ln(x_ref[...], acc_ref[...], g_ref[...], b_ref[...])
    xo_ref[...] = y
    xbo_ref[...] = y.astype(BF16)


def _ffn_layer(x, xb, batch, w_in, conv_w, conv_b, w_out, ln_g, ln_b, tm):
    t, d = x.shape
    seq = t // batch
    nch = D_FF_PAD // FF_CHUNK
    padc = D_FF_PAD - D_FF
    chunks = lambda w: jnp.pad(w, ((0, 0), (0, padc))).reshape(w.shape[0], nch, FF_CHUNK).transpose(1, 0, 2)
    wa = chunks(w_in[:, :D_FF]).astype(BF16)
    wb = chunks(w_in[:, D_FF:]).astype(BF16)
    cw = chunks(conv_w)
    cb = chunks(conv_b.reshape(1, D_FF))
    wo = jnp.pad(w_out, ((0, padc), (0, 0))).reshape(nch, FF_CHUNK, d).astype(BF16)
    halo_rows = 16
    halo = pl.BlockSpec((halo_rows, d), lambda i: (jnp.maximum(i * (tm // halo_rows) - 1, 0), 0))
    return pl.pallas_call(
        functools.partial(_ffn_kernel, tiles_per_seq=seq // tm, n_chunks=nch, halo_rows=halo_rows),
        grid=(t // tm,),
        in_specs=[_rows(tm, d), _rows(tm, d), halo, _resident(wa.shape), _resident(wb.shape), _resident(cw.shape),
                  _resident(cb.shape), _resident(wo.shape), _resident((1, d)), _resident((1, d))],
        out_specs=[_rows(tm, d), _rows(tm, d)],
        out_shape=[jax.ShapeDtypeStruct((t, d), F32), jax.ShapeDtypeStruct((t, d), BF16)],
        scratch_shapes=[pltpu.VMEM((tm, d), F32)],
        compiler_params=_params("parallel"),
        name="conv_ffn",
    )(x, xb, xb, wa, wb, cw, cb, wo, ln_g.reshape(1, d), ln_b.reshape(1, d))


def kernel(x, positions, ln_g, ln_b, mla_w_down, mla_q_norm, mla_kv_norm, mla_w_uq, mla_w_ukv, mla_w_o,
           dil_w_qkv, dil_w_o, rwkv_mu, rwkv_w_rkv, rwkv_w0, rwkv_w1, rwkv_w2, rwkv_a0, rwkv_a1, rwkv_a2,
           rwkv_g1, rwkv_g2, rwkv_k_k, rwkv_k_a, rwkv_r_k, rwkv_ln_w, rwkv_ln_b, rwkv_w_o,
           ffn_w_in, ffn_conv_w, ffn_conv_b, ffn_w_out):
    batch, seq, d = x.shape
    t = batch * seq
    tm = min(512, seq)
    xf = x.reshape(t, d)
    xb = xf.astype(BF16)
    cos, sin = _rope_tables(positions, tm)
    ia = ib = ic = 0
    for i in range(DEPTH):
        kind = i % 3
        if kind == 0:
            xf, xb = _mla_layer(xf, xb, cos, sin, batch, mla_w_down[ia], mla_q_norm[ia], mla_kv_norm[ia],
                                mla_w_uq[ia], mla_w_ukv[ia], mla_w_o[ia], ln_g[i, 0], ln_b[i, 0], tm)
            ia += 1
        elif kind == 1:
            xf, xb = _dil_layer(xf, xb, batch, dil_w_qkv[ib], dil_w_o[ib], ln_g[i, 0], ln_b[i, 0], tm)
            ib += 1
        else:
            xf, xb = _rwkv_layer(xf, batch, rwkv_mu[ic], rwkv_w_rkv[ic], rwkv_w0[ic], rwkv_w1[ic], rwkv_w2[ic],
                                 rwkv_a0[ic], rwkv_a1[ic], rwkv_a2[ic], rwkv_g1[ic], rwkv_g2[ic],
                                 rwkv_k_k[ic], rwkv_k_a[ic], rwkv_r_k[ic], rwkv_ln_w[ic], rwkv_ln_b[ic],
                                 rwkv_w_o[ic], ln_g[i, 0], ln_b[i, 0], tm)
            ic += 1
        xf, xb = _ffn_layer(xf, xb, batch, ffn_w_in[i], ffn_conv_w[i], ffn_conv_b[i], ffn_w_out[i],
                            ln_g[i, 1], ln_b[i, 1], tm)
    return xf.reshape(batch, seq, d)
```

```python
import functools

import jax
import jax.numpy as jnp
from jax import lax
from jax.experimental import pallas as pl
from jax.experimental.pallas import tpu as pltpu

F32 = jnp.float32
BF16 = jnp.bfloat16

D_MODEL = 1024
DEPTH = 4
MLA_HEADS = 16
MLA_Q_LORA = 384
MLA_KV_LORA = 256
MLA_NOPE = 64
MLA_ROPE = 32
MLA_V = 64
ROPE_THETA = 10000.0
DIL_GROUPS = ((128, 1), (512, 4), (2048, 16))
DIL_HEADS = 16
DIL_HEAD_DIM = 64
DIL_BLOCK = 128
DIL_STEP_BLOCKS = 2
DIL_STAGE_PAIRS = 4
DIL_LSE_LANES = 128 // DIL_HEADS
RWKV_HEAD = 64
RWKV_GN_EPS = 64e-5
D_FF = 2752
ALPHA = (2 * DEPTH) ** 0.25
LN_EPS = 1e-5
RMS_EPS = 1e-6
NEG_INF = -1e30
LOG2_E = 1.4426950408889634

LANES = 128
VMEM_LIMIT_BYTES = 56 * 2**20

SLOT = LANES
ROPE_LO = MLA_NOPE
ROPE_HALF = MLA_ROPE // 2
MLA_STEP_PAIRS = 4
FF_CHUNK = 256
D_FF_PAD = -(-D_FF // FF_CHUNK) * FF_CHUNK
RWKV_CHUNK = 64
RWKV_INV_BASE = 8
RWKV_STEP_PAIRS = 8
RWKV_STEP_CHUNKS = 2


def _params(*sem):
    return pltpu.CompilerParams(dimension_semantics=sem, vmem_limit_bytes=VMEM_LIMIT_BYTES)


def _dot(a, b):
    return jnp.dot(a, b, preferred_element_type=F32)


def _dot_nt(a, b):
    return lax.dot_general(a, b, (((1,), (1,)), ((), ())), preferred_element_type=F32)


def _resident(shape):
    nd = len(shape)
    return pl.BlockSpec(shape, lambda *_: (0,) * nd, pipeline_mode=pl.Buffered(1))


def _rows(tm, width):
    return pl.BlockSpec((tm, width), lambda i: (i, 0))


def _post_ln(x, h, g, b):
    y = ALPHA * x + h
    mu = jnp.mean(y, axis=-1, keepdims=True)
    d = y - mu
    var = jnp.mean(d * d, axis=-1, keepdims=True)
    return d * lax.rsqrt(var + LN_EPS) * g + b


def _rms(x, g):
    return x * lax.rsqrt(jnp.mean(x * x, axis=-1, keepdims=True) + RMS_EPS) * g


def _split_bf16(x, parts):
    out = []
    for _ in range(parts - 1):
        hi = x.astype(BF16)
        out.append(hi)
        x = x - hi.astype(F32)
    out.append(x.astype(BF16))
    return out


def _head_sums(x, ones_bd):
    cols = []
    for p in range(x.shape[1] // LANES):
        xs = x[:, p * LANES:(p + 1) * LANES]
        hi, lo = _split_bf16(xs, 2)
        cols.append(_dot(hi, ones_bd) + _dot(lo, ones_bd))
    return jnp.concatenate(cols, axis=1)


def _rope_table_kernel(pos_ref, freq_ref, cos_ref, sin_ref):
    ang = pos_ref[...].astype(F32) * freq_ref[...]
    cos_ref[...] = jnp.cos(ang)
    sin_ref[...] = jnp.sin(ang)


def _rope_tables(positions, tm):
    t = positions.size
    inv_freq = ROPE_THETA ** (-jnp.arange(0, MLA_ROPE, 2, dtype=F32) / MLA_ROPE)
    freq = jnp.zeros((1, SLOT), F32)
    freq = freq.at[0, ROPE_LO:ROPE_LO + ROPE_HALF].set(inv_freq)
    freq = freq.at[0, ROPE_LO + ROPE_HALF:ROPE_LO + MLA_ROPE].set(inv_freq)
    return pl.pallas_call(
        _rope_table_kernel,
        grid=(t // tm,),
        in_specs=[_rows(tm, 1), _resident((1, SLOT))],
        out_specs=[_rows(tm, SLOT), _rows(tm, SLOT)],
        out_shape=[jax.ShapeDtypeStruct((t, SLOT), F32)] * 2,
        compiler_params=_params("parallel"),
        name="rope_tables",
    )(positions.reshape(t, 1), freq)


def _rope_slots(z, cos, sin):
    width = z.shape[1]
    lane = lax.broadcasted_iota(jnp.int32, (1, width), 1) % SLOT
    first = (lane >= ROPE_LO) & (lane < ROPE_LO + ROPE_HALF)
    second = (lane >= ROPE_LO + ROPE_HALF) & (lane < ROPE_LO + MLA_ROPE)
    from_left = pltpu.roll(z, ROPE_HALF, axis=1)
    from_right = pltpu.roll(z, width - ROPE_HALF, axis=1)
    partner = jnp.where(second, from_left, jnp.where(first, -from_right, 0.0))
    return z * cos + partner * sin


def _rope_slots_partnered(z, cos, sin):
    width = z.shape[1]
    return z * cos + pltpu.roll(z, width - MLA_ROPE, axis=1) * sin


def _mla_qkv_kernel(xb_ref, wd_ref, qn_ref, kvn_ref, wq_ref, wkv_ref, cos_ref, sin_ref, q_ref, k_ref, vt_ref, *, scale):
    cos, sin = cos_ref[...], sin_ref[...]
    lat = _dot(xb_ref[...], wd_ref[...])
    cq = _rms(lat[:, :MLA_Q_LORA], qn_ref[...]).astype(BF16)
    ckv = _rms(lat[:, MLA_Q_LORA:MLA_Q_LORA + MLA_KV_LORA], kvn_ref[...]).astype(BF16)
    kpe = _rope_slots(lat[:, MLA_Q_LORA + MLA_KV_LORA:], cos, sin)
    q = _dot(cq, wq_ref[...])
    q_ref[...] = (_rope_slots_partnered(q, jnp.tile(cos, (1, MLA_HEADS)), jnp.tile(sin, (1, MLA_HEADS)))
                  * scale).astype(BF16)
    kv = _dot(ckv, wkv_ref[...])
    kw = MLA_HEADS * SLOT
    k_ref[...] = (kv[:, :kw] + jnp.tile(kpe, (1, MLA_HEADS))).astype(BF16)
    vt_ref[0] = kv[:, kw:].T.astype(BF16)


def _mla_attn_kernel(q_ref, k_ref, vt_ref, o_ref, *, tq, tk):
    assert tq == tk
    n_full = pl.program_id(2)
    heads = tuple(range(q_ref.shape[1] // SLOT))
    slots = tuple(slice(hh * SLOT, (hh + 1) * SLOT) for hh in heads)

    def own_rows(hh, width):
        sub = lax.broadcasted_iota(jnp.int32, (LANES, width), 0)
        return sub < MLA_V if hh % 2 == 0 else sub >= MLA_V

    def causal(keys, queries):
        return (lax.broadcasted_iota(jnp.int32, (keys, queries), 0)
                <= lax.broadcasted_iota(jnp.int32, (keys, queries), 1))

    def scores(j):
        start = pl.multiple_of(j * tk, tk)
        return tuple(_dot_nt(k_ref[pl.ds(start, tk), slot], q_ref[:, slot]) for slot in slots)

    def accumulate(s, state, vt, hh):
        m, acc = state
        m_new = jnp.maximum(m, jnp.max(s, axis=0, keepdims=True))
        alpha = jnp.exp2(m - m_new)
        p = jnp.exp2(s - m_new).astype(BF16)
        vt = jnp.where(own_rows(hh, vt.shape[1]), vt, jnp.ones(vt.shape, BF16))
        return m_new, alpha * acc + _dot(vt, p)

    def update(j, s_heads, states):
        return tuple(accumulate(s, st, vt_ref[j, hh // 2 * LANES:(hh // 2 + 1) * LANES, :], hh)
                     for hh, s, st in zip(heads, s_heads, states))

    def body(j, states):
        return update(j, scores(j), states)

    def body2(j2, states):
        s_even, s_odd = scores(2 * j2), scores(2 * j2 + 1)
        return update(2 * j2 + 1, s_odd, update(2 * j2, s_even, states))

    init = (jnp.full((1, tq), NEG_INF, F32), jnp.zeros((LANES, tq), F32))
    states = lax.fori_loop(0, n_full // 2, body2, (init,) * len(heads))
    states = lax.fori_loop(n_full - n_full % 2, n_full, body, states)
    s_diag = tuple(jnp.where(causal(tk, tq), s, NEG_INF) for s in scores(n_full))
    accs = [acc for _, acc in update(n_full, s_diag, states)]
    for pp in range(len(heads) // 2):
        acc0, acc1 = accs[2 * pp], accs[2 * pp + 1]
        out_t = jnp.where(own_rows(0, tq), acc0 / acc0[MLA_V:MLA_V + 1, :], acc1 / acc1[0:1, :])
        o_ref[:, pp * LANES:(pp + 1) * LANES] = out_t.T.astype(o_ref.dtype)


def _proj_ln_kernel(a_ref, w_ref, x_ref, g_ref, b_ref, xo_ref, xbo_ref):
    y = _post_ln(x_ref[...], _dot(a_ref[...], w_ref[...]), g_ref[...], b_ref[...])
    xo_ref[...] = y
    xbo_ref[...] = y.astype(BF16)


def _proj_ln(a, w, x, g, b, tm):
    t, d = x.shape
    return pl.pallas_call(
        _proj_ln_kernel,
        grid=(t // tm,),
        in_specs=[_rows(tm, a.shape[1]), _resident(w.shape), _rows(tm, d), _resident((1, d)), _resident((1, d))],
        out_specs=[_rows(tm, d), _rows(tm, d)],
        out_shape=[jax.ShapeDtypeStruct((t, d), F32), jax.ShapeDtypeStruct((t, d), BF16)],
        compiler_params=_params("parallel"),
        name="proj_ln",
    )(a, w, x, g.reshape(1, d), b.reshape(1, d))


def _mla_layer(x, xb, cos, sin, batch, w_down, q_norm, kv_norm, w_uq, w_ukv, w_o, ln_g, ln_b, tm):
    t, d = x.shape
    seq = t // batch
    h = MLA_HEADS
    pad = jnp.zeros((d, ROPE_LO), F32)
    w_down_s = jnp.concatenate(
        [w_down[:, :MLA_Q_LORA + MLA_KV_LORA], pad, w_down[:, MLA_Q_LORA + MLA_KV_LORA:],
         jnp.zeros((d, SLOT - ROPE_LO - MLA_ROPE), F32)], axis=1).astype(BF16)
    w_uq3 = w_uq.reshape(MLA_Q_LORA, h, MLA_NOPE + MLA_ROPE)
    w_partner = jnp.concatenate([-w_uq3[:, :, MLA_NOPE + ROPE_HALF:], w_uq3[:, :, MLA_NOPE:MLA_NOPE + ROPE_HALF]], axis=2)
    w_uq_s = jnp.concatenate([w_uq3, w_partner], axis=2).reshape(MLA_Q_LORA, h * SLOT).astype(BF16)
    w_ukv3 = w_ukv.reshape(MLA_KV_LORA, h, MLA_NOPE + MLA_V)
    w_k_s = jnp.pad(w_ukv3[:, :, :MLA_NOPE], ((0, 0), (0, 0), (0, SLOT - MLA_NOPE))).reshape(MLA_KV_LORA, h * SLOT)
    w_v_s = w_ukv3[:, :, MLA_NOPE:].reshape(MLA_KV_LORA, h * MLA_V)
    w_kv_s = jnp.concatenate([w_k_s, w_v_s], axis=1).astype(BF16)

    assert SLOT == MLA_NOPE + 2 * MLA_ROPE
    tq = tk = tm
    nq = seq // tq
    q, k, vt = pl.pallas_call(
        functools.partial(_mla_qkv_kernel, scale=(MLA_NOPE + MLA_ROPE) ** -0.5 * LOG2_E),
        grid=(t // tm,),
        in_specs=[_rows(tm, d), _resident(w_down_s.shape), _resident((1, MLA_Q_LORA)), _resident((1, MLA_KV_LORA)),
                  _resident(w_uq_s.shape), _resident(w_kv_s.shape), _rows(tm, SLOT), _rows(tm, SLOT)],
        out_specs=[_rows(tm, h * SLOT), _rows(tm, h * SLOT), pl.BlockSpec((1, h * MLA_V, tm), lambda i: (i, 0, 0))],
        out_shape=[jax.ShapeDtypeStruct((t, h * SLOT), BF16), jax.ShapeDtypeStruct((t, h * SLOT), BF16),
                   jax.ShapeDtypeStruct((t // tm, h * MLA_V, tm), BF16)],
        compiler_params=_params("parallel"),
        name="mla_qkv",
    )(xb, w_down_s, q_norm.reshape(1, -1), kv_norm.reshape(1, -1), w_uq_s, w_kv_s, cos, sin)

    hps = 2 * MLA_STEP_PAIRS
    o = pl.pallas_call(
        functools.partial(_mla_attn_kernel, tq=tq, tk=tk),
        grid=(batch, h // hps, nq),
        in_specs=[pl.BlockSpec((tq, hps * SLOT), lambda b, p, i: (b * nq + i, p)),
                  pl.BlockSpec((seq, hps * SLOT), lambda b, p, i: (b, p)),
                  pl.BlockSpec((nq, hps * MLA_V, tk), lambda b, p, i: (b, p, 0))],
        out_specs=pl.BlockSpec((tq, hps * MLA_V), lambda b, p, i: (b * nq + i, p)),
        out_shape=jax.ShapeDtypeStruct((t, h * MLA_V), BF16),
        compiler_params=_params("parallel", "parallel", "arbitrary"),
        name="mla_attn",
    )(q, k, vt)

    return _proj_ln(o, w_o.astype(BF16), x, ln_g, ln_b, tm)


def _dil_qkv_kernel(x_ref, w_ref, o_ref, res_ref, *, dil):
    res = _dot(x_ref[...], w_ref[...])
    if dil == 1:
        o_ref[0, 0] = res.astype(o_ref.dtype)
        return
    rows = res.shape[0] // dil
    for p in range(res.shape[1] // LANES):
        sl = slice(p * LANES, (p + 1) * LANES)
        res_ref[p] = res[:, sl]
        for r in range(dil):
            o_ref[0, r, :, sl] = res_ref[p, pl.ds(r, rows, stride=dil), :].astype(o_ref.dtype)


def _dil_attn_kernel(q_ref, kp_ref, kc_ref, vp_ref, vc_ref, o_ref, lse_ref, *, n_sub):
    q_ref, kp_ref, kc_ref, vp_ref, vc_ref, o_ref, lse_ref = (
        ref.at[0, 0] for ref in (q_ref, kp_ref, kc_ref, vp_ref, vc_ref, o_ref, lse_ref))
    bk = DIL_BLOCK
    i = lax.broadcasted_iota(jnp.int32, (bk, 2 * bk), 0)
    j = lax.broadcasted_iota(jnp.int32, (bk, 2 * bk), 1)
    dist = i + bk - j
    band = (dist >= 0) & (dist <= bk)
    prev_exists = (pl.program_id(2) > 0) | (j >= bk)
    lse_lane = lax.broadcasted_iota(jnp.int32, (1, LANES), 1)
    lo = lse_lane < DIL_HEAD_DIM
    pairs = [slice(p * LANES, (p + 1) * LANES) for p in range(DIL_HEADS // 2)]
    for sb in range(n_sub):
        rows = slice(sb * bk, (sb + 1) * bk)
        valid = band & prev_exists if sb == 0 else band

        def window(prev_ref, cur_ref, sl, sb=sb):
            if sb == 0:
                return jnp.concatenate([prev_ref[:, sl], cur_ref[:bk, sl]], axis=0)
            return cur_ref[(sb - 1) * bk:(sb + 1) * bk, sl]

        def pair_scores_of(sl):
            q = q_ref[rows, sl]
            k = window(kp_ref, kc_ref, sl)
            zero = jnp.zeros_like(q)
            return [_dot_nt(qh, k) for qh in (jnp.where(lo, q, zero), jnp.where(lo, zero, q))]

        lse_blk = jnp.zeros((bk, LANES), F32)
        scores = {}
        for p, sl in enumerate(pairs):
            if p % DIL_STAGE_PAIRS == 0:
                scores = {pp: pair_scores_of(pairs[pp]) for pp in range(p, min(p + DIL_STAGE_PAIRS, len(pairs)))}
            pair_scores = scores[p]
            v = window(vp_ref, vc_ref, sl)
            res = []
            for s in pair_scores:
                s = jnp.where(valid, s, NEG_INF)
                m = jnp.max(s, axis=-1, keepdims=True)
                e = jnp.exp(s - m)
                l = jnp.sum(e, axis=-1, keepdims=True)
                res.append((_dot(e.astype(BF16), v) / l, m + jnp.log(l)))
            o_ref[rows, sl] = jnp.where(lo, res[0][0], res[1][0]).astype(o_ref.dtype)
            lse_blk = jnp.where(lse_lane // DIL_LSE_LANES == 2 * p, res[0][1],
                                jnp.where(lse_lane // DIL_LSE_LANES == 2 * p + 1, res[1][1], lse_blk))
        lse_ref[rows, :] = lse_blk


def _dil_out_kernel(o1_ref, o2_ref, o3_ref, l1_ref, l2_ref, l3_ref, expand_ref, w_ref, x_ref, g_ref, b_ref,
                    xo_ref, xbo_ref, *bufs):
    bufs = list(bufs)

    def token_order(ref, dil):
        if dil == 1:
            return ref[0, 0].astype(F32)
        n_blocks = ref.shape[3] // LANES
        buf = bufs.pop()
        for p in range(n_blocks):
            for r in range(dil):
                buf[p, pl.ds(r, ref.shape[2], stride=dil), :] = ref[0, r, :, p * LANES:(p + 1) * LANES].astype(F32)
        return jnp.concatenate([buf[p] for p in range(n_blocks)], axis=1)

    dils = [dil for _, dil in DIL_GROUPS]
    l1, l2, l3 = (token_order(ref, dil) for ref, dil in zip((l1_ref, l2_ref, l3_ref), dils))
    m = jnp.maximum(jnp.maximum(l1, l2), l3)
    e1, e2, e3 = jnp.exp(l1 - m), jnp.exp(l2 - m), jnp.exp(l3 - m)
    inv = 1.0 / (e1 + e2 + e3)
    w1, w2, w3 = (_dot((e * inv).astype(BF16), expand_ref[...]) for e in (e1, e2, e3))
    o1, o2, o3 = (token_order(ref, dil) for ref, dil in zip((o1_ref, o2_ref, o3_ref), dils))
    o = w1 * o1 + w2 * o2 + w3 * o3
    y = _post_ln(x_ref[...], _dot(o.astype(BF16), w_ref[...]), g_ref[...], b_ref[...])
    xo_ref[...] = y
    xbo_ref[...] = y.astype(BF16)


def _dil_layer(x, xb, batch, w_qkv, w_o, ln_g, ln_b, tm):
    t, d = x.shape
    seq = t // batch
    ng = len(DIL_GROUPS)
    hd = DIL_HEADS * DIL_HEAD_DIM
    nqkv = ng * 3 * hd
    col_kind = (jnp.arange(nqkv) // hd) % 3
    w_s = (w_qkv * jnp.where(col_kind == 0, DIL_HEAD_DIM ** -0.5, 1.0)[None, :]).astype(BF16)
    tps = seq // tm
    bk = DIL_BLOCK
    outs, lses = [], []
    for gi, (window, dil) in enumerate(DIL_GROUPS):
        assert window // dil == bk and seq % (dil * bk * DIL_STEP_BLOCKS) == 0 and tm % (16 * dil) == 0
        tmq = min(2 * tm, seq)
        tpq = seq // tmq
        qkv = pl.pallas_call(
            functools.partial(_dil_qkv_kernel, dil=dil),
            grid=(3, t // tmq),
            in_specs=[pl.BlockSpec((tmq, d), lambda c, i: (i, 0)),
                      pl.BlockSpec((d, hd), lambda c, i, gi=gi: (0, gi * 3 + c))],
            out_specs=pl.BlockSpec((1, dil, tmq // dil, hd), lambda c, i, tpq=tpq: (i // tpq, 0, i % tpq, c)),
            out_shape=jax.ShapeDtypeStruct((batch, dil, seq // dil, 3 * hd), BF16),
            scratch_shapes=[pltpu.VMEM((hd // LANES, tmq, LANES), F32)],
            compiler_params=_params("parallel", "parallel"),
            name=f"dil_qkv_d{dil}",
        )(xb, w_s)

        n_sub = DIL_STEP_BLOCKS

        def spec(kind, prev, n_sub=n_sub):
            if prev:
                return pl.BlockSpec((1, 1, bk, hd), lambda b, r, n: (b, r, jnp.maximum(n * n_sub - 1, 0), kind))
            return pl.BlockSpec((1, 1, n_sub * bk, hd), lambda b, r, n: (b, r, n, kind))

        out_specs = [pl.BlockSpec((1, 1, n_sub * bk, width), lambda b, r, n: (b, r, n, 0)) for width in (hd, LANES)]
        o, lse = pl.pallas_call(
            functools.partial(_dil_attn_kernel, n_sub=n_sub),
            grid=(batch, dil, seq // (dil * bk * n_sub)),
            in_specs=[spec(0, False), spec(1, True), spec(1, False), spec(2, True), spec(2, False)],
            out_specs=out_specs,
            out_shape=[jax.ShapeDtypeStruct((batch, dil, seq // dil, hd), BF16),
                       jax.ShapeDtypeStruct((batch, dil, seq // dil, LANES), F32)],
            compiler_params=_params("parallel", "parallel", "arbitrary"),
            name=f"dil_attn_d{dil}",
        )(qkv, qkv, qkv, qkv, qkv)
        outs.append(o)
        lses.append(lse)

    def res_specs(width):
        return [pl.BlockSpec((1, dil, tm // dil, width), lambda i: (i // tps, 0, i % tps, 0)) for _, dil in DIL_GROUPS]

    expand = (jnp.arange(LANES)[:, None] == DIL_LSE_LANES * (jnp.arange(hd) // DIL_HEAD_DIM)[None, :]).astype(BF16)
    n_strided = sum(dil > 1 for _, dil in DIL_GROUPS)
    bufs = [pltpu.VMEM((hd // LANES, tm, LANES), F32)] * n_strided + [pltpu.VMEM((1, tm, LANES), F32)] * n_strided
    return pl.pallas_call(
        _dil_out_kernel,
        grid=(t // tm,),
        in_specs=res_specs(hd) + res_specs(LANES) + [_resident((LANES, hd)), _resident((hd, d)), _rows(tm, d),
                                                     _resident((1, d)), _resident((1, d))],
        out_specs=[_rows(tm, d), _rows(tm, d)],
        out_shape=[jax.ShapeDtypeStruct((t, d), F32), jax.ShapeDtypeStruct((t, d), BF16)],
        scratch_shapes=bufs,
        compiler_params=_params("parallel"),
        name="dil_out",
    )(*outs, *lses, expand, w_o.astype(BF16), x, ln_g.reshape(1, d), ln_b.reshape(1, d))


def _rwkv_pre_kernel(x_ref, halo_ref, mu_ref, wrkv_ref, w0_ref, w1_ref, w2_ref, a0_ref, a1_ref, a2_ref,
                     g1_ref, g2_ref, kk_ref, ka_ref, bd_ref,
                     r_o, k_o, v_o, kkn_o, a_o, lw_o, g_o, *, tiles_per_seq):
    first = (pl.program_id(0) % tiles_per_seq) == 0
    x = x_ref[...]
    tm = x.shape[0]
    prev_row = jnp.where(first, 0.0, halo_ref[7:8, :])
    row = lax.broadcasted_iota(jnp.int32, (tm, 1), 0)
    xx = jnp.where(row == 0, prev_row, pltpu.roll(x, 1, axis=0)) - x
    mu = mu_ref[...]

    def mix(j):
        return (x + xx * mu[j:j + 1, :]).astype(BF16)

    r = _dot(mix(0), wrkv_ref[0])
    k = _dot(mix(1), wrkv_ref[1])
    v = _dot(mix(2), wrkv_ref[2])
    z = w0_ref[...] + _dot(jnp.tanh(_dot(mix(3), w1_ref[...])).astype(BF16), w2_ref[...])
    nz = -z
    softplus = jnp.maximum(nz, 0.0) + jnp.log(1.0 + jnp.exp(-jnp.abs(nz)))
    lw = -jnp.exp(-softplus - 0.5)
    a = jax.nn.sigmoid(a0_ref[...] + _dot(_dot(mix(4), a1_ref[...]).astype(BF16), a2_ref[...]))
    g = _dot(jax.nn.sigmoid(_dot(mix(5), g1_ref[...])).astype(BF16), g2_ref[...])
    kk = k * kk_ref[...]
    norm = jnp.sqrt(_head_sums(kk * kk, bd_ref[...]))
    r_o[...] = r
    k_o[...] = k * (1.0 + (a - 1.0) * ka_ref[...])
    v_o[...] = v.astype(v_o.dtype)
    kkn_o[...] = kk / jnp.maximum(norm, 1e-12)
    a_o[...] = a
    lw_o[...] = lw
    g_o[...] = g.astype(g_o.dtype)


def _rwkv_chunk_kernel(r_ref, k_ref, v_ref, kk_ref, a_ref, lw_ref, o_ref, s_ref, *, n_pairs, n_chunks):
    c = RWKV_CHUNK

    @pl.when(pl.program_id(2) == 0)
    def _():
        s_ref[...] = jnp.zeros_like(s_ref)

    row = lax.broadcasted_iota(jnp.int32, (c, LANES), 0)
    col = lax.broadcasted_iota(jnp.int32, (c, LANES), 1) % RWKV_HEAD
    blk = {}
    size = RWKV_INV_BASE
    while size < c:
        blk[size] = (row // size) == (col // size)
        size *= 2
    strict = col < row
    incl = col <= row
    eye = jnp.where(row == col, 1.0, 0.0).astype(F32)
    same_head = (lax.broadcasted_iota(jnp.int32, (LANES, LANES), 0) // RWKV_HEAD
                 == lax.broadcasted_iota(jnp.int32, (LANES, LANES), 1) // RWKV_HEAD)
    tri = jnp.where(lax.broadcasted_iota(jnp.int32, (c, c), 1) <= lax.broadcasted_iota(jnp.int32, (c, c), 0),
                    1.0, 0.0).astype(BF16)
    lane_lo = lax.broadcasted_iota(jnp.int32, (1, LANES), 1) < RWKV_HEAD

    def stack(x):
        x = x.astype(BF16)
        zero = jnp.zeros_like(x)
        return jnp.concatenate([jnp.where(lane_lo, x, zero), jnp.where(lane_lo, zero, x)], axis=0)

    def bf(x):
        return x.astype(BF16)

    streams = [(ch, p) for ch in range(n_chunks) for p in range(n_pairs)]

    def load(ref):
        return [ref[ch * c:(ch + 1) * c, p * LANES:(p + 1) * LANES] for ch, p in streams]

    rc, kc, vc, kkc, ac, lwc = (load(ref) for ref in (r_ref, k_ref, v_ref, kk_ref, a_ref, lw_ref))

    cum = []
    for x in lwc:
        cs = _dot(tri, jnp.concatenate(_split_bf16(x, 3), axis=1))
        cum.append(cs[:, :LANES] + cs[:, LANES:2 * LANES] + cs[:, 2 * LANES:])
    p_incl = [jnp.exp(x) for x in cum]
    p_inv = [jnp.exp(-x) for x in cum]
    p_end = [x[c - 1:c, :] for x in p_incl]
    at = [bf(-kk * jnp.exp(cm - lw)) for kk, cm, lw in zip(kkc, cum, lwc)]
    bt = [bf(kk * a * pi) for kk, a, pi in zip(kkc, ac, p_inv)]
    kt = [bf(k * pi) for k, pi in zip(kc, p_inv)]
    rt = [r * pf for r, pf in zip(rc, p_incl)]
    v_st = [stack(x) for x in vc]

    prod = [_dot_nt(jnp.concatenate([a, bf(r)], axis=0), jnp.concatenate([stack(b), stack(k)], axis=0))
            for a, r, b, k in zip(at, rt, bt, kt)]
    l_ab = [jnp.where(strict, x[:c, :LANES], 0.0) for x in prod]
    l_ak = [bf(jnp.where(strict, x[:c, LANES:], 0.0)) for x in prod]
    m_rb = [bf(jnp.where(incl, x[c:, :LANES], 0.0)) for x in prod]
    m_rk = [bf(jnp.where(incl, x[c:, LANES:], 0.0)) for x in prod]

    dg = [bf(jnp.where(blk[RWKV_INV_BASE], x, 0.0)) for x in l_ab]
    d2 = [bf(_dot(x, stack(x))) for x in dg]
    d4 = [_dot(x, stack(x)) for x in d2]
    tinv = [eye + x.astype(F32) for x in dg]
    tinv = [t + _dot(bf(t), stack(y)) for t, y in zip(tinv, d2)]
    tinv = [t + _dot(bf(t), stack(y)) for t, y in zip(tinv, d4)]
    size = RWKV_INV_BASE
    while size < c:
        joins = jnp.logical_not(blk[size]) if 2 * size == c else blk[2 * size] & jnp.logical_not(blk[size])
        off = [jnp.where(joins, x, 0.0) for x in l_ab]
        te = [_dot(bf(t), stack(e)) for t, e in zip(tinv, off)]
        tinv = [t + _dot(bf(x), stack(t)) for t, x in zip(tinv, te)]
        size *= 2

    lakv = [_dot(x, v) for x, v in zip(l_ak, v_st)]
    gu = [_dot(bf(t), jnp.concatenate([stack(a), stack(y)], axis=1)) for t, a, y in zip(tinv, at, lakv)]
    g_nat = [x[:, :LANES] for x in gu]
    uv_nat = [x[:, LANES:] for x in gu]
    mg = [_dot(m, jnp.concatenate([stack(g), stack(u)], axis=1)) for m, g, u in zip(m_rb, g_nat, uv_nat)]
    rg = [bf(r + x[:, :LANES]) for r, x in zip(rt, mg)]
    o_v = [x[:, LANES:] + _dot(m, v) for x, m, v in zip(mg, m_rk, v_st)]
    q_mat = [bf(jnp.where(same_head, _dot(bf(g.T), b), 0.0) * pe) for g, b, pe in zip(g_nat, bt, p_end)]
    n_mat = [jnp.where(same_head, _dot(bf(jnp.concatenate([u, v.astype(F32)], axis=0).T),
                                       jnp.concatenate([b, k], axis=0)), 0.0) * pe
             for u, v, b, k, pe in zip(uv_nat, vc, bt, kt, p_end)]

    state = [s_ref[p] for p in range(n_pairs)]
    for i, (ch, p) in enumerate(streams):
        sb = bf(state[p])
        o_ref[ch * c:(ch + 1) * c, p * LANES:(p + 1) * LANES] = _dot_nt(rg[i], sb) + o_v[i]
        state[p] = state[p] * p_end[i] + _dot(sb, q_mat[i]) + n_mat[i]
    for p in range(n_pairs):
        s_ref[p] = state[p]


def _rwkv_post_kernel(o_ref, r_ref, k_ref, v_ref, g_ref, x_ref, lnw_ref, lnb_ref, rk_ref, bd_ref, w_ref,
                      gl_ref, bl_ref, xo_ref, xbo_ref):
    bd = bd_ref[...]
    o = o_ref[...]
    mean = _head_sums(o, bd) * (1.0 / RWKV_HEAD)
    dlt = o - mean
    var = _head_sums(dlt * dlt, bd) * (1.0 / RWKV_HEAD)
    y = dlt * lax.rsqrt(var + RWKV_GN_EPS) * lnw_ref[...] + lnb_ref[...]
    bonus = _head_sums(r_ref[...] * k_ref[...] * rk_ref[...], bd) * v_ref[...]
    out = ((y + bonus) * g_ref[...]).astype(BF16)
    res = _post_ln(x_ref[...], _dot(out, w_ref[...]), gl_ref[...], bl_ref[...])
    xo_ref[...] = res
    xbo_ref[...] = res.astype(BF16)


def _rwkv_layer(x, batch, mu, w_rkv, w0, w1, w2, a0, a1, a2, g1, g2, k_k, k_a, r_k, ln_w, ln_b, w_o,
                ln_g_post, ln_b_post, tm):
    t, d = x.shape
    seq = t // batch
    tmr = min(256, seq)
    vec = lambda p: p.reshape(1, d)
    idx = jnp.arange(LANES) // RWKV_HEAD
    ones_bd = (idx[:, None] == idx[None, :]).astype(BF16)
    tok = [jax.ShapeDtypeStruct((t, d), dt) for dt in (F32, F32, BF16, F32, F32, F32, BF16)]
    halo = pl.BlockSpec((8, d), lambda i: (jnp.maximum(i * (tmr // 8) - 1, 0), 0))
    r, k, v, kkn, a, lw, g = pl.pallas_call(
        functools.partial(_rwkv_pre_kernel, tiles_per_seq=seq // tmr),
        grid=(t // tmr,),
        in_specs=[_rows(tmr, d), halo, _resident((6, d)), _resident((3, d, d)),
                  _resident((1, d)), _resident(w1.shape), _resident(w2.shape),
                  _resident((1, d)), _resident(a1.shape), _resident(a2.shape),
                  _resident(g1.shape), _resident(g2.shape), _resident((1, d)), _resident((1, d)),
                  _resident((LANES, LANES))],
        out_specs=[_rows(tmr, d)] * 7,
        out_shape=tok,
        compiler_params=_params("parallel"),
        name="rwkv_pre",
    )(x, x, mu, w_rkv.astype(BF16), vec(w0), w1.astype(BF16), w2.astype(BF16), vec(a0), a1.astype(BF16),
      a2.astype(BF16), g1.astype(BF16), g2.astype(BF16), vec(k_k), vec(k_a), ones_bd)

    tc = min(RWKV_STEP_CHUNKS * RWKV_CHUNK, seq)
    nt = seq // tc
    width = RWKV_STEP_PAIRS * LANES
    pair = pl.BlockSpec((tc, width), lambda b, p, i: (b * nt + i, p))
    o = pl.pallas_call(
        functools.partial(_rwkv_chunk_kernel, n_pairs=RWKV_STEP_PAIRS, n_chunks=tc // RWKV_CHUNK),
        grid=(batch, d // width, nt),
        in_specs=[pair] * 6,
        out_specs=pair,
        out_shape=jax.ShapeDtypeStruct((t, d), F32),
        scratch_shapes=[pltpu.VMEM((RWKV_STEP_PAIRS, LANES, LANES), F32)],
        compiler_params=_params("parallel", "parallel", "arbitrary"),
        name="rwkv_chunk",
    )(r, k, v, kkn, a, lw)

    return pl.pallas_call(
        _rwkv_post_kernel,
        grid=(t // tm,),
        in_specs=[_rows(tm, d)] * 6 + [_resident((1, d))] * 3 + [_resident((LANES, LANES)), _resident((d, d)),
                                                                _resident((1, d)), _resident((1, d))],
        out_specs=[_rows(tm, d), _rows(tm, d)],
        out_shape=[jax.ShapeDtypeStruct((t, d), F32), jax.ShapeDtypeStruct((t, d), BF16)],
        compiler_params=_params("parallel"),
        name="rwkv_post",
    )(o, r, k, v, g, x, vec(ln_w), vec(ln_b), vec(r_k), ones_bd, w_o.astype(BF16), vec(ln_g_post), vec(ln_b_post))


def _ffn_kernel(x_ref, xb_ref, halo_ref, wa_ref, wb_ref, cw_ref, cb_ref, wo_ref, g_ref, b_ref,
                xo_ref, xbo_ref, acc_ref, *, tiles_per_seq, n_chunks, halo_rows):
    first = (pl.program_id(0) % tiles_per_seq) == 0
    xb = xb_ref[...]
    tm = xb.shape[0]
    halo = halo_ref[...]
    halo = jnp.where(first, jnp.zeros_like(halo), halo)
    xe = jnp.concatenate([halo, xb], axis=0)
    acc_ref[...] = jnp.zeros_like(acc_ref)

    def up(c):
        return _dot(xe, wa_ref[c]), _dot(xb, wb_ref[c])

    def down(c, ha, hb):
        cw = cw_ref[c]
        conv = (cw[0:1, :] * ha[halo_rows - 2:halo_rows - 2 + tm, :]
                + cw[1:2, :] * ha[halo_rows - 1:halo_rows - 1 + tm, :]
                + cw[2:3, :] * ha[halo_rows:, :] + cb_ref[c])
        act = conv * jax.nn.sigmoid(conv) * hb
        acc_ref[...] += _dot(act.astype(BF16), wo_ref[c])

    h = up(0)
    for c in range(n_chunks):
        h_next = up(c + 1) if c + 1 < n_chunks else None
        down(c, *h)
        h = h_next
    y = _post_ln(x_ref[...], acc_ref[...], g_ref[...], b_ref[...])
    xo_ref[...] = y
    xbo_ref[...] = y.astype(BF16)


def _ffn_layer(x, xb, batch, w_in, conv_w, conv_b, w_out, ln_g, ln_b, tm):
    t, d = x.shape
    seq = t // batch
    nch = D_FF_PAD // FF_CHUNK
    padc = D_FF_PAD - D_FF
    chunks = lambda w: jnp.pad(w, ((0, 0), (0, padc))).reshape(w.shape[0], nch, FF_CHUNK).transpose(1, 0, 2)
    wa = chunks(w_in[:, :D_FF]).astype(BF16)
    wb = chunks(w_in[:, D_FF:]).astype(BF16)
    cw = chunks(conv_w)
    cb = chunks(conv_b.reshape(1, D_FF))
    wo = jnp.pad(w_out, ((0, padc), (0, 0))).reshape(nch, FF_CHUNK, d).astype(BF16)
    halo_rows = 16
    halo = pl.BlockSpec((halo_rows, d), lambda i: (jnp.maximum(i * (tm // halo_rows) - 1, 0), 0))
    return pl.pallas_call(
        functools.partial(_ffn_kernel, tiles_per_seq=seq // tm, n_chunks=nch, halo_rows=halo_rows),
        grid=(t // tm,),
        in_specs=[_rows(tm, d), _rows(tm, d), halo, _resident(wa.shape), _resident(wb.shape), _resident(cw.shape),
                  _resident(cb.shape), _resident(wo.shape), _resident((1, d)), _resident((1, d))],
        out_specs=[_rows(tm, d), _rows(tm, d)],
        out_shape=[jax.ShapeDtypeStruct((t, d), F32), jax.ShapeDtypeStruct((t, d), BF16)],
        scratch_shapes=[pltpu.VMEM((tm, d), F32)],
        compiler_params=_params("parallel"),
        name="conv_ffn",
    )(x, xb, xb, wa, wb, cw, cb, wo, ln_g.reshape(1, d), ln_b.reshape(1, d))


def kernel(x, positions, ln_g, ln_b, mla_w_down, mla_q_norm, mla_kv_norm, mla_w_uq, mla_w_ukv, mla_w_o,
           dil_w_qkv, dil_w_o, rwkv_mu, rwkv_w_rkv, rwkv_w0, rwkv_w1, rwkv_w2, rwkv_a0, rwkv_a1, rwkv_a2,
           rwkv_g1, rwkv_g2, rwkv_k_k, rwkv_k_a, rwkv_r_k, rwkv_ln_w, rwkv_ln_b, rwkv_w_o,
           ffn_w_in, ffn_conv_w, ffn_conv_b, ffn_w_out):
    batch, seq, d = x.shape
    t = batch * seq
    tm = min(512, seq)
    xf = x.reshape(t, d)
    xb = xf.astype(BF16)
    cos, sin = _rope_tables(positions, tm)
    ia = ib = ic = 0
    for i in range(DEPTH):
        kind = i % 3
        if kind == 0:
            xf, xb = _mla_layer(xf, xb, cos, sin, batch, mla_w_down[ia], mla_q_norm[ia], mla_kv_norm[ia],
                                mla_w_uq[ia], mla_w_ukv[ia], mla_w_o[ia], ln_g[i, 0], ln_b[i, 0], tm)
            ia += 1
        elif kind == 1:
            xf, xb = _dil_layer(xf, xb, batch, dil_w_qkv[ib], dil_w_o[ib], ln_g[i, 0], ln_b[i, 0], tm)
            ib += 1
        else:
            xf, xb = _rwkv_layer(xf, batch, rwkv_mu[ic], rwkv_w_rkv[ic], rwkv_w0[ic], rwkv_w1[ic], rwkv_w2[ic],
                                 rwkv_a0[ic], rwkv_a1[ic], rwkv_a2[ic], rwkv_g1[ic], rwkv_g2[ic],
                                 rwkv_k_k[ic], rwkv_k_a[ic], rwkv_r_k[ic], rwkv_ln_w[ic], rwkv_ln_b[ic],
                                 rwkv_w_o[ic], ln_g[i, 0], ln_b[i, 0], tm)
            ic += 1
        xf, xb = _ffn_layer(xf, xb, batch, ffn_w_in[i], ffn_conv_w[i], ffn_conv_b[i], ffn_w_out[i],
                            ln_g[i, 1], ln_b[i, 1], tm)
    return xf.reshape(batch, seq, d)
```

```python
import functools

import jax
import jax.numpy as jnp
from jax import lax
from jax.experimental import pallas as pl
from jax.experimental.pallas import tpu as pltpu

F32 = jnp.float32
BF16 = jnp.bfloat16

D_MODEL = 1024
DEPTH = 4
MLA_HEADS = 16
MLA_Q_LORA = 384
MLA_KV_LORA = 256
MLA_NOPE = 64
MLA_ROPE = 32
MLA_V = 64
ROPE_THETA = 10000.0
DIL_GROUPS = ((128, 1), (512, 4), (2048, 16))
DIL_HEADS = 16
DIL_HEAD_DIM = 64
DIL_BLOCK = 128
DIL_STEP_BLOCKS = 4
DIL_STAGE_PAIRS = 4
DIL_LSE_LANES = 128 // DIL_HEADS
RWKV_HEAD = 64
RWKV_GN_EPS = 64e-5
D_FF = 2752
ALPHA = (2 * DEPTH) ** 0.25
LN_EPS = 1e-5
RMS_EPS = 1e-6
NEG_INF = -1e30
LOG2_E = 1.4426950408889634

LANES = 128
VMEM_LIMIT_BYTES = 56 * 2**20

SLOT = LANES
ROPE_LO = MLA_NOPE
ROPE_HALF = MLA_ROPE // 2
MLA_STEP_PAIRS = 4
FF_CHUNK = 256
D_FF_PAD = -(-D_FF // FF_CHUNK) * FF_CHUNK
RWKV_CHUNK = 64
RWKV_INV_BASE = 8
RWKV_STEP_PAIRS = 8
RWKV_STEP_CHUNKS = 2


def _params(*sem):
    return pltpu.CompilerParams(dimension_semantics=sem, vmem_limit_bytes=VMEM_LIMIT_BYTES)


def _dot(a, b):
    return jnp.dot(a, b, preferred_element_type=F32)


def _dot_nt(a, b):
    return lax.dot_general(a, b, (((1,), (1,)), ((), ())), preferred_element_type=F32)


def _resident(shape):
    nd = len(shape)
    return pl.BlockSpec(shape, lambda *_: (0,) * nd, pipeline_mode=pl.Buffered(1))


def _rows(tm, width):
    return pl.BlockSpec((tm, width), lambda i: (i, 0))


def _post_ln(x, h, g, b):
    y = ALPHA * x + h
    mu = jnp.mean(y, axis=-1, keepdims=True)
    d = y - mu
    var = jnp.mean(d * d, axis=-1, keepdims=True)
    return d * lax.rsqrt(var + LN_EPS) * g + b


def _rms(x, g):
    return x * lax.rsqrt(jnp.mean(x * x, axis=-1, keepdims=True) + RMS_EPS) * g


def _split_bf16(x, parts):
    out = []
    for _ in range(parts - 1):
        hi = x.astype(BF16)
        out.append(hi)
        x = x - hi.astype(F32)
    out.append(x.astype(BF16))
    return out


def _head_sums(x, ones_bd):
    cols = []
    for p in range(x.shape[1] // LANES):
        xs = x[:, p * LANES:(p + 1) * LANES]
        hi, lo = _split_bf16(xs, 2)
        cols.append(_dot(hi, ones_bd) + _dot(lo, ones_bd))
    return jnp.concatenate(cols, axis=1)


def _rope_table_kernel(pos_ref, freq_ref, cos_ref, sin_ref):
    ang = pos_ref[...].astype(F32) * freq_ref[...]
    cos_ref[...] = jnp.cos(ang)
    sin_ref[...] = jnp.sin(ang)


def _rope_tables(positions, tm):
    t = positions.size
    inv_freq = ROPE_THETA ** (-jnp.arange(0, MLA_ROPE, 2, dtype=F32) / MLA_ROPE)
    freq = jnp.zeros((1, SLOT), F32)
    freq = freq.at[0, ROPE_LO:ROPE_LO + ROPE_HALF].set(inv_freq)
    freq = freq.at[0, ROPE_LO + ROPE_HALF:ROPE_LO + MLA_ROPE].set(inv_freq)
    return pl.pallas_call(
        _rope_table_kernel,
        grid=(t // tm,),
        in_specs=[_rows(tm, 1), _resident((1, SLOT))],
        out_specs=[_rows(tm, SLOT), _rows(tm, SLOT)],
        out_shape=[jax.ShapeDtypeStruct((t, SLOT), F32)] * 2,
        compiler_params=_params("parallel"),
        name="rope_tables",
    )(positions.reshape(t, 1), freq)


def _rope_slots(z, cos, sin):
    width = z.shape[1]
    lane = lax.broadcasted_iota(jnp.int32, (1, width), 1) % SLOT
    first = (lane >= ROPE_LO) & (lane < ROPE_LO + ROPE_HALF)
    second = (lane >= ROPE_LO + ROPE_HALF) & (lane < ROPE_LO + MLA_ROPE)
    from_left = pltpu.roll(z, ROPE_HALF, axis=1)
    from_right = pltpu.roll(z, width - ROPE_HALF, axis=1)
    partner = jnp.where(second, from_left, jnp.where(first, -from_right, 0.0))
    return z * cos + partner * sin


def _rope_slots_partnered(z, cos, sin):
    width = z.shape[1]
    return z * cos + pltpu.roll(z, width - MLA_ROPE, axis=1) * sin


def _mla_qkv_kernel(xb_ref, wd_ref, qn_ref, kvn_ref, wq_ref, wkv_ref, cos_ref, sin_ref, q_ref, k_ref, vt_ref, *, scale):
    cos, sin = cos_ref[...], sin_ref[...]
    lat = _dot(xb_ref[...], wd_ref[...])
    cq = _rms(lat[:, :MLA_Q_LORA], qn_ref[...]).astype(BF16)
    ckv = _rms(lat[:, MLA_Q_LORA:MLA_Q_LORA + MLA_KV_LORA], kvn_ref[...]).astype(BF16)
    kpe = _rope_slots(lat[:, MLA_Q_LORA + MLA_KV_LORA:], cos, sin)
    q = _dot(cq, wq_ref[...])
    q_ref[...] = (_rope_slots_partnered(q, jnp.tile(cos, (1, MLA_HEADS)), jnp.tile(sin, (1, MLA_HEADS)))
                  * scale).astype(BF16)
    kv = _dot(ckv, wkv_ref[...])
    kw = MLA_HEADS * SLOT
    k_ref[...] = (kv[:, :kw] + jnp.tile(kpe, (1, MLA_HEADS))).astype(BF16)
    vt_ref[0] = kv[:, kw:].T.astype(BF16)


def _mla_attn_kernel(q_ref, k_ref, vt_ref, o_ref, *, tq, tk):
    assert tq == tk
    n_full = pl.program_id(2)
    heads = tuple(range(q_ref.shape[1] // SLOT))
    slots = tuple(slice(hh * SLOT, (hh + 1) * SLOT) for hh in heads)

    def own_rows(hh, width):
        sub = lax.broadcasted_iota(jnp.int32, (LANES, width), 0)
        return sub < MLA_V if hh % 2 == 0 else sub >= MLA_V

    def causal(keys, queries):
        return (lax.broadcasted_iota(jnp.int32, (keys, queries), 0)
                <= lax.broadcasted_iota(jnp.int32, (keys, queries), 1))

    def scores(j):
        start = pl.multiple_of(j * tk, tk)
        return tuple(_dot_nt(k_ref[pl.ds(start, tk), slot], q_ref[:, slot]) for slot in slots)

    def accumulate(s, state, vt, hh):
        m, acc = state
        m_new = jnp.maximum(m, jnp.max(s, axis=0, keepdims=True))
        alpha = jnp.exp2(m - m_new)
        p = jnp.exp2(s - m_new).astype(BF16)
        vt = jnp.where(own_rows(hh, vt.shape[1]), vt, jnp.ones(vt.shape, BF16))
        return m_new, alpha * acc + _dot(vt, p)

    def update(j, s_heads, states):
        return tuple(accumulate(s, st, vt_ref[j, hh // 2 * LANES:(hh // 2 + 1) * LANES, :], hh)
                     for hh, s, st in zip(heads, s_heads, states))

    def body(j, states):
        return update(j, scores(j), states)

    def body2(j2, states):
        s_even, s_odd = scores(2 * j2), scores(2 * j2 + 1)
        return update(2 * j2 + 1, s_odd, update(2 * j2, s_even, states))

    init = (jnp.full((1, tq), NEG_INF, F32), jnp.zeros((LANES, tq), F32))
    states = lax.fori_loop(0, n_full // 2, body2, (init,) * len(heads))
    states = lax.fori_loop(n_full - n_full % 2, n_full, body, states)
    s_diag = tuple(jnp.where(causal(tk, tq), s, NEG_INF) for s in scores(n_full))
    accs = [acc for _, acc in update(n_full, s_diag, states)]
    for pp in range(len(heads) // 2):
        acc0, acc1 = accs[2 * pp], accs[2 * pp + 1]
        out_t = jnp.where(own_rows(0, tq), acc0 / acc0[MLA_V:MLA_V + 1, :], acc1 / acc1[0:1, :])
        o_ref[:, pp * LANES:(pp + 1) * LANES] = out_t.T.astype(o_ref.dtype)


def _proj_ln_kernel(a_ref, w_ref, x_ref, g_ref, b_ref, xo_ref, xbo_ref):
    y = _post_ln(x_ref[...], _dot(a_ref[...], w_ref[...]), g_ref[...], b_ref[...])
    xo_ref[...] = y
    xbo_ref[...] = y.astype(BF16)


def _proj_ln(a, w, x, g, b, tm):
    t, d = x.shape
    return pl.pallas_call(
        _proj_ln_kernel,
        grid=(t // tm,),
        in_specs=[_rows(tm, a.shape[1]), _resident(w.shape), _rows(tm, d), _resident((1, d)), _resident((1, d))],
        out_specs=[_rows(tm, d), _rows(tm, d)],
        out_shape=[jax.ShapeDtypeStruct((t, d), F32), jax.ShapeDtypeStruct((t, d), BF16)],
        compiler_params=_params("parallel"),
        name="proj_ln",
    )(a, w, x, g.reshape(1, d), b.reshape(1, d))


def _mla_layer(x, xb, cos, sin, batch, w_down, q_norm, kv_norm, w_uq, w_ukv, w_o, ln_g, ln_b, tm):
    t, d = x.shape
    seq = t // batch
    h = MLA_HEADS
    pad = jnp.zeros((d, ROPE_LO), F32)
    w_down_s = jnp.concatenate(
        [w_down[:, :MLA_Q_LORA + MLA_KV_LORA], pad, w_down[:, MLA_Q_LORA + MLA_KV_LORA:],
         jnp.zeros((d, SLOT - ROPE_LO - MLA_ROPE), F32)], axis=1).astype(BF16)
    w_uq3 = w_uq.reshape(MLA_Q_LORA, h, MLA_NOPE + MLA_ROPE)
    w_partner = jnp.concatenate([-w_uq3[:, :, MLA_NOPE + ROPE_HALF:], w_uq3[:, :, MLA_NOPE:MLA_NOPE + ROPE_HALF]], axis=2)
    w_uq_s = jnp.concatenate([w_uq3, w_partner], axis=2).reshape(MLA_Q_LORA, h * SLOT).astype(BF16)
    w_ukv3 = w_ukv.reshape(MLA_KV_LORA, h, MLA_NOPE + MLA_V)
    w_k_s = jnp.pad(w_ukv3[:, :, :MLA_NOPE], ((0, 0), (0, 0), (0, SLOT - MLA_NOPE))).reshape(MLA_KV_LORA, h * SLOT)
    w_v_s = w_ukv3[:, :, MLA_NOPE:].reshape(MLA_KV_LORA, h * MLA_V)
    w_kv_s = jnp.concatenate([w_k_s, w_v_s], axis=1).astype(BF16)

    assert SLOT == MLA_NOPE + 2 * MLA_ROPE
    tq = tk = tm
    nq = seq // tq
    q, k, vt = pl.pallas_call(
        functools.partial(_mla_qkv_kernel, scale=(MLA_NOPE + MLA_ROPE) ** -0.5 * LOG2_E),
        grid=(t // tm,),
        in_specs=[_rows(tm, d), _resident(w_down_s.shape), _resident((1, MLA_Q_LORA)), _resident((1, MLA_KV_LORA)),
                  _resident(w_uq_s.shape), _resident(w_kv_s.shape), _rows(tm, SLOT), _rows(tm, SLOT)],
        out_specs=[_rows(tm, h * SLOT), _rows(tm, h * SLOT), pl.BlockSpec((1, h * MLA_V, tm), lambda i: (i, 0, 0))],
        out_shape=[jax.ShapeDtypeStruct((t, h * SLOT), BF16), jax.ShapeDtypeStruct((t, h * SLOT), BF16),
                   jax.ShapeDtypeStruct((t // tm, h * MLA_V, tm), BF16)],
        compiler_params=_params("parallel"),
        name="mla_qkv",
    )(xb, w_down_s, q_norm.reshape(1, -1), kv_norm.reshape(1, -1), w_uq_s, w_kv_s, cos, sin)

    hps = 2 * MLA_STEP_PAIRS
    o = pl.pallas_call(
        functools.partial(_mla_attn_kernel, tq=tq, tk=tk),
        grid=(batch, h // hps, nq),
        in_specs=[pl.BlockSpec((tq, hps * SLOT), lambda b, p, i: (b * nq + i, p)),
                  pl.BlockSpec((seq, hps * SLOT), lambda b, p, i: (b, p)),
                  pl.BlockSpec((nq, hps * MLA_V, tk), lambda b, p, i: (b, p, 0))],
        out_specs=pl.BlockSpec((tq, hps * MLA_V), lambda b, p, i: (b * nq + i, p)),
        out_shape=jax.ShapeDtypeStruct((t, h * MLA_V), BF16),
        compiler_params=_params("parallel", "parallel", "arbitrary"),
        name="mla_attn",
    )(q, k, vt)

    return _proj_ln(o, w_o.astype(BF16), x, ln_g, ln_b, tm)


def _dil_qkv_kernel(x_ref, w_ref, o_ref, *scratch, dil):
    if dil == 1:
        o_ref[0, 0] = _dot(x_ref[...].astype(BF16), w_ref[...]).astype(o_ref.dtype)
        return
    lanes_ref, xperm_ref = scratch
    rows = x_ref.shape[0] // dil

    @pl.when(pl.program_id(1) == 0)
    def _():
        for p in range(x_ref.shape[1] // LANES):
            sl = slice(p * LANES, (p + 1) * LANES)
            lanes_ref[p] = x_ref[:, sl]
            for r in range(dil):
                xperm_ref[r * rows:(r + 1) * rows, sl] = lanes_ref[p, pl.ds(r, rows, stride=dil), :].astype(BF16)

    res = _dot(xperm_ref[...], w_ref[...])
    for r in range(dil):
        o_ref[0, r] = res[r * rows:(r + 1) * rows, :].astype(o_ref.dtype)


def _dil_attn_kernel(q_ref, kp_ref, kc_ref, vp_ref, vc_ref, o_ref, lse_ref, *, n_sub):
    q_ref, kp_ref, kc_ref, vp_ref, vc_ref, o_ref, lse_ref = (
        ref.at[0, 0] for ref in (q_ref, kp_ref, kc_ref, vp_ref, vc_ref, o_ref, lse_ref))
    bk = DIL_BLOCK
    i = lax.broadcasted_iota(jnp.int32, (bk, 2 * bk), 0)
    j = lax.broadcasted_iota(jnp.int32, (bk, 2 * bk), 1)
    dist = i + bk - j
    band = (dist >= 0) & (dist <= bk)
    prev_exists = (pl.program_id(2) > 0) | (j >= bk)
    lse_lane = lax.broadcasted_iota(jnp.int32, (1, LANES), 1)
    lo = lse_lane < DIL_HEAD_DIM
    pairs = [slice(p * LANES, (p + 1) * LANES) for p in range(DIL_HEADS // 2)]
    for sb in range(n_sub):
        rows = slice(sb * bk, (sb + 1) * bk)
        valid = band & prev_exists if sb == 0 else band

        def window(prev_ref, cur_ref, sl, sb=sb):
            if sb == 0:
                return jnp.concatenate([prev_ref[:, sl], cur_ref[:bk, sl]], axis=0)
            return cur_ref[(sb - 1) * bk:(sb + 1) * bk, sl]

        def pair_scores_of(sl):
            q = q_ref[rows, sl]
            k = window(kp_ref, kc_ref, sl)
            zero = jnp.zeros_like(q)
            return [_dot_nt(qh, k) for qh in (jnp.where(lo, q, zero), jnp.where(lo, zero, q))]

        lse_blk = jnp.zeros((bk, LANES), F32)
        scores = {}
        for p, sl in enumerate(pairs):
            if p % DIL_STAGE_PAIRS == 0:
                scores = {pp: pair_scores_of(pairs[pp]) for pp in range(p, min(p + DIL_STAGE_PAIRS, len(pairs)))}
            pair_scores = scores[p]
            v = window(vp_ref, vc_ref, sl)
            res = []
            for s in pair_scores:
                s = jnp.where(valid, s, NEG_INF)
                m = jnp.max(s, axis=-1, keepdims=True)
                e = jnp.exp(s - m)
                l = jnp.sum(e, axis=-1, keepdims=True)
                res.append((_dot(e.astype(BF16), v) / l, m + jnp.log(l)))
            o_ref[rows, sl] = jnp.where(lo, res[0][0], res[1][0]).astype(o_ref.dtype)
            lse_blk = jnp.where(lse_lane // DIL_LSE_LANES == 2 * p, res[0][1],
                                jnp.where(lse_lane // DIL_LSE_LANES == 2 * p + 1, res[1][1], lse_blk))
        lse_ref[rows, :] = lse_blk


def _dil_out_kernel(o1_ref, o2_ref, o3_ref, l1_ref, l2_ref, l3_ref, expand_ref, w_ref, x_ref, g_ref, b_ref,
                    xo_ref, xbo_ref, *bufs):
    bufs = list(bufs)

    def token_order(ref, dil):
        if dil == 1:
            return ref[0, 0].astype(F32)
        n_blocks = ref.shape[3] // LANES
        buf = bufs.pop()
        for p in range(n_blocks):
            for r in range(dil):
                buf[p, pl.ds(r, ref.shape[2], stride=dil), :] = ref[0, r, :, p * LANES:(p + 1) * LANES].astype(F32)
        return jnp.concatenate([buf[p] for p in range(n_blocks)], axis=1)

    dils = [dil for _, dil in DIL_GROUPS]
    l1, l2, l3 = (token_order(ref, dil) for ref, dil in zip((l1_ref, l2_ref, l3_ref), dils))
    m = jnp.maximum(jnp.maximum(l1, l2), l3)
    e1, e2, e3 = jnp.exp(l1 - m), jnp.exp(l2 - m), jnp.exp(l3 - m)
    inv = 1.0 / (e1 + e2 + e3)
    w1, w2, w3 = (_dot((e * inv).astype(BF16), expand_ref[...]) for e in (e1, e2, e3))
    o1, o2, o3 = (token_order(ref, dil) for ref, dil in zip((o1_ref, o2_ref, o3_ref), dils))
    o = w1 * o1 + w2 * o2 + w3 * o3
    y = _post_ln(x_ref[...], _dot(o.astype(BF16), w_ref[...]), g_ref[...], b_ref[...])
    xo_ref[...] = y
    xbo_ref[...] = y.astype(BF16)


def _dil_layer(x, xb, batch, w_qkv, w_o, ln_g, ln_b, tm):
    t, d = x.shape
    seq = t // batch
    ng = len(DIL_GROUPS)
    hd = DIL_HEADS * DIL_HEAD_DIM
    nqkv = ng * 3 * hd
    col_kind = (jnp.arange(nqkv) // hd) % 3
    w_s = (w_qkv * jnp.where(col_kind == 0, DIL_HEAD_DIM ** -0.5, 1.0)[None, :]).astype(BF16)
    tps = seq // tm
    bk = DIL_BLOCK
    outs, lses = [], []
    for gi, (window, dil) in enumerate(DIL_GROUPS):
        assert window // dil == bk and tm % (16 * dil) == 0
        assert seq % (dil * bk * min(DIL_STEP_BLOCKS, seq // (dil * bk))) == 0
        tmq = min(2 * tm, seq)
        tpq = seq // tmq
        qkv = pl.pallas_call(
            functools.partial(_dil_qkv_kernel, dil=dil),
            grid=(t // tmq, 3),
            in_specs=[pl.BlockSpec((tmq, d), lambda i, c: (i, 0)),
                      pl.BlockSpec((d, hd), lambda i, c, gi=gi: (0, gi * 3 + c))],
            out_specs=pl.BlockSpec((1, dil, tmq // dil, hd), lambda i, c, tpq=tpq: (i // tpq, 0, i % tpq, c)),
            out_shape=jax.ShapeDtypeStruct((batch, dil, seq // dil, 3 * hd), BF16),
            scratch_shapes=[] if dil == 1 else [pltpu.VMEM((d // LANES, tmq, LANES), F32), pltpu.VMEM((tmq, d), BF16)],
            compiler_params=_params("parallel", "arbitrary"),
            name=f"dil_qkv_d{dil}",
        )(xb if dil == 1 else x, w_s)

        n_sub = min(DIL_STEP_BLOCKS, seq // (dil * bk))

        def spec(kind, prev, n_sub=n_sub):
            if prev:
                return pl.BlockSpec((1, 1, bk, hd), lambda b, r, n: (b, r, jnp.maximum(n * n_sub - 1, 0), kind))
            return pl.BlockSpec((1, 1, n_sub * bk, hd), lambda b, r, n: (b, r, n, kind))

        out_specs = [pl.BlockSpec((1, 1, n_sub * bk, width), lambda b, r, n: (b, r, n, 0)) for width in (hd, LANES)]
        o, lse = pl.pallas_call(
            functools.partial(_dil_attn_kernel, n_sub=n_sub),
            grid=(batch, dil, seq // (dil * bk * n_sub)),
            in_specs=[spec(0, False), spec(1, True), spec(1, False), spec(2, True), spec(2, False)],
            out_specs=out_specs,
            out_shape=[jax.ShapeDtypeStruct((batch, dil, seq // dil, hd), BF16),
                       jax.ShapeDtypeStruct((batch, dil, seq // dil, LANES), F32)],
            compiler_params=_params("parallel", "parallel", "arbitrary"),
            name=f"dil_attn_d{dil}",
        )(qkv, qkv, qkv, qkv, qkv)
        outs.append(o)
        lses.append(lse)

    def res_specs(width):
        return [pl.BlockSpec((1, dil, tm // dil, width), lambda i: (i // tps, 0, i % tps, 0)) for _, dil in DIL_GROUPS]

    expand = (jnp.arange(LANES)[:, None] == DIL_LSE_LANES * (jnp.arange(hd) // DIL_HEAD_DIM)[None, :]).astype(BF16)
    n_strided = sum(dil > 1 for _, dil in DIL_GROUPS)
    bufs = [pltpu.VMEM((hd // LANES, tm, LANES), F32)] * n_strided + [pltpu.VMEM((1, tm, LANES), F32)] * n_strided
    return pl.pallas_call(
        _dil_out_kernel,
        grid=(t // tm,),
        in_specs=res_specs(hd) + res_specs(LANES) + [_resident((LANES, hd)), _resident((hd, d)), _rows(tm, d),
                                                     _resident((1, d)), _resident((1, d))],
        out_specs=[_rows(tm, d), _rows(tm, d)],
        out_shape=[jax.ShapeDtypeStruct((t, d), F32), jax.ShapeDtypeStruct((t, d), BF16)],
        scratch_shapes=bufs,
        compiler_params=_params("parallel"),
        name="dil_out",
    )(*outs, *lses, expand, w_o.astype(BF16), x, ln_g.reshape(1, d), ln_b.reshape(1, d))


def _rwkv_pre_kernel(x_ref, halo_ref, mu_ref, wrkv_ref, w0_ref, w1_ref, w2_ref, a0_ref, a1_ref, a2_ref,
                     g1_ref, g2_ref, kk_ref, ka_ref, bd_ref,
                     r_o, k_o, v_o, kkn_o, a_o, lw_o, g_o, *, tiles_per_seq):
    first = (pl.program_id(0) % tiles_per_seq) == 0
    x = x_ref[...]
    tm = x.shape[0]
    prev_row = jnp.where(first, 0.0, halo_ref[7:8, :])
    row = lax.broadcasted_iota(jnp.int32, (tm, 1), 0)
    xx = jnp.where(row == 0, prev_row, pltpu.roll(x, 1, axis=0)) - x
    mu = mu_ref[...]

    def mix(j):
        return (x + xx * mu[j:j + 1, :]).astype(BF16)

    r = _dot(mix(0), wrkv_ref[0])
    k = _dot(mix(1), wrkv_ref[1])
    v = _dot(mix(2), wrkv_ref[2])
    z = w0_ref[...] + _dot(jnp.tanh(_dot(mix(3), w1_ref[...])).astype(BF16), w2_ref[...])
    nz = -z
    softplus = jnp.maximum(nz, 0.0) + jnp.log(1.0 + jnp.exp(-jnp.abs(nz)))
    lw = -jnp.exp(-softplus - 0.5)
    a = jax.nn.sigmoid(a0_ref[...] + _dot(_dot(mix(4), a1_ref[...]).astype(BF16), a2_ref[...]))
    g = _dot(jax.nn.sigmoid(_dot(mix(5), g1_ref[...])).astype(BF16), g2_ref[...])
    kk = k * kk_ref[...]
    norm = jnp.sqrt(_head_sums(kk * kk, bd_ref[...]))
    r_o[...] = r
    k_o[...] = k * (1.0 + (a - 1.0) * ka_ref[...])
    v_o[...] = v.astype(v_o.dtype)
    kkn_o[...] = kk / jnp.maximum(norm, 1e-12)
    a_o[...] = a
    lw_o[...] = lw
    g_o[...] = g.astype(g_o.dtype)


def _rwkv_chunk_kernel(r_ref, k_ref, v_ref, kk_ref, a_ref, lw_ref, o_ref, s_ref, *, n_pairs, n_chunks):
    c = RWKV_CHUNK

    @pl.when(pl.program_id(2) == 0)
    def _():
        s_ref[...] = jnp.zeros_like(s_ref)

    row = lax.broadcasted_iota(jnp.int32, (c, LANES), 0)
    col = lax.broadcasted_iota(jnp.int32, (c, LANES), 1) % RWKV_HEAD
    blk = {}
    size = RWKV_INV_BASE
    while size < c:
        blk[size] = (row // size) == (col // size)
        size *= 2
    strict = col < row
    incl = col <= row
    eye = jnp.where(row == col, 1.0, 0.0).astype(F32)
    same_head = (lax.broadcasted_iota(jnp.int32, (LANES, LANES), 0) // RWKV_HEAD
                 == lax.broadcasted_iota(jnp.int32, (LANES, LANES), 1) // RWKV_HEAD)
    tri = jnp.where(lax.broadcasted_iota(jnp.int32, (c, c), 1) <= lax.broadcasted_iota(jnp.int32, (c, c), 0),
                    1.0, 0.0).astype(BF16)
    lane_lo = lax.broadcasted_iota(jnp.int32, (1, LANES), 1) < RWKV_HEAD

    def stack(x):
        x = x.astype(BF16)
        zero = jnp.zeros_like(x)
        return jnp.concatenate([jnp.where(lane_lo, x, zero), jnp.where(lane_lo, zero, x)], axis=0)

    def bf(x):
        return x.astype(BF16)

    streams = [(ch, p) for ch in range(n_chunks) for p in range(n_pairs)]

    def load(ref):
        return [ref[ch * c:(ch + 1) * c, p * LANES:(p + 1) * LANES] for ch, p in streams]

    rc, kc, vc, kkc, ac, lwc = (load(ref) for ref in (r_ref, k_ref, v_ref, kk_ref, a_ref, lw_ref))

    cum = []
    for x in lwc:
        cs = _dot(tri, jnp.concatenate(_split_bf16(x, 3), axis=1))
        cum.append(cs[:, :LANES] + cs[:, LANES:2 * LANES] + cs[:, 2 * LANES:])
    p_incl = [jnp.exp(x) for x in cum]
    p_inv = [jnp.exp(-x) for x in cum]
    p_end = [x[c - 1:c, :] for x in p_incl]
    at = [bf(-kk * jnp.exp(cm - lw)) for kk, cm, lw in zip(kkc, cum, lwc)]
    bt = [bf(kk * a * pi) for kk, a, pi in zip(kkc, ac, p_inv)]
    kt = [bf(k * pi) for k, pi in zip(kc, p_inv)]
    rt = [r * pf for r, pf in zip(rc, p_incl)]
    v_st = [stack(x) for x in vc]

    prod = [_dot_nt(jnp.concatenate([a, bf(r)], axis=0), jnp.concatenate([stack(b), stack(k)], axis=0))
            for a, r, b, k in zip(at, rt, bt, kt)]
    l_ab = [jnp.where(strict, x[:c, :LANES], 0.0) for x in prod]
    l_ak = [bf(jnp.where(strict, x[:c, LANES:], 0.0)) for x in prod]
    m_rb = [bf(jnp.where(incl, x[c:, :LANES], 0.0)) for x in prod]
    m_rk = [bf(jnp.where(incl, x[c:, LANES:], 0.0)) for x in prod]

    dg = [bf(jnp.where(blk[RWKV_INV_BASE], x, 0.0)) for x in l_ab]
    d2 = [bf(_dot(x, stack(x))) for x in dg]
    d4 = [_dot(x, stack(x)) for x in d2]
    tinv = [eye + x.astype(F32) for x in dg]
    tinv = [t + _dot(bf(t), stack(y)) for t, y in zip(tinv, d2)]
    tinv = [t + _dot(bf(t), stack(y)) for t, y in zip(tinv, d4)]
    size = RWKV_INV_BASE
    while size < c:
        joins = jnp.logical_not(blk[size]) if 2 * size == c else blk[2 * size] & jnp.logical_not(blk[size])
        off = [jnp.where(joins, x, 0.0) for x in l_ab]
        te = [_dot(bf(t), stack(e)) for t, e in zip(tinv, off)]
        tinv = [t + _dot(bf(x), stack(t)) for t, x in zip(tinv, te)]
        size *= 2

    lakv = [_dot(x, v) for x, v in zip(l_ak, v_st)]
    gu = [_dot(bf(t), jnp.concatenate([stack(a), stack(y)], axis=1)) for t, a, y in zip(tinv, at, lakv)]
    g_nat = [x[:, :LANES] for x in gu]
    uv_nat = [x[:, LANES:] for x in gu]
    mg = [_dot(m, jnp.concatenate([stack(g), stack(u)], axis=1)) for m, g, u in zip(m_rb, g_nat, uv_nat)]
    rg = [bf(r + x[:, :LANES]) for r, x in zip(rt, mg)]
    o_v = [x[:, LANES:] + _dot(m, v) for x, m, v in zip(mg, m_rk, v_st)]
    q_mat = [bf(jnp.where(same_head, _dot(bf(g.T), b), 0.0) * pe) for g, b, pe in zip(g_nat, bt, p_end)]
    n_mat = [jnp.where(same_head, _dot(bf(jnp.concatenate([u, v.astype(F32)], axis=0).T),
                                       jnp.concatenate([b, k], axis=0)), 0.0) * pe
             for u, v, b, k, pe in zip(uv_nat, vc, bt, kt, p_end)]

    state = [s_ref[p] for p in range(n_pairs)]
    for i, (ch, p) in enumerate(streams):
        sb = bf(state[p])
        o_ref[ch * c:(ch + 1) * c, p * LANES:(p + 1) * LANES] = _dot_nt(rg[i], sb) + o_v[i]
        state[p] = state[p] * p_end[i] + _dot(sb, q_mat[i]) + n_mat[i]
    for p in range(n_pairs):
        s_ref[p] = state[p]


def _rwkv_post_kernel(o_ref, r_ref, k_ref, v_ref, g_ref, x_ref, lnw_ref, lnb_ref, rk_ref, bd_ref, w_ref,
                      gl_ref, bl_ref, xo_ref, xbo_ref):
    bd = bd_ref[...]
    o = o_ref[...]
    mean = _head_sums(o, bd) * (1.0 / RWKV_HEAD)
    dlt = o - mean
    var = _head_sums(dlt * dlt, bd) * (1.0 / RWKV_HEAD)
    y = dlt * lax.rsqrt(var + RWKV_GN_EPS) * lnw_ref[...] + lnb_ref[...]
    bonus = _head_sums(r_ref[...] * k_ref[...] * rk_ref[...], bd) * v_ref[...]
    out = ((y + bonus) * g_ref[...]).astype(BF16)
    res = _post_ln(x_ref[...], _dot(out, w_ref[...]), gl_ref[...], bl_ref[...])
    xo_ref[...] = res
    xbo_ref[...] = res.astype(BF16)


def _rwkv_layer(x, batch, mu, w_rkv, w0, w1, w2, a0, a1, a2, g1, g2, k_k, k_a, r_k, ln_w, ln_b, w_o,
                ln_g_post, ln_b_post, tm):
    t, d = x.shape
    seq = t // batch
    tmr = min(256, seq)
    vec = lambda p: p.reshape(1, d)
    idx = jnp.arange(LANES) // RWKV_HEAD
    ones_bd = (idx[:, None] == idx[None, :]).astype(BF16)
    tok = [jax.ShapeDtypeStruct((t, d), dt) for dt in (F32, F32, BF16, F32, F32, F32, BF16)]
    halo = pl.BlockSpec((8, d), lambda i: (jnp.maximum(i * (tmr // 8) - 1, 0), 0))
    r, k, v, kkn, a, lw, g = pl.pallas_call(
        functools.partial(_rwkv_pre_kernel, tiles_per_seq=seq // tmr),
        grid=(t // tmr,),
        in_specs=[_rows(tmr, d), halo, _resident((6, d)), _resident((3, d, d)),
                  _resident((1, d)), _resident(w1.shape), _resident(w2.shape),
                  _resident((1, d)), _resident(a1.shape), _resident(a2.shape),
                  _resident(g1.shape), _resident(g2.shape), _resident((1, d)), _resident((1, d)),
                  _resident((LANES, LANES))],
        out_specs=[_rows(tmr, d)] * 7,
        out_shape=tok,
        compiler_params=_params("parallel"),
        name="rwkv_pre",
    )(x, x, mu, w_rkv.astype(BF16), vec(w0), w1.astype(BF16), w2.astype(BF16), vec(a0), a1.astype(BF16),
      a2.astype(BF16), g1.astype(BF16), g2.astype(BF16), vec(k_k), vec(k_a), ones_bd)

    tc = min(RWKV_STEP_CHUNKS * RWKV_CHUNK, seq)
    nt = seq // tc
    width = RWKV_STEP_PAIRS * LANES
    pair = pl.BlockSpec((tc, width), lambda b, p, i: (b * nt + i, p))
    o = pl.pallas_call(
        functools.partial(_rwkv_chunk_kernel, n_pairs=RWKV_STEP_PAIRS, n_chunks=tc // RWKV_CHUNK),
        grid=(batch, d // width, nt),
        in_specs=[pair] * 6,
        out_specs=pair,
        out_shape=jax.ShapeDtypeStruct((t, d), F32),
        scratch_shapes=[pltpu.VMEM((RWKV_STEP_PAIRS, LANES, LANES), F32)],
        compiler_params=_params("parallel", "parallel", "arbitrary"),
        name="rwkv_chunk",
    )(r, k, v, kkn, a, lw)

    return pl.pallas_call(
        _rwkv_post_kernel,
        grid=(t // tm,),
        in_specs=[_rows(tm, d)] * 6 + [_resident((1, d))] * 3 + [_resident((LANES, LANES)), _resident((d, d)),
                                                                _resident((1, d)), _resident((1, d))],
        out_specs=[_rows(tm, d), _rows(tm, d)],
        out_shape=[jax.ShapeDtypeStruct((t, d), F32), jax.ShapeDtypeStruct((t, d), BF16)],
        compiler_params=_params("parallel"),
        name="rwkv_post",
    )(o, r, k, v, g, x, vec(ln_w), vec(ln_b), vec(r_k), ones_bd, w_o.astype(BF16), vec(ln_g_post), vec(ln_b_post))


def _ffn_kernel(x_ref, xb_ref, halo_ref, wa_ref, wb_ref, cw_ref, cb_ref, wo_ref, g_ref, b_ref,
                xo_ref, xbo_ref, acc_ref, *, tiles_per_seq, n_chunks, halo_rows):
    first = (pl.program_id(0) % tiles_per_seq) == 0
    xb = xb_ref[...]
    tm = xb.shape[0]
    halo = halo_ref[...]
    halo = jnp.where(first, jnp.zeros_like(halo), halo)
    xe = jnp.concatenate([halo, xb], axis=0)
    acc_ref[...] = jnp.zeros_like(acc_ref)

    def up(c):
        return _dot(xe, wa_ref[c]), _dot(xb, wb_ref[c])

    def down(c, ha, hb):
        cw = cw_ref[c]
        conv = (cw[0:1, :] * ha[halo_rows - 2:halo_rows - 2 + tm, :]
                + cw[1:2, :] * ha[halo_rows - 1:halo_rows - 1 + tm, :]
                + cw[2:3, :] * ha[halo_rows:, :] + cb_ref[c])
        act = conv * jax.nn.sigmoid(conv) * hb
        acc_ref[...] += _dot(act.astype(BF16), wo_ref[c])

    ahead = 2
    pending = {c: up(c) for c in range(min(ahead, n_chunks))}
    for c in range(n_chunks):
        if c + ahead < n_chunks:
            pending[c + ahead] = up(c + ahead)
        down(c, *pending.pop(c))
    y = _post_ln(x_ref[...], acc_ref[...], g_ref[...], b_ref[...])
    xo_ref[...] = y
    xbo_ref[...] = y.astype(BF16)


def _ffn_layer(x, xb, batch, w_in, conv_w, conv_b, w_out, ln_g, ln_b, tm):
    t, d = x.shape
    seq = t // batch
    nch = D_FF_PAD // FF_CHUNK
    padc = D_FF_PAD - D_FF
    chunks = lambda w: jnp.pad(w, ((0, 0), (0, padc))).reshape(w.shape[0], nch, FF_CHUNK).transpose(1, 0, 2)
    wa = chunks(w_in[:, :D_FF]).astype(BF16)
    wb = chunks(w_in[:, D_FF:]).astype(BF16)
    cw = chunks(conv_w)
    cb = chunks(conv_b.reshape(1, D_FF))
    wo = jnp.pad(w_out, ((0, padc), (0, 0))).reshape(nch, FF_CHUNK, d).astype(BF16)
    halo_rows = 16
    halo = pl.BlockSpec((halo_rows, d), lambda i: (jnp.maximum(i * (tm // halo_rows) - 1, 0), 0))
    return pl.pallas_call(
        functools.partial(_ffn_kernel, tiles_per_seq=seq // tm, n_chunks=nch, halo_rows=halo_rows),
        grid=(t // tm,),
        in_specs=[_rows(tm, d), _rows(tm, d), halo, _resident(wa.shape), _resident(wb.shape), _resident(cw.shape),
                  _resident(cb.shape), _resident(wo.shape), _resident((1, d)), _resident((1, d))],
        out_specs=[_rows(tm, d), _rows(tm, d)],
        out_shape=[jax.ShapeDtypeStruct((t, d), F32), jax.ShapeDtypeStruct((t, d), BF16)],
        scratch_shapes=[pltpu.VMEM((tm, d), F32)],
        compiler_params=_params("parallel"),
        name="conv_ffn",
    )(x, xb, xb, wa, wb, cw, cb, wo, ln_g.reshape(1, d), ln_b.reshape(1, d))


def kernel(x, positions, ln_g, ln_b, mla_w_down, mla_q_norm, mla_kv_norm, mla_w_uq, mla_w_ukv, mla_w_o,
           dil_w_qkv, dil_w_o, rwkv_mu, rwkv_w_rkv, rwkv_w0, rwkv_w1, rwkv_w2, rwkv_a0, rwkv_a1, rwkv_a2,
           rwkv_g1, rwkv_g2, rwkv_k_k, rwkv_k_a, rwkv_r_k, rwkv_ln_w, rwkv_ln_b, rwkv_w_o,
           ffn_w_in, ffn_conv_w, ffn_conv_b, ffn_w_out):
    batch, seq, d = x.shape
    t = batch * seq
    tm = min(512, seq)
    xf = x.reshape(t, d)
    xb = xf.astype(BF16)
    cos, sin = _rope_tables(positions, tm)
    ia = ib = ic = 0
    for i in range(DEPTH):
        kind = i % 3
        if kind == 0:
            xf, xb = _mla_layer(xf, xb, cos, sin, batch, mla_w_down[ia], mla_q_norm[ia], mla_kv_norm[ia],
                                mla_w_uq[ia], mla_w_ukv[ia], mla_w_o[ia], ln_g[i, 0], ln_b[i, 0], tm)
            ia += 1
        elif kind == 1:
            xf, xb = _dil_layer(xf, xb, batch, dil_w_qkv[ib], dil_w_o[ib], ln_g[i, 0], ln_b[i, 0], tm)
            ib += 1
        else:
            xf, xb = _rwkv_layer(xf, batch, rwkv_mu[ic], rwkv_w_rkv[ic], rwkv_w0[ic], rwkv_w1[ic], rwkv_w2[ic],
                                 rwkv_a0[ic], rwkv_a1[ic], rwkv_a2[ic], rwkv_g1[ic], rwkv_g2[ic],
                                 rwkv_k_k[ic], rwkv_k_a[ic], rwkv_r_k[ic], rwkv_ln_w[ic], rwkv_ln_b[ic],
                                 rwkv_w_o[ic], ln_g[i, 0], ln_b[i, 0], tm)
            ic += 1
        xf, xb = _ffn_layer(xf, xb, batch, ffn_w_in[i], ffn_conv_w[i], ffn_conv_b[i], ffn_w_out[i],
                            ln_g[i, 1], ln_b[i, 1], tm)
    return xf.reshape(batch, seq, d)
```

```python
import functools

import jax
import jax.numpy as jnp
from jax import lax
from jax.experimental import pallas as pl
from jax.experimental.pallas import tpu as pltpu

F32 = jnp.float32
BF16 = jnp.bfloat16

D_MODEL = 1024
DEPTH = 4
MLA_HEADS = 16
MLA_Q_LORA = 384
MLA_KV_LORA = 256
MLA_NOPE = 64
MLA_ROPE = 32
MLA_V = 64
ROPE_THETA = 10000.0
DIL_GROUPS = ((128, 1), (512, 4), (2048, 16))
DIL_HEADS = 16
DIL_HEAD_DIM = 64
DIL_BLOCK = 128
DIL_STEP_BLOCKS = 4
DIL_STAGE_PAIRS = 4
DIL_LSE_LANES = 128 // DIL_HEADS
RWKV_HEAD = 64
RWKV_GN_EPS = 64e-5
D_FF = 2752
ALPHA = (2 * DEPTH) ** 0.25
LN_EPS = 1e-5
RMS_EPS = 1e-6
NEG_INF = -1e30
LOG2_E = 1.4426950408889634

LANES = 128
VMEM_LIMIT_BYTES = 56 * 2**20

SLOT = LANES
ROPE_LO = MLA_NOPE
ROPE_HALF = MLA_ROPE // 2
MLA_STEP_PAIRS = 4
FF_CHUNK = 256
D_FF_PAD = -(-D_FF // FF_CHUNK) * FF_CHUNK
RWKV_CHUNK = 64
RWKV_INV_BASE = 8
RWKV_STEP_PAIRS = 8
RWKV_STEP_CHUNKS = 2


def _params(*sem):
    return pltpu.CompilerParams(dimension_semantics=sem, vmem_limit_bytes=VMEM_LIMIT_BYTES)


def _dot(a, b):
    return jnp.dot(a, b, preferred_element_type=F32)


def _dot_nt(a, b):
    return lax.dot_general(a, b, (((1,), (1,)), ((), ())), preferred_element_type=F32)


def _resident(shape):
    nd = len(shape)
    return pl.BlockSpec(shape, lambda *_: (0,) * nd, pipeline_mode=pl.Buffered(1))


def _rows(tm, width):
    return pl.BlockSpec((tm, width), lambda i: (i, 0))


def _post_ln(x, h, g, b):
    y = ALPHA * x + h
    mu = jnp.mean(y, axis=-1, keepdims=True)
    d = y - mu
    var = jnp.mean(d * d, axis=-1, keepdims=True)
    return d * lax.rsqrt(var + LN_EPS) * g + b


def _rms(x, g):
    return x * lax.rsqrt(jnp.mean(x * x, axis=-1, keepdims=True) + RMS_EPS) * g


def _split_bf16(x, parts):
    out = []
    for _ in range(parts - 1):
        hi = x.astype(BF16)
        out.append(hi)
        x = x - hi.astype(F32)
    out.append(x.astype(BF16))
    return out


def _head_sums(x, ones_bd):
    cols = []
    for p in range(x.shape[1] // LANES):
        xs = x[:, p * LANES:(p + 1) * LANES]
        hi, lo = _split_bf16(xs, 2)
        cols.append(_dot(hi, ones_bd) + _dot(lo, ones_bd))
    return jnp.concatenate(cols, axis=1)


def _rope_table_kernel(pos_ref, freq_ref, cos_ref, sin_ref):
    ang = pos_ref[...].astype(F32) * freq_ref[...]
    cos_ref[...] = jnp.cos(ang)
    sin_ref[...] = jnp.sin(ang)


def _rope_tables(positions, tm):
    t = positions.size
    inv_freq = ROPE_THETA ** (-jnp.arange(0, MLA_ROPE, 2, dtype=F32) / MLA_ROPE)
    freq = jnp.zeros((1, SLOT), F32)
    freq = freq.at[0, ROPE_LO:ROPE_LO + ROPE_HALF].set(inv_freq)
    freq = freq.at[0, ROPE_LO + ROPE_HALF:ROPE_LO + MLA_ROPE].set(inv_freq)
    return pl.pallas_call(
        _rope_table_kernel,
        grid=(t // tm,),
        in_specs=[_rows(tm, 1), _resident((1, SLOT))],
        out_specs=[_rows(tm, SLOT), _rows(tm, SLOT)],
        out_shape=[jax.ShapeDtypeStruct((t, SLOT), F32)] * 2,
        compiler_params=_params("parallel"),
        name="rope_tables",
    )(positions.reshape(t, 1), freq)


def _rope_slots(z, cos, sin):
    width = z.shape[1]
    lane = lax.broadcasted_iota(jnp.int32, (1, width), 1) % SLOT
    first = (lane >= ROPE_LO) & (lane < ROPE_LO + ROPE_HALF)
    second = (lane >= ROPE_LO + ROPE_HALF) & (lane < ROPE_LO + MLA_ROPE)
    from_left = pltpu.roll(z, ROPE_HALF, axis=1)
    from_right = pltpu.roll(z, width - ROPE_HALF, axis=1)
    partner = jnp.where(second, from_left, jnp.where(first, -from_right, 0.0))
    return z * cos + partner * sin


def _rope_slots_partnered(z, cos, sin):
    width = z.shape[1]
    return z * cos + pltpu.roll(z, width - MLA_ROPE, axis=1) * sin


def _mla_qkv_kernel(xb_ref, wd_ref, qn_ref, kvn_ref, wq_ref, wkv_ref, cos_ref, sin_ref, q_ref, k_ref, vt_ref, *, scale):
    cos, sin = cos_ref[...], sin_ref[...]
    lat = _dot(xb_ref[...], wd_ref[...])
    cq = _rms(lat[:, :MLA_Q_LORA], qn_ref[...]).astype(BF16)
    ckv = _rms(lat[:, MLA_Q_LORA:MLA_Q_LORA + MLA_KV_LORA], kvn_ref[...]).astype(BF16)
    kpe = _rope_slots(lat[:, MLA_Q_LORA + MLA_KV_LORA:], cos, sin)
    q = _dot(cq, wq_ref[...])
    q_ref[...] = (_rope_slots_partnered(q, jnp.tile(cos, (1, MLA_HEADS)), jnp.tile(sin, (1, MLA_HEADS)))
                  * scale).astype(BF16)
    kv = _dot(ckv, wkv_ref[...])
    kw = MLA_HEADS * SLOT
    k_ref[...] = (kv[:, :kw] + jnp.tile(kpe, (1, MLA_HEADS))).astype(BF16)
    vt_ref[0] = kv[:, kw:].T.astype(BF16)


def _mla_attn_kernel(q_ref, k_ref, vt_ref, o_ref, *, tq, tk):
    assert tq == tk
    n_full = pl.program_id(2)
    heads = tuple(range(q_ref.shape[1] // SLOT))
    slots = tuple(slice(hh * SLOT, (hh + 1) * SLOT) for hh in heads)

    def own_rows(hh, width):
        sub = lax.broadcasted_iota(jnp.int32, (LANES, width), 0)
        return sub < MLA_V if hh % 2 == 0 else sub >= MLA_V

    def causal(keys, queries):
        return (lax.broadcasted_iota(jnp.int32, (keys, queries), 0)
                <= lax.broadcasted_iota(jnp.int32, (keys, queries), 1))

    def scores(j):
        start = pl.multiple_of(j * tk, tk)
        return tuple(_dot_nt(k_ref[pl.ds(start, tk), slot], q_ref[:, slot]) for slot in slots)

    def accumulate(s, state, vt, hh):
        m, acc = state
        m_new = jnp.maximum(m, jnp.max(s, axis=0, keepdims=True))
        alpha = jnp.exp2(m - m_new)
        p = jnp.exp2(s - m_new).astype(BF16)
        vt = jnp.where(own_rows(hh, vt.shape[1]), vt, jnp.ones(vt.shape, BF16))
        return m_new, alpha * acc + _dot(vt, p)

    def update(j, s_heads, states):
        return tuple(accumulate(s, st, vt_ref[j, hh // 2 * LANES:(hh // 2 + 1) * LANES, :], hh)
                     for hh, s, st in zip(heads, s_heads, states))

    def body(j, states):
        return update(j, scores(j), states)

    def body2(j2, states):
        s_even, s_odd = scores(2 * j2), scores(2 * j2 + 1)
        return update(2 * j2 + 1, s_odd, update(2 * j2, s_even, states))

    init = (jnp.full((1, tq), NEG_INF, F32), jnp.zeros((LANES, tq), F32))
    states = lax.fori_loop(0, n_full // 2, body2, (init,) * len(heads))
    states = lax.fori_loop(n_full - n_full % 2, n_full, body, states)
    s_diag = tuple(jnp.where(causal(tk, tq), s, NEG_INF) for s in scores(n_full))
    accs = [acc for _, acc in update(n_full, s_diag, states)]
    for pp in range(len(heads) // 2):
        acc0, acc1 = accs[2 * pp], accs[2 * pp + 1]
        out_t = jnp.where(own_rows(0, tq), acc0 / acc0[MLA_V:MLA_V + 1, :], acc1 / acc1[0:1, :])
        o_ref[:, pp * LANES:(pp + 1) * LANES] = out_t.T.astype(o_ref.dtype)


def _mla_layer(x, xb, cos, sin, batch, w_down, q_norm, kv_norm, w_uq, w_ukv, w_o, tm):
    t, d = x.shape
    seq = t // batch
    h = MLA_HEADS
    pad = jnp.zeros((d, ROPE_LO), F32)
    w_down_s = jnp.concatenate(
        [w_down[:, :MLA_Q_LORA + MLA_KV_LORA], pad, w_down[:, MLA_Q_LORA + MLA_KV_LORA:],
         jnp.zeros((d, SLOT - ROPE_LO - MLA_ROPE), F32)], axis=1).astype(BF16)
    w_uq3 = w_uq.reshape(MLA_Q_LORA, h, MLA_NOPE + MLA_ROPE)
    w_partner = jnp.concatenate([-w_uq3[:, :, MLA_NOPE + ROPE_HALF:], w_uq3[:, :, MLA_NOPE:MLA_NOPE + ROPE_HALF]], axis=2)
    w_uq_s = jnp.concatenate([w_uq3, w_partner], axis=2).reshape(MLA_Q_LORA, h * SLOT).astype(BF16)
    w_ukv3 = w_ukv.reshape(MLA_KV_LORA, h, MLA_NOPE + MLA_V)
    w_k_s = jnp.pad(w_ukv3[:, :, :MLA_NOPE], ((0, 0), (0, 0), (0, SLOT - MLA_NOPE))).reshape(MLA_KV_LORA, h * SLOT)
    w_v_s = w_ukv3[:, :, MLA_NOPE:].reshape(MLA_KV_LORA, h * MLA_V)
    w_kv_s = jnp.concatenate([w_k_s, w_v_s], axis=1).astype(BF16)

    assert SLOT == MLA_NOPE + 2 * MLA_ROPE
    tq = tk = tm
    nq = seq // tq
    q, k, vt = pl.pallas_call(
        functools.partial(_mla_qkv_kernel, scale=(MLA_NOPE + MLA_ROPE) ** -0.5 * LOG2_E),
        grid=(t // tm,),
        in_specs=[_rows(tm, d), _resident(w_down_s.shape), _resident((1, MLA_Q_LORA)), _resident((1, MLA_KV_LORA)),
                  _resident(w_uq_s.shape), _resident(w_kv_s.shape), _rows(tm, SLOT), _rows(tm, SLOT)],
        out_specs=[_rows(tm, h * SLOT), _rows(tm, h * SLOT), pl.BlockSpec((1, h * MLA_V, tm), lambda i: (i, 0, 0))],
        out_shape=[jax.ShapeDtypeStruct((t, h * SLOT), BF16), jax.ShapeDtypeStruct((t, h * SLOT), BF16),
                   jax.ShapeDtypeStruct((t // tm, h * MLA_V, tm), BF16)],
        compiler_params=_params("parallel"),
        name="mla_qkv",
    )(xb, w_down_s, q_norm.reshape(1, -1), kv_norm.reshape(1, -1), w_uq_s, w_kv_s, cos, sin)

    hps = 2 * MLA_STEP_PAIRS
    o = pl.pallas_call(
        functools.partial(_mla_attn_kernel, tq=tq, tk=tk),
        grid=(batch, h // hps, nq),
        in_specs=[pl.BlockSpec((tq, hps * SLOT), lambda b, p, i: (b * nq + i, p)),
                  pl.BlockSpec((seq, hps * SLOT), lambda b, p, i: (b, p)),
                  pl.BlockSpec((nq, hps * MLA_V, tk), lambda b, p, i: (b, p, 0))],
        out_specs=pl.BlockSpec((tq, hps * MLA_V), lambda b, p, i: (b * nq + i, p)),
        out_shape=jax.ShapeDtypeStruct((t, h * MLA_V), BF16),
        compiler_params=_params("parallel", "parallel", "arbitrary"),
        name="mla_attn",
    )(q, k, vt)

    return o, w_o.astype(BF16)


def _dil_qkv_kernel(x_ref, w_ref, o_ref, *scratch, dil):
    if dil == 1:
        o_ref[0, 0] = _dot(x_ref[...].astype(BF16), w_ref[...]).astype(o_ref.dtype)
        return
    lanes_ref, xperm_ref = scratch
    rows = x_ref.shape[0] // dil

    @pl.when(pl.program_id(1) == 0)
    def _():
        for p in range(x_ref.shape[1] // LANES):
            sl = slice(p * LANES, (p + 1) * LANES)
            lanes_ref[p] = x_ref[:, sl]
            for r in range(dil):
                xperm_ref[r * rows:(r + 1) * rows, sl] = lanes_ref[p, pl.ds(r, rows, stride=dil), :].astype(BF16)

    res = _dot(xperm_ref[...], w_ref[...])
    for r in range(dil):
        o_ref[0, r] = res[r * rows:(r + 1) * rows, :].astype(o_ref.dtype)


def _dil_attn_kernel(q_ref, kp_ref, kc_ref, vp_ref, vc_ref, o_ref, lse_ref, *, n_sub):
    q_ref, kp_ref, kc_ref, vp_ref, vc_ref, o_ref, lse_ref = (
        ref.at[0, 0] for ref in (q_ref, kp_ref, kc_ref, vp_ref, vc_ref, o_ref, lse_ref))
    bk = DIL_BLOCK
    i = lax.broadcasted_iota(jnp.int32, (bk, 2 * bk), 0)
    j = lax.broadcasted_iota(jnp.int32, (bk, 2 * bk), 1)
    dist = i + bk - j
    band = (dist >= 0) & (dist <= bk)
    prev_exists = (pl.program_id(2) > 0) | (j >= bk)
    lse_lane = lax.broadcasted_iota(jnp.int32, (1, LANES), 1)
    lo = lse_lane < DIL_HEAD_DIM
    pairs = [slice(p * LANES, (p + 1) * LANES) for p in range(DIL_HEADS // 2)]
    for sb in range(n_sub):
        rows = slice(sb * bk, (sb + 1) * bk)
        valid = band & prev_exists if sb == 0 else band

        def window(prev_ref, cur_ref, sl, sb=sb):
            if sb == 0:
                return jnp.concatenate([prev_ref[:, sl], cur_ref[:bk, sl]], axis=0)
            return cur_ref[(sb - 1) * bk:(sb + 1) * bk, sl]

        def pair_scores_of(sl):
            q = q_ref[rows, sl]
            k = window(kp_ref, kc_ref, sl)
            zero = jnp.zeros_like(q)
            return [_dot_nt(qh, k) for qh in (jnp.where(lo, q, zero), jnp.where(lo, zero, q))]

        lse_blk = jnp.zeros((bk, LANES), F32)
        scores = {}
        for p, sl in enumerate(pairs):
            if p % DIL_STAGE_PAIRS == 0:
                scores = {pp: pair_scores_of(pairs[pp]) for pp in range(p, min(p + DIL_STAGE_PAIRS, len(pairs)))}
            pair_scores = scores[p]
            v = window(vp_ref, vc_ref, sl)
            res = []
            for s in pair_scores:
                s = jnp.where(valid, s, NEG_INF)
                m = jnp.max(s, axis=-1, keepdims=True)
                e = jnp.exp(s - m)
                l = jnp.sum(e, axis=-1, keepdims=True)
                res.append((_dot(e.astype(BF16), v) / l, m + jnp.log(l)))
            o_ref[rows, sl] = jnp.where(lo, res[0][0], res[1][0]).astype(o_ref.dtype)
            lse_blk = jnp.where(lse_lane // DIL_LSE_LANES == 2 * p, res[0][1],
                                jnp.where(lse_lane // DIL_LSE_LANES == 2 * p + 1, res[1][1], lse_blk))
        lse_ref[rows, :] = lse_blk


def _dil_out_kernel(o1_ref, o2_ref, o3_ref, l1_ref, l2_ref, l3_ref, expand_ref, w_ref, x_ref, g_ref, b_ref,
                    xo_ref, xbo_ref, *bufs):
    bufs = list(bufs)

    def token_order(ref, dil):
        if dil == 1:
            return ref[0, 0].astype(F32)
        n_blocks = ref.shape[3] // LANES
        buf = bufs.pop()
        for p in range(n_blocks):
            for r in range(dil):
                buf[p, pl.ds(r, ref.shape[2], stride=dil), :] = ref[0, r, :, p * LANES:(p + 1) * LANES].astype(F32)
        return jnp.concatenate([buf[p] for p in range(n_blocks)], axis=1)

    dils = [dil for _, dil in DIL_GROUPS]
    l1, l2, l3 = (token_order(ref, dil) for ref, dil in zip((l1_ref, l2_ref, l3_ref), dils))
    m = jnp.maximum(jnp.maximum(l1, l2), l3)
    e1, e2, e3 = jnp.exp(l1 - m), jnp.exp(l2 - m), jnp.exp(l3 - m)
    inv = 1.0 / (e1 + e2 + e3)
    w1, w2, w3 = (_dot((e * inv).astype(BF16), expand_ref[...]) for e in (e1, e2, e3))
    o1, o2, o3 = (token_order(ref, dil) for ref, dil in zip((o1_ref, o2_ref, o3_ref), dils))
    o = w1 * o1 + w2 * o2 + w3 * o3
    y = _post_ln(x_ref[...], _dot(o.astype(BF16), w_ref[...]), g_ref[...], b_ref[...])
    xo_ref[...] = y
    xbo_ref[...] = y.astype(BF16)


def _dil_layer(x, xb, batch, w_qkv, w_o, ln_g, ln_b, tm):
    t, d = x.shape
    seq = t // batch
    ng = len(DIL_GROUPS)
    hd = DIL_HEADS * DIL_HEAD_DIM
    nqkv = ng * 3 * hd
    col_kind = (jnp.arange(nqkv) // hd) % 3
    w_s = (w_qkv * jnp.where(col_kind == 0, DIL_HEAD_DIM ** -0.5, 1.0)[None, :]).astype(BF16)
    tps = seq // tm
    bk = DIL_BLOCK
    outs, lses = [], []
    for gi, (window, dil) in enumerate(DIL_GROUPS):
        assert window // dil == bk and tm % (16 * dil) == 0
        assert seq % (dil * bk * min(DIL_STEP_BLOCKS, seq // (dil * bk))) == 0
        tmq = min(2 * tm, seq)
        tpq = seq // tmq
        qkv = pl.pallas_call(
            functools.partial(_dil_qkv_kernel, dil=dil),
            grid=(t // tmq, 3),
            in_specs=[pl.BlockSpec((tmq, d), lambda i, c: (i, 0)),
                      pl.BlockSpec((d, hd), lambda i, c, gi=gi: (0, gi * 3 + c))],
            out_specs=pl.BlockSpec((1, dil, tmq // dil, hd), lambda i, c, tpq=tpq: (i // tpq, 0, i % tpq, c)),
            out_shape=jax.ShapeDtypeStruct((batch, dil, seq // dil, 3 * hd), BF16),
            scratch_shapes=[] if dil == 1 else [pltpu.VMEM((d // LANES, tmq, LANES), F32), pltpu.VMEM((tmq, d), BF16)],
            compiler_params=_params("parallel", "arbitrary"),
            name=f"dil_qkv_d{dil}",
        )(xb if dil == 1 else x, w_s)

        n_sub = min(DIL_STEP_BLOCKS, seq // (dil * bk))

        def spec(kind, prev, n_sub=n_sub):
            if prev:
                return pl.BlockSpec((1, 1, bk, hd), lambda b, r, n: (b, r, jnp.maximum(n * n_sub - 1, 0), kind))
            return pl.BlockSpec((1, 1, n_sub * bk, hd), lambda b, r, n: (b, r, n, kind))

        out_specs = [pl.BlockSpec((1, 1, n_sub * bk, width), lambda b, r, n: (b, r, n, 0)) for width in (hd, LANES)]
        o, lse = pl.pallas_call(
            functools.partial(_dil_attn_kernel, n_sub=n_sub),
            grid=(batch, dil, seq // (dil * bk * n_sub)),
            in_specs=[spec(0, False), spec(1, True), spec(1, False), spec(2, True), spec(2, False)],
            out_specs=out_specs,
            out_shape=[jax.ShapeDtypeStruct((batch, dil, seq // dil, hd), BF16),
                       jax.ShapeDtypeStruct((batch, dil, seq // dil, LANES), F32)],
            compiler_params=_params("parallel", "parallel", "arbitrary"),
            name=f"dil_attn_d{dil}",
        )(qkv, qkv, qkv, qkv, qkv)
        outs.append(o)
        lses.append(lse)

    def res_specs(width):
        return [pl.BlockSpec((1, dil, tm // dil, width), lambda i: (i // tps, 0, i % tps, 0)) for _, dil in DIL_GROUPS]

    expand = (jnp.arange(LANES)[:, None] == DIL_LSE_LANES * (jnp.arange(hd) // DIL_HEAD_DIM)[None, :]).astype(BF16)
    n_strided = sum(dil > 1 for _, dil in DIL_GROUPS)
    bufs = [pltpu.VMEM((hd // LANES, tm, LANES), F32)] * n_strided + [pltpu.VMEM((1, tm, LANES), F32)] * n_strided
    return pl.pallas_call(
        _dil_out_kernel,
        grid=(t // tm,),
        in_specs=res_specs(hd) + res_specs(LANES) + [_resident((LANES, hd)), _resident((hd, d)), _rows(tm, d),
                                                     _resident((1, d)), _resident((1, d))],
        out_specs=[_rows(tm, d), _rows(tm, d)],
        out_shape=[jax.ShapeDtypeStruct((t, d), F32), jax.ShapeDtypeStruct((t, d), BF16)],
        scratch_shapes=bufs,
        compiler_params=_params("parallel"),
        name="dil_out",
    )(*outs, *lses, expand, w_o.astype(BF16), x, ln_g.reshape(1, d), ln_b.reshape(1, d))


def _rwkv_pre_kernel(x_ref, halo_ref, mu_ref, wrkv_ref, w0_ref, w1_ref, w2_ref, a0_ref, a1_ref, a2_ref,
                     g1_ref, g2_ref, kk_ref, ka_ref, bd_ref,
                     r_o, k_o, v_o, kkn_o, a_o, lw_o, g_o, *, tiles_per_seq):
    first = (pl.program_id(0) % tiles_per_seq) == 0
    x = x_ref[...]
    tm = x.shape[0]
    prev_row = jnp.where(first, 0.0, halo_ref[7:8, :])
    row = lax.broadcasted_iota(jnp.int32, (tm, 1), 0)
    xx = jnp.where(row == 0, prev_row, pltpu.roll(x, 1, axis=0)) - x
    mu = mu_ref[...]

    def mix(j):
        return (x + xx * mu[j:j + 1, :]).astype(BF16)

    r = _dot(mix(0), wrkv_ref[0])
    k = _dot(mix(1), wrkv_ref[1])
    v = _dot(mix(2), wrkv_ref[2])
    z = w0_ref[...] + _dot(jnp.tanh(_dot(mix(3), w1_ref[...])).astype(BF16), w2_ref[...])
    nz = -z
    softplus = jnp.maximum(nz, 0.0) + jnp.log(1.0 + jnp.exp(-jnp.abs(nz)))
    lw = -jnp.exp(-softplus - 0.5)
    a = jax.nn.sigmoid(a0_ref[...] + _dot(_dot(mix(4), a1_ref[...]).astype(BF16), a2_ref[...]))
    g = _dot(jax.nn.sigmoid(_dot(mix(5), g1_ref[...])).astype(BF16), g2_ref[...])
    kk = k * kk_ref[...]
    norm = jnp.sqrt(_head_sums(kk * kk, bd_ref[...]))
    r_o[...] = r
    k_o[...] = k * (1.0 + (a - 1.0) * ka_ref[...])
    v_o[...] = v.astype(v_o.dtype)
    kkn_o[...] = kk / jnp.maximum(norm, 1e-12)
    a_o[...] = a
    lw_o[...] = lw
    g_o[...] = g.astype(g_o.dtype)


def _rwkv_chunk_kernel(r_ref, k_ref, v_ref, kk_ref, a_ref, lw_ref, o_ref, s_ref, *, n_pairs, n_chunks):
    c = RWKV_CHUNK

    @pl.when(pl.program_id(2) == 0)
    def _():
        s_ref[...] = jnp.zeros_like(s_ref)

    row = lax.broadcasted_iota(jnp.int32, (c, LANES), 0)
    col = lax.broadcasted_iota(jnp.int32, (c, LANES), 1) % RWKV_HEAD
    blk = {}
    size = RWKV_INV_BASE
    while size < c:
        blk[size] = (row // size) == (col // size)
        size *= 2
    strict = col < row
    incl = col <= row
    eye = jnp.where(row == col, 1.0, 0.0).astype(F32)
    same_head = (lax.broadcasted_iota(jnp.int32, (LANES, LANES), 0) // RWKV_HEAD
                 == lax.broadcasted_iota(jnp.int32, (LANES, LANES), 1) // RWKV_HEAD)
    tri = jnp.where(lax.broadcasted_iota(jnp.int32, (c, c), 1) <= lax.broadcasted_iota(jnp.int32, (c, c), 0),
                    1.0, 0.0).astype(BF16)
    lane_lo = lax.broadcasted_iota(jnp.int32, (1, LANES), 1) < RWKV_HEAD

    def stack(x):
        x = x.astype(BF16)
        zero = jnp.zeros_like(x)
        return jnp.concatenate([jnp.where(lane_lo, x, zero), jnp.where(lane_lo, zero, x)], axis=0)

    def bf(x):
        return x.astype(BF16)

    streams = [(ch, p) for ch in range(n_chunks) for p in range(n_pairs)]

    def load(ref):
        return [ref[ch * c:(ch + 1) * c, p * LANES:(p + 1) * LANES] for ch, p in streams]

    rc, kc, vc, kkc, ac, lwc = (load(ref) for ref in (r_ref, k_ref, v_ref, kk_ref, a_ref, lw_ref))

    cum = []
    for x in lwc:
        cs = _dot(tri, jnp.concatenate(_split_bf16(x, 3), axis=1))
        cum.append(cs[:, :LANES] + cs[:, LANES:2 * LANES] + cs[:, 2 * LANES:])
    p_incl = [jnp.exp(x) for x in cum]
    p_inv = [jnp.exp(-x) for x in cum]
    p_end = [x[c - 1:c, :] for x in p_incl]
    at = [bf(-kk * jnp.exp(cm - lw)) for kk, cm, lw in zip(kkc, cum, lwc)]
    bt = [bf(kk * a * pi) for kk, a, pi in zip(kkc, ac, p_inv)]
    kt = [bf(k * pi) for k, pi in zip(kc, p_inv)]
    rt = [r * pf for r, pf in zip(rc, p_incl)]
    v_st = [stack(x) for x in vc]

    prod = [_dot_nt(jnp.concatenate([a, bf(r)], axis=0), jnp.concatenate([stack(b), stack(k)], axis=0))
            for a, r, b, k in zip(at, rt, bt, kt)]
    l_ab = [jnp.where(strict, x[:c, :LANES], 0.0) for x in prod]
    l_ak = [bf(jnp.where(strict, x[:c, LANES:], 0.0)) for x in prod]
    m_rb = [bf(jnp.where(incl, x[c:, :LANES], 0.0)) for x in prod]
    m_rk = [bf(jnp.where(incl, x[c:, LANES:], 0.0)) for x in prod]

    dg = [bf(jnp.where(blk[RWKV_INV_BASE], x, 0.0)) for x in l_ab]
    d2 = [bf(_dot(x, stack(x))) for x in dg]
    d4 = [_dot(x, stack(x)) for x in d2]
    tinv = [eye + x.astype(F32) for x in dg]
    tinv = [t + _dot(bf(t), stack(y)) for t, y in zip(tinv, d2)]
    tinv = [t + _dot(bf(t), stack(y)) for t, y in zip(tinv, d4)]
    size = RWKV_INV_BASE
    while size < c:
        joins = jnp.logical_not(blk[size]) if 2 * size == c else blk[2 * size] & jnp.logical_not(blk[size])
        off = [jnp.where(joins, x, 0.0) for x in l_ab]
        te = [_dot(bf(t), stack(e)) for t, e in zip(tinv, off)]
        tinv = [t + _dot(bf(x), stack(t)) for t, x in zip(tinv, te)]
        size *= 2

    lakv = [_dot(x, v) for x, v in zip(l_ak, v_st)]
    gu = [_dot(bf(t), jnp.concatenate([stack(a), stack(y)], axis=1)) for t, a, y in zip(tinv, at, lakv)]
    g_nat = [x[:, :LANES] for x in gu]
    uv_nat = [x[:, LANES:] for x in gu]
    mg = [_dot(m, jnp.concatenate([stack(g), stack(u)], axis=1)) for m, g, u in zip(m_rb, g_nat, uv_nat)]
    rg = [bf(r + x[:, :LANES]) for r, x in zip(rt, mg)]
    o_v = [x[:, LANES:] + _dot(m, v) for x, m, v in zip(mg, m_rk, v_st)]
    q_mat = [bf(jnp.where(same_head, _dot(bf(g.T), b), 0.0) * pe) for g, b, pe in zip(g_nat, bt, p_end)]
    n_mat = [jnp.where(same_head, _dot(bf(jnp.concatenate([u, v.astype(F32)], axis=0).T),
                                       jnp.concatenate([b, k], axis=0)), 0.0) * pe
             for u, v, b, k, pe in zip(uv_nat, vc, bt, kt, p_end)]

    state = [s_ref[p] for p in range(n_pairs)]
    for i, (ch, p) in enumerate(streams):
        sb = bf(state[p])
        o_ref[ch * c:(ch + 1) * c, p * LANES:(p + 1) * LANES] = _dot_nt(rg[i], sb) + o_v[i]
        state[p] = state[p] * p_end[i] + _dot(sb, q_mat[i]) + n_mat[i]
    for p in range(n_pairs):
        s_ref[p] = state[p]


def _rwkv_post_kernel(o_ref, r_ref, k_ref, v_ref, g_ref, x_ref, lnw_ref, lnb_ref, rk_ref, bd_ref, w_ref,
                      gl_ref, bl_ref, xo_ref, xbo_ref):
    bd = bd_ref[...]
    o = o_ref[...]
    mean = _head_sums(o, bd) * (1.0 / RWKV_HEAD)
    dlt = o - mean
    var = _head_sums(dlt * dlt, bd) * (1.0 / RWKV_HEAD)
    y = dlt * lax.rsqrt(var + RWKV_GN_EPS) * lnw_ref[...] + lnb_ref[...]
    bonus = _head_sums(r_ref[...] * k_ref[...] * rk_ref[...], bd) * v_ref[...]
    out = ((y + bonus) * g_ref[...]).astype(BF16)
    res = _post_ln(x_ref[...], _dot(out, w_ref[...]), gl_ref[...], bl_ref[...])
    xo_ref[...] = res
    xbo_ref[...] = res.astype(BF16)


def _rwkv_layer(x, batch, mu, w_rkv, w0, w1, w2, a0, a1, a2, g1, g2, k_k, k_a, r_k, ln_w, ln_b, w_o,
                ln_g_post, ln_b_post, tm):
    t, d = x.shape
    seq = t // batch
    tmr = min(256, seq)
    vec = lambda p: p.reshape(1, d)
    idx = jnp.arange(LANES) // RWKV_HEAD
    ones_bd = (idx[:, None] == idx[None, :]).astype(BF16)
    tok = [jax.ShapeDtypeStruct((t, d), dt) for dt in (F32, F32, BF16, F32, F32, F32, BF16)]
    halo = pl.BlockSpec((8, d), lambda i: (jnp.maximum(i * (tmr // 8) - 1, 0), 0))
    r, k, v, kkn, a, lw, g = pl.pallas_call(
        functools.partial(_rwkv_pre_kernel, tiles_per_seq=seq // tmr),
        grid=(t // tmr,),
        in_specs=[_rows(tmr, d), halo, _resident((6, d)), _resident((3, d, d)),
                  _resident((1, d)), _resident(w1.shape), _resident(w2.shape),
                  _resident((1, d)), _resident(a1.shape), _resident(a2.shape),
                  _resident(g1.shape), _resident(g2.shape), _resident((1, d)), _resident((1, d)),
                  _resident((LANES, LANES))],
        out_specs=[_rows(tmr, d)] * 7,
        out_shape=tok,
        compiler_params=_params("parallel"),
        name="rwkv_pre",
    )(x, x, mu, w_rkv.astype(BF16), vec(w0), w1.astype(BF16), w2.astype(BF16), vec(a0), a1.astype(BF16),
      a2.astype(BF16), g1.astype(BF16), g2.astype(BF16), vec(k_k), vec(k_a), ones_bd)

    tc = min(RWKV_STEP_CHUNKS * RWKV_CHUNK, seq)
    nt = seq // tc
    width = RWKV_STEP_PAIRS * LANES
    pair = pl.BlockSpec((tc, width), lambda b, p, i: (b * nt + i, p))
    o = pl.pallas_call(
        functools.partial(_rwkv_chunk_kernel, n_pairs=RWKV_STEP_PAIRS, n_chunks=tc // RWKV_CHUNK),
        grid=(batch, d // width, nt),
        in_specs=[pair] * 6,
        out_specs=pair,
        out_shape=jax.ShapeDtypeStruct((t, d), F32),
        scratch_shapes=[pltpu.VMEM((RWKV_STEP_PAIRS, LANES, LANES), F32)],
        compiler_params=_params("parallel", "parallel", "arbitrary"),
        name="rwkv_chunk",
    )(r, k, v, kkn, a, lw)

    return pl.pallas_call(
        _rwkv_post_kernel,
        grid=(t // tm,),
        in_specs=[_rows(tm, d)] * 6 + [_resident((1, d))] * 3 + [_resident((LANES, LANES)), _resident((d, d)),
                                                                _resident((1, d)), _resident((1, d))],
        out_specs=[_rows(tm, d), _rows(tm, d)],
        out_shape=[jax.ShapeDtypeStruct((t, d), F32), jax.ShapeDtypeStruct((t, d), BF16)],
        compiler_params=_params("parallel"),
        name="rwkv_post",
    )(o, r, k, v, g, x, vec(ln_w), vec(ln_b), vec(r_k), ones_bd, w_o.astype(BF16), vec(ln_g_post), vec(ln_b_post))


def _ffn_core(x1, xe, wa_ref, wb_ref, cw_ref, cb_ref, wo_ref, g_ref, b_ref, xo_ref, xbo_ref, acc_ref,
              *, n_chunks, halo_rows):
    tm = x1.shape[0]
    xb = xe[halo_rows:, :]
    acc_ref[...] = jnp.zeros_like(acc_ref)

    def up(c):
        return _dot(xe, wa_ref[c]), _dot(xb, wb_ref[c])

    def down(c, ha, hb):
        cw = cw_ref[c]
        conv = (cw[0:1, :] * ha[halo_rows - 2:halo_rows - 2 + tm, :]
                + cw[1:2, :] * ha[halo_rows - 1:halo_rows - 1 + tm, :]
                + cw[2:3, :] * ha[halo_rows:, :] + cb_ref[c])
        act = conv * jax.nn.sigmoid(conv) * hb
        acc_ref[...] += _dot(act.astype(BF16), wo_ref[c])

    ahead = 2
    pending = {c: up(c) for c in range(min(ahead, n_chunks))}
    for c in range(n_chunks):
        if c + ahead < n_chunks:
            pending[c + ahead] = up(c + ahead)
        down(c, *pending.pop(c))
    y = _post_ln(x1, acc_ref[...], g_ref[...], b_ref[...])
    xo_ref[...] = y
    xbo_ref[...] = y.astype(BF16)


def _ffn_kernel(x_ref, xb_ref, halo_ref, *rest, tiles_per_seq, n_chunks, halo_rows):
    first = (pl.program_id(0) % tiles_per_seq) == 0
    halo = halo_ref[...]
    halo = jnp.where(first, jnp.zeros_like(halo), halo)
    xe = jnp.concatenate([halo, xb_ref[...]], axis=0)
    _ffn_core(x_ref[...], xe, *rest, n_chunks=n_chunks, halo_rows=halo_rows)


def _proj_ffn_kernel(a_ref, ahalo_ref, x_ref, xhalo_ref, wp_ref, g1_ref, b1_ref, *rest,
                     tiles_per_seq, n_chunks, halo_rows):
    first = (pl.program_id(0) % tiles_per_seq) == 0
    a_ext = jnp.concatenate([ahalo_ref[...], a_ref[...]], axis=0)
    x_ext = jnp.concatenate([xhalo_ref[...], x_ref[...]], axis=0)
    x1_ext = _post_ln(x_ext, _dot(a_ext, wp_ref[...]), g1_ref[...], b1_ref[...])
    row = lax.broadcasted_iota(jnp.int32, (x1_ext.shape[0], 1), 0)
    x1_ext = jnp.where(first & (row < halo_rows), 0.0, x1_ext)
    _ffn_core(x1_ext[halo_rows:, :], x1_ext.astype(BF16), *rest, n_chunks=n_chunks, halo_rows=halo_rows)


def _ffn_layer(x, xb, batch, w_in, conv_w, conv_b, w_out, ln_g, ln_b, tm, pre=None):
    t, d = x.shape
    seq = t // batch
    nch = D_FF_PAD // FF_CHUNK
    padc = D_FF_PAD - D_FF
    chunks = lambda w: jnp.pad(w, ((0, 0), (0, padc))).reshape(w.shape[0], nch, FF_CHUNK).transpose(1, 0, 2)
    wa = chunks(w_in[:, :D_FF]).astype(BF16)
    wb = chunks(w_in[:, D_FF:]).astype(BF16)
    cw = chunks(conv_w)
    cb = chunks(conv_b.reshape(1, D_FF))
    wo = jnp.pad(w_out, ((0, padc), (0, 0))).reshape(nch, FF_CHUNK, d).astype(BF16)
    halo_rows = 16
    halo = lambda width: pl.BlockSpec((halo_rows, width), lambda i: (jnp.maximum(i * (tm // halo_rows) - 1, 0), 0))
    ffn_specs = [_resident(wa.shape), _resident(wb.shape), _resident(cw.shape), _resident(cb.shape),
                 _resident(wo.shape), _resident((1, d)), _resident((1, d))]
    ffn_args = (wa, wb, cw, cb, wo, ln_g.reshape(1, d), ln_b.reshape(1, d))
    static = dict(tiles_per_seq=seq // tm, n_chunks=nch, halo_rows=halo_rows)
    if pre is None:
        body = functools.partial(_ffn_kernel, **static)
        in_specs = [_rows(tm, d), _rows(tm, d), halo(d)] + ffn_specs
        args = (x, xb, xb) + ffn_args
    else:
        a, w, g1, b1 = pre
        body = functools.partial(_proj_ffn_kernel, **static)
        in_specs = [_rows(tm, a.shape[1]), halo(a.shape[1]), _rows(tm, d), halo(d), _resident(w.shape),
                    _resident((1, d)), _resident((1, d))] + ffn_specs
        args = (a, a, x, x, w, g1.reshape(1, d), b1.reshape(1, d)) + ffn_args
    return pl.pallas_call(
        body,
        grid=(t // tm,),
        in_specs=in_specs,
        out_specs=[_rows(tm, d), _rows(tm, d)],
        out_shape=[jax.ShapeDtypeStruct((t, d), F32), jax.ShapeDtypeStruct((t, d), BF16)],
        scratch_shapes=[pltpu.VMEM((tm, d), F32)],
        compiler_params=_params("parallel"),
        name="conv_ffn" if pre is None else "proj_conv_ffn",
    )(*args)


def kernel(x, positions, ln_g, ln_b, mla_w_down, mla_q_norm, mla_kv_norm, mla_w_uq, mla_w_ukv, mla_w_o,
           dil_w_qkv, dil_w_o, rwkv_mu, rwkv_w_rkv, rwkv_w0, rwkv_w1, rwkv_w2, rwkv_a0, rwkv_a1, rwkv_a2,
           rwkv_g1, rwkv_g2, rwkv_k_k, rwkv_k_a, rwkv_r_k, rwkv_ln_w, rwkv_ln_b, rwkv_w_o,
           ffn_w_in, ffn_conv_w, ffn_conv_b, ffn_w_out):
    batch, seq, d = x.shape
    t = batch * seq
    tm = min(512, seq)
    xf = x.reshape(t, d)
    xb = xf.astype(BF16)
    cos, sin = _rope_tables(positions, tm)
    ia = ib = ic = 0
    for i in range(DEPTH):
        kind = i % 3
        pre = None
        if kind == 0:
            pre = _mla_layer(xf, xb, cos, sin, batch, mla_w_down[ia], mla_q_norm[ia], mla_kv_norm[ia],
                             mla_w_uq[ia], mla_w_ukv[ia], mla_w_o[ia], tm) + (ln_g[i, 0], ln_b[i, 0])
            ia += 1
        elif kind == 1:
            xf, xb = _dil_layer(xf, xb, batch, dil_w_qkv[ib], dil_w_o[ib], ln_g[i, 0], ln_b[i, 0], tm)
            ib += 1
        else:
            xf, xb = _rwkv_layer(xf, batch, rwkv_mu[ic], rwkv_w_rkv[ic], rwkv_w0[ic], rwkv_w1[ic], rwkv_w2[ic],
                                 rwkv_a0[ic], rwkv_a1[ic], rwkv_a2[ic], rwkv_g1[ic], rwkv_g2[ic],
                                 rwkv_k_k[ic], rwkv_k_a[ic], rwkv_r_k[ic], rwkv_ln_w[ic], rwkv_ln_b[ic],
                                 rwkv_w_o[ic], ln_g[i, 0], ln_b[i, 0], tm)
            ic += 1
        xf, xb = _ffn_layer(xf, xb, batch, ffn_w_in[i], ffn_conv_w[i], ffn_conv_b[i], ffn_w_out[i],
                            ln_g[i, 1], ln_b[i, 1], tm, pre=pre)
    return xf.reshape(batch, seq, d)
```

```python
import functools

import jax
import jax.numpy as jnp
from jax import lax
from jax.experimental import pallas as pl
from jax.experimental.pallas import tpu as pltpu

F32 = jnp.float32
BF16 = jnp.bfloat16

D_MODEL = 1024
DEPTH = 4
MLA_HEADS = 16
MLA_Q_LORA = 384
MLA_KV_LORA = 256
MLA_NOPE = 64
MLA_ROPE = 32
MLA_V = 64
ROPE_THETA = 10000.0
DIL_GROUPS = ((128, 1), (512, 4), (2048, 16))
DIL_HEADS = 16
DIL_HEAD_DIM = 64
DIL_BLOCK = 128
DIL_STEP_BLOCKS = 4
DIL_STAGE_PAIRS = 4
DIL_LSE_LANES = 128 // DIL_HEADS
RWKV_HEAD = 64
RWKV_GN_EPS = 64e-5
D_FF = 2752
ALPHA = (2 * DEPTH) ** 0.25
LN_EPS = 1e-5
RMS_EPS = 1e-6
NEG_INF = -1e30
LOG2_E = 1.4426950408889634

LANES = 128
VMEM_LIMIT_BYTES = 56 * 2**20

SLOT = LANES
ROPE_LO = MLA_NOPE
ROPE_HALF = MLA_ROPE // 2
MLA_STEP_PAIRS = 4
FF_CHUNK = 256
D_FF_PAD = -(-D_FF // FF_CHUNK) * FF_CHUNK
RWKV_CHUNK = 64
RWKV_INV_BASE = 8
RWKV_STEP_PAIRS = 8
RWKV_STEP_CHUNKS = 2


def _params(*sem):
    return pltpu.CompilerParams(dimension_semantics=sem, vmem_limit_bytes=VMEM_LIMIT_BYTES)


def _dot(a, b):
    return jnp.dot(a, b, preferred_element_type=F32)


def _dot_nt(a, b):
    return lax.dot_general(a, b, (((1,), (1,)), ((), ())), preferred_element_type=F32)


def _resident(shape):
    nd = len(shape)
    return pl.BlockSpec(shape, lambda *_: (0,) * nd, pipeline_mode=pl.Buffered(1))


def _rows(tm, width):
    return pl.BlockSpec((tm, width), lambda i: (i, 0))


def _post_ln(x, h, g, b):
    y = ALPHA * x + h
    mu = jnp.mean(y, axis=-1, keepdims=True)
    d = y - mu
    var = jnp.mean(d * d, axis=-1, keepdims=True)
    return d * lax.rsqrt(var + LN_EPS) * g + b


def _rms(x, g):
    return x * lax.rsqrt(jnp.mean(x * x, axis=-1, keepdims=True) + RMS_EPS) * g


def _split_bf16(x, parts):
    out = []
    for _ in range(parts - 1):
        hi = x.astype(BF16)
        out.append(hi)
        x = x - hi.astype(F32)
    out.append(x.astype(BF16))
    return out


def _head_sums(x, ones_bd):
    cols = []
    for p in range(x.shape[1] // LANES):
        xs = x[:, p * LANES:(p + 1) * LANES]
        hi, lo = _split_bf16(xs, 2)
        cols.append(_dot(hi, ones_bd) + _dot(lo, ones_bd))
    return jnp.concatenate(cols, axis=1)


def _rope_table_kernel(pos_ref, freq_ref, cos_ref, sin_ref):
    ang = pos_ref[...].astype(F32) * freq_ref[...]
    cos_ref[...] = jnp.cos(ang)
    sin_ref[...] = jnp.sin(ang)


def _rope_tables(positions, tm):
    t = positions.size
    inv_freq = ROPE_THETA ** (-jnp.arange(0, MLA_ROPE, 2, dtype=F32) / MLA_ROPE)
    freq = jnp.zeros((1, SLOT), F32)
    freq = freq.at[0, ROPE_LO:ROPE_LO + ROPE_HALF].set(inv_freq)
    freq = freq.at[0, ROPE_LO + ROPE_HALF:ROPE_LO + MLA_ROPE].set(inv_freq)
    return pl.pallas_call(
        _rope_table_kernel,
        grid=(t // tm,),
        in_specs=[_rows(tm, 1), _resident((1, SLOT))],
        out_specs=[_rows(tm, SLOT), _rows(tm, SLOT)],
        out_shape=[jax.ShapeDtypeStruct((t, SLOT), F32)] * 2,
        compiler_params=_params("parallel"),
        name="rope_tables",
    )(positions.reshape(t, 1), freq)


def _rope_slots(z, cos, sin):
    width = z.shape[1]
    lane = lax.broadcasted_iota(jnp.int32, (1, width), 1) % SLOT
    first = (lane >= ROPE_LO) & (lane < ROPE_LO + ROPE_HALF)
    second = (lane >= ROPE_LO + ROPE_HALF) & (lane < ROPE_LO + MLA_ROPE)
    from_left = pltpu.roll(z, ROPE_HALF, axis=1)
    from_right = pltpu.roll(z, width - ROPE_HALF, axis=1)
    partner = jnp.where(second, from_left, jnp.where(first, -from_right, 0.0))
    return z * cos + partner * sin


def _rope_slots_partnered(z, cos, sin):
    width = z.shape[1]
    return z * cos + pltpu.roll(z, width - MLA_ROPE, axis=1) * sin


def _mla_qkv_kernel(xb_ref, wd_ref, qn_ref, kvn_ref, wq_ref, wkv_ref, cos_ref, sin_ref, q_ref, k_ref, vt_ref, *, scale):
    cos, sin = cos_ref[...], sin_ref[...]
    lat = _dot(xb_ref[...].astype(BF16), wd_ref[...])
    cq = _rms(lat[:, :MLA_Q_LORA], qn_ref[...]).astype(BF16)
    ckv = _rms(lat[:, MLA_Q_LORA:MLA_Q_LORA + MLA_KV_LORA], kvn_ref[...]).astype(BF16)
    kpe = _rope_slots(lat[:, MLA_Q_LORA + MLA_KV_LORA:], cos, sin)
    q = _dot(cq, wq_ref[...])
    q_ref[...] = (_rope_slots_partnered(q, jnp.tile(cos, (1, MLA_HEADS)), jnp.tile(sin, (1, MLA_HEADS)))
                  * scale).astype(BF16)
    kv = _dot(ckv, wkv_ref[...])
    kw = MLA_HEADS * SLOT
    k_ref[...] = (kv[:, :kw] + jnp.tile(kpe, (1, MLA_HEADS))).astype(BF16)
    vt_ref[0] = kv[:, kw:].T.astype(BF16)


def _mla_attn_kernel(q_ref, k_ref, vt_ref, o_ref, *, tq, tk):
    assert tq == tk
    n_full = pl.program_id(2)
    heads = tuple(range(q_ref.shape[1] // SLOT))
    slots = tuple(slice(hh * SLOT, (hh + 1) * SLOT) for hh in heads)

    def own_rows(hh, width):
        sub = lax.broadcasted_iota(jnp.int32, (LANES, width), 0)
        return sub < MLA_V if hh % 2 == 0 else sub >= MLA_V

    def causal(keys, queries):
        return (lax.broadcasted_iota(jnp.int32, (keys, queries), 0)
                <= lax.broadcasted_iota(jnp.int32, (keys, queries), 1))

    def scores(j):
        start = pl.multiple_of(j * tk, tk)
        return tuple(_dot_nt(k_ref[pl.ds(start, tk), slot], q_ref[:, slot]) for slot in slots)

    def accumulate(s, state, vt, hh):
        m, acc = state
        m_new = jnp.maximum(m, jnp.max(s, axis=0, keepdims=True))
        alpha = jnp.exp2(m - m_new)
        p = jnp.exp2(s - m_new).astype(BF16)
        vt = jnp.where(own_rows(hh, vt.shape[1]), vt, jnp.ones(vt.shape, BF16))
        return m_new, alpha * acc + _dot(vt, p)

    def update(j, s_heads, states):
        return tuple(accumulate(s, st, vt_ref[j, hh // 2 * LANES:(hh // 2 + 1) * LANES, :], hh)
                     for hh, s, st in zip(heads, s_heads, states))

    def body(j, states):
        return update(j, scores(j), states)

    def body2(j2, states):
        s_even, s_odd = scores(2 * j2), scores(2 * j2 + 1)
        return update(2 * j2 + 1, s_odd, update(2 * j2, s_even, states))

    init = (jnp.full((1, tq), NEG_INF, F32), jnp.zeros((LANES, tq), F32))
    states = lax.fori_loop(0, n_full // 2, body2, (init,) * len(heads))
    states = lax.fori_loop(n_full - n_full % 2, n_full, body, states)
    s_diag = tuple(jnp.where(causal(tk, tq), s, NEG_INF) for s in scores(n_full))
    accs = [acc for _, acc in update(n_full, s_diag, states)]
    for pp in range(len(heads) // 2):
        acc0, acc1 = accs[2 * pp], accs[2 * pp + 1]
        out_t = jnp.where(own_rows(0, tq), acc0 / acc0[MLA_V:MLA_V + 1, :], acc1 / acc1[0:1, :])
        o_ref[:, pp * LANES:(pp + 1) * LANES] = out_t.T.astype(o_ref.dtype)


def _mla_layer(x, xb, cos, sin, batch, w_down, q_norm, kv_norm, w_uq, w_ukv, w_o, tm):
    t, d = x.shape
    seq = t // batch
    h = MLA_HEADS
    pad = jnp.zeros((d, ROPE_LO), F32)
    w_down_s = jnp.concatenate(
        [w_down[:, :MLA_Q_LORA + MLA_KV_LORA], pad, w_down[:, MLA_Q_LORA + MLA_KV_LORA:],
         jnp.zeros((d, SLOT - ROPE_LO - MLA_ROPE), F32)], axis=1).astype(BF16)
    w_uq3 = w_uq.reshape(MLA_Q_LORA, h, MLA_NOPE + MLA_ROPE)
    w_partner = jnp.concatenate([-w_uq3[:, :, MLA_NOPE + ROPE_HALF:], w_uq3[:, :, MLA_NOPE:MLA_NOPE + ROPE_HALF]], axis=2)
    w_uq_s = jnp.concatenate([w_uq3, w_partner], axis=2).reshape(MLA_Q_LORA, h * SLOT).astype(BF16)
    w_ukv3 = w_ukv.reshape(MLA_KV_LORA, h, MLA_NOPE + MLA_V)
    w_k_s = jnp.pad(w_ukv3[:, :, :MLA_NOPE], ((0, 0), (0, 0), (0, SLOT - MLA_NOPE))).reshape(MLA_KV_LORA, h * SLOT)
    w_v_s = w_ukv3[:, :, MLA_NOPE:].reshape(MLA_KV_LORA, h * MLA_V)
    w_kv_s = jnp.concatenate([w_k_s, w_v_s], axis=1).astype(BF16)

    assert SLOT == MLA_NOPE + 2 * MLA_ROPE
    tq = tk = tm
    nq = seq // tq
    q, k, vt = pl.pallas_call(
        functools.partial(_mla_qkv_kernel, scale=(MLA_NOPE + MLA_ROPE) ** -0.5 * LOG2_E),
        grid=(t // tm,),
        in_specs=[_rows(tm, d), _resident(w_down_s.shape), _resident((1, MLA_Q_LORA)), _resident((1, MLA_KV_LORA)),
                  _resident(w_uq_s.shape), _resident(w_kv_s.shape), _rows(tm, SLOT), _rows(tm, SLOT)],
        out_specs=[_rows(tm, h * SLOT), _rows(tm, h * SLOT), pl.BlockSpec((1, h * MLA_V, tm), lambda i: (i, 0, 0))],
        out_shape=[jax.ShapeDtypeStruct((t, h * SLOT), BF16), jax.ShapeDtypeStruct((t, h * SLOT), BF16),
                   jax.ShapeDtypeStruct((t // tm, h * MLA_V, tm), BF16)],
        compiler_params=_params("parallel"),
        name="mla_qkv",
    )(xb, w_down_s, q_norm.reshape(1, -1), kv_norm.reshape(1, -1), w_uq_s, w_kv_s, cos, sin)

    hps = 2 * MLA_STEP_PAIRS
    o = pl.pallas_call(
        functools.partial(_mla_attn_kernel, tq=tq, tk=tk),
        grid=(batch, h // hps, nq),
        in_specs=[pl.BlockSpec((tq, hps * SLOT), lambda b, p, i: (b * nq + i, p)),
                  pl.BlockSpec((seq, hps * SLOT), lambda b, p, i: (b, p)),
                  pl.BlockSpec((nq, hps * MLA_V, tk), lambda b, p, i: (b, p, 0))],
        out_specs=pl.BlockSpec((tq, hps * MLA_V), lambda b, p, i: (b * nq + i, p)),
        out_shape=jax.ShapeDtypeStruct((t, h * MLA_V), BF16),
        compiler_params=_params("parallel", "parallel", "arbitrary"),
        name="mla_attn",
    )(q, k, vt)

    return o, w_o.astype(BF16)


def _dil_qkv_kernel(x_ref, w_ref, o_ref, *scratch, dil):
    if dil == 1:
        o_ref[0, 0] = _dot(x_ref[...].astype(BF16), w_ref[...]).astype(o_ref.dtype)
        return
    lanes_ref, xperm_ref = scratch
    rows = x_ref.shape[0] // dil

    @pl.when(pl.program_id(1) == 0)
    def _():
        for p in range(x_ref.shape[1] // LANES):
            sl = slice(p * LANES, (p + 1) * LANES)
            lanes_ref[p] = x_ref[:, sl]
            for r in range(dil):
                xperm_ref[r * rows:(r + 1) * rows, sl] = lanes_ref[p, pl.ds(r, rows, stride=dil), :].astype(BF16)

    res = _dot(xperm_ref[...], w_ref[...])
    for r in range(dil):
        o_ref[0, r] = res[r * rows:(r + 1) * rows, :].astype(o_ref.dtype)


def _dil_attn_kernel(q_ref, kp_ref, kc_ref, vp_ref, vc_ref, o_ref, lse_ref, *, n_sub):
    q_ref, kp_ref, kc_ref, vp_ref, vc_ref, o_ref, lse_ref = (
        ref.at[0, 0] for ref in (q_ref, kp_ref, kc_ref, vp_ref, vc_ref, o_ref, lse_ref))
    bk = DIL_BLOCK
    i = lax.broadcasted_iota(jnp.int32, (bk, 2 * bk), 0)
    j = lax.broadcasted_iota(jnp.int32, (bk, 2 * bk), 1)
    dist = i + bk - j
    band = (dist >= 0) & (dist <= bk)
    prev_exists = (pl.program_id(2) > 0) | (j >= bk)
    lse_lane = lax.broadcasted_iota(jnp.int32, (1, LANES), 1)
    lo = lse_lane < DIL_HEAD_DIM
    pairs = [slice(p * LANES, (p + 1) * LANES) for p in range(DIL_HEADS // 2)]
    for sb in range(n_sub):
        rows = slice(sb * bk, (sb + 1) * bk)
        valid = band & prev_exists if sb == 0 else band

        def window(prev_ref, cur_ref, sl, sb=sb):
            if sb == 0:
                return jnp.concatenate([prev_ref[:, sl], cur_ref[:bk, sl]], axis=0)
            return cur_ref[(sb - 1) * bk:(sb + 1) * bk, sl]

        def pair_scores_of(sl):
            q = q_ref[rows, sl]
            k = window(kp_ref, kc_ref, sl)
            zero = jnp.zeros_like(q)
            return [_dot_nt(qh, k) for qh in (jnp.where(lo, q, zero), jnp.where(lo, zero, q))]

        lse_blk = jnp.zeros((bk, LANES), F32)
        scores = {}
        for p, sl in enumerate(pairs):
            if p % DIL_STAGE_PAIRS == 0:
                scores = {pp: pair_scores_of(pairs[pp]) for pp in range(p, min(p + DIL_STAGE_PAIRS, len(pairs)))}
            pair_scores = scores[p]
            v = window(vp_ref, vc_ref, sl)
            res = []
            for s in pair_scores:
                s = jnp.where(valid, s, NEG_INF)
                m = jnp.max(s, axis=-1, keepdims=True)
                e = jnp.exp(s - m)
                l = jnp.sum(e, axis=-1, keepdims=True)
                res.append((_dot(e.astype(BF16), v) / l, m + jnp.log(l)))
            o_ref[rows, sl] = jnp.where(lo, res[0][0], res[1][0]).astype(o_ref.dtype)
            lse_blk = jnp.where(lse_lane // DIL_LSE_LANES == 2 * p, res[0][1],
                                jnp.where(lse_lane // DIL_LSE_LANES == 2 * p + 1, res[1][1], lse_blk))
        lse_ref[rows, :] = lse_blk


def _dil_out_kernel(o1_ref, o2_ref, o3_ref, l1_ref, l2_ref, l3_ref, expand_ref, w_ref, x_ref, g_ref, b_ref,
                    xo_ref, xbo_ref, *bufs):
    bufs = list(bufs)

    def token_order(ref, dil):
        if dil == 1:
            return ref[0, 0].astype(F32)
        n_blocks = ref.shape[3] // LANES
        buf = bufs.pop()
        for p in range(n_blocks):
            for r in range(dil):
                buf[p, pl.ds(r, ref.shape[2], stride=dil), :] = ref[0, r, :, p * LANES:(p + 1) * LANES].astype(F32)
        return jnp.concatenate([buf[p] for p in range(n_blocks)], axis=1)

    dils = [dil for _, dil in DIL_GROUPS]
    l1, l2, l3 = (token_order(ref, dil) for ref, dil in zip((l1_ref, l2_ref, l3_ref), dils))
    m = jnp.maximum(jnp.maximum(l1, l2), l3)
    e1, e2, e3 = jnp.exp(l1 - m), jnp.exp(l2 - m), jnp.exp(l3 - m)
    inv = 1.0 / (e1 + e2 + e3)
    w1, w2, w3 = (_dot((e * inv).astype(BF16), expand_ref[...]) for e in (e1, e2, e3))
    o1, o2, o3 = (token_order(ref, dil) for ref, dil in zip((o1_ref, o2_ref, o3_ref), dils))
    o = w1 * o1 + w2 * o2 + w3 * o3
    y = _post_ln(x_ref[...], _dot(o.astype(BF16), w_ref[...]), g_ref[...], b_ref[...])
    xo_ref[...] = y
    xbo_ref[...] = y.astype(BF16)


def _dil_layer(x, xb, batch, w_qkv, w_o, ln_g, ln_b, tm):
    t, d = x.shape
    seq = t // batch
    ng = len(DIL_GROUPS)
    hd = DIL_HEADS * DIL_HEAD_DIM
    nqkv = ng * 3 * hd
    col_kind = (jnp.arange(nqkv) // hd) % 3
    w_s = (w_qkv * jnp.where(col_kind == 0, DIL_HEAD_DIM ** -0.5, 1.0)[None, :]).astype(BF16)
    tps = seq // tm
    bk = DIL_BLOCK
    outs, lses = [], []
    for gi, (window, dil) in enumerate(DIL_GROUPS):
        assert window // dil == bk and tm % (16 * dil) == 0
        assert seq % (dil * bk * min(DIL_STEP_BLOCKS, seq // (dil * bk))) == 0
        tmq = min(2 * tm, seq)
        tpq = seq // tmq
        qkv = pl.pallas_call(
            functools.partial(_dil_qkv_kernel, dil=dil),
            grid=(t // tmq, 3),
            in_specs=[pl.BlockSpec((tmq, d), lambda i, c: (i, 0)),
                      pl.BlockSpec((d, hd), lambda i, c, gi=gi: (0, gi * 3 + c))],
            out_specs=pl.BlockSpec((1, dil, tmq // dil, hd), lambda i, c, tpq=tpq: (i // tpq, 0, i % tpq, c)),
            out_shape=jax.ShapeDtypeStruct((batch, dil, seq // dil, 3 * hd), BF16),
            scratch_shapes=[] if dil == 1 else [pltpu.VMEM((d // LANES, tmq, LANES), F32), pltpu.VMEM((tmq, d), BF16)],
            compiler_params=_params("parallel", "arbitrary"),
            name=f"dil_qkv_d{dil}",
        )(xb if dil == 1 else x, w_s)

        n_sub = min(DIL_STEP_BLOCKS, seq // (dil * bk))

        def spec(kind, prev, n_sub=n_sub):
            if prev:
                return pl.BlockSpec((1, 1, bk, hd), lambda b, r, n: (b, r, jnp.maximum(n * n_sub - 1, 0), kind))
            return pl.BlockSpec((1, 1, n_sub * bk, hd), lambda b, r, n: (b, r, n, kind))

        out_specs = [pl.BlockSpec((1, 1, n_sub * bk, width), lambda b, r, n: (b, r, n, 0)) for width in (hd, LANES)]
        o, lse = pl.pallas_call(
            functools.partial(_dil_attn_kernel, n_sub=n_sub),
            grid=(batch, dil, seq // (dil * bk * n_sub)),
            in_specs=[spec(0, False), spec(1, True), spec(1, False), spec(2, True), spec(2, False)],
            out_specs=out_specs,
            out_shape=[jax.ShapeDtypeStruct((batch, dil, seq // dil, hd), BF16),
                       jax.ShapeDtypeStruct((batch, dil, seq // dil, LANES), F32)],
            compiler_params=_params("parallel", "parallel", "arbitrary"),
            name=f"dil_attn_d{dil}",
        )(qkv, qkv, qkv, qkv, qkv)
        outs.append(o)
        lses.append(lse)

    def res_specs(width):
        return [pl.BlockSpec((1, dil, tm // dil, width), lambda i: (i // tps, 0, i % tps, 0)) for _, dil in DIL_GROUPS]

    expand = (jnp.arange(LANES)[:, None] == DIL_LSE_LANES * (jnp.arange(hd) // DIL_HEAD_DIM)[None, :]).astype(BF16)
    n_strided = sum(dil > 1 for _, dil in DIL_GROUPS)
    bufs = [pltpu.VMEM((hd // LANES, tm, LANES), F32)] * n_strided + [pltpu.VMEM((1, tm, LANES), F32)] * n_strided
    return pl.pallas_call(
        _dil_out_kernel,
        grid=(t // tm,),
        in_specs=res_specs(hd) + res_specs(LANES) + [_resident((LANES, hd)), _resident((hd, d)), _rows(tm, d),
                                                     _resident((1, d)), _resident((1, d))],
        out_specs=[_rows(tm, d), _rows(tm, d)],
        out_shape=[jax.ShapeDtypeStruct((t, d), F32), jax.ShapeDtypeStruct((t, d), BF16)],
        scratch_shapes=bufs,
        compiler_params=_params("parallel"),
        name="dil_out",
    )(*outs, *lses, expand, w_o.astype(BF16), x, ln_g.reshape(1, d), ln_b.reshape(1, d))


def _rwkv_pre_kernel(x_ref, halo_ref, mu_ref, wrkv_ref, w0_ref, w1_ref, w2_ref, a0_ref, a1_ref, a2_ref,
                     g1_ref, g2_ref, kk_ref, ka_ref, bd_ref,
                     r_o, k_o, v_o, kkn_o, a_o, lw_o, g_o, *, tiles_per_seq):
    first = (pl.program_id(0) % tiles_per_seq) == 0
    x = x_ref[...]
    tm = x.shape[0]
    prev_row = jnp.where(first, 0.0, halo_ref[7:8, :])
    row = lax.broadcasted_iota(jnp.int32, (tm, 1), 0)
    xx = jnp.where(row == 0, prev_row, pltpu.roll(x, 1, axis=0)) - x
    mu = mu_ref[...]

    mixes = [(x + xx * mu[j:j + 1, :]).astype(BF16) for j in range(6)]
    down_w, down_a, down_g = (_dot(mixes[j], w_ref[...]) for j, w_ref in ((3, w1_ref), (4, a1_ref), (5, g1_ref)))
    r = _dot(mixes[0], wrkv_ref[0])
    k = _dot(mixes[1], wrkv_ref[1])
    v = _dot(mixes[2], wrkv_ref[2])
    z = w0_ref[...] + _dot(jnp.tanh(down_w).astype(BF16), w2_ref[...])
    nz = -z
    softplus = jnp.maximum(nz, 0.0) + jnp.log(1.0 + jnp.exp(-jnp.abs(nz)))
    lw = -jnp.exp(-softplus - 0.5)
    a = jax.nn.sigmoid(a0_ref[...] + _dot(down_a.astype(BF16), a2_ref[...]))
    g = _dot(jax.nn.sigmoid(down_g).astype(BF16), g2_ref[...])
    kk = k * kk_ref[...]
    norm = jnp.sqrt(_head_sums(kk * kk, bd_ref[...]))
    r_o[...] = r
    k_o[...] = k * (1.0 + (a - 1.0) * ka_ref[...])
    v_o[...] = v.astype(v_o.dtype)
    kkn_o[...] = kk / jnp.maximum(norm, 1e-12)
    a_o[...] = a
    lw_o[...] = lw
    g_o[...] = g.astype(g_o.dtype)


def _rwkv_chunk_kernel(r_ref, k_ref, v_ref, kk_ref, a_ref, lw_ref, o_ref, s_ref, *, n_pairs, n_chunks):
    c = RWKV_CHUNK

    @pl.when(pl.program_id(2) == 0)
    def _():
        s_ref[...] = jnp.zeros_like(s_ref)

    row = lax.broadcasted_iota(jnp.int32, (c, LANES), 0)
    col = lax.broadcasted_iota(jnp.int32, (c, LANES), 1) % RWKV_HEAD
    blk = {}
    size = RWKV_INV_BASE
    while size < c:
        blk[size] = (row // size) == (col // size)
        size *= 2
    strict = col < row
    incl = col <= row
    eye = jnp.where(row == col, 1.0, 0.0).astype(F32)
    same_head = (lax.broadcasted_iota(jnp.int32, (LANES, LANES), 0) // RWKV_HEAD
                 == lax.broadcasted_iota(jnp.int32, (LANES, LANES), 1) // RWKV_HEAD)
    tri = jnp.where(lax.broadcasted_iota(jnp.int32, (c, c), 1) <= lax.broadcasted_iota(jnp.int32, (c, c), 0),
                    1.0, 0.0).astype(BF16)
    lane_lo = lax.broadcasted_iota(jnp.int32, (1, LANES), 1) < RWKV_HEAD

    def stack(x):
        x = x.astype(BF16)
        zero = jnp.zeros_like(x)
        return jnp.concatenate([jnp.where(lane_lo, x, zero), jnp.where(lane_lo, zero, x)], axis=0)

    def bf(x):
        return x.astype(BF16)

    streams = [(ch, p) for ch in range(n_chunks) for p in range(n_pairs)]

    def load(ref):
        return [ref[ch * c:(ch + 1) * c, p * LANES:(p + 1) * LANES] for ch, p in streams]

    rc, kc, vc, kkc, ac, lwc = (load(ref) for ref in (r_ref, k_ref, v_ref, kk_ref, a_ref, lw_ref))

    cum = []
    for x in lwc:
        cs = _dot(tri, jnp.concatenate(_split_bf16(x, 3), axis=1))
        cum.append(cs[:, :LANES] + cs[:, LANES:2 * LANES] + cs[:, 2 * LANES:])
    p_incl = [jnp.exp(x) for x in cum]
    p_inv = [jnp.exp(-x) for x in cum]
    p_end = [x[c - 1:c, :] for x in p_incl]
    at = [bf(-kk * jnp.exp(cm - lw)) for kk, cm, lw in zip(kkc, cum, lwc)]
    bt = [bf(kk * a * pi) for kk, a, pi in zip(kkc, ac, p_inv)]
    kt = [bf(k * pi) for k, pi in zip(kc, p_inv)]
    rt = [r * pf for r, pf in zip(rc, p_incl)]
    v_st = [stack(x) for x in vc]

    prod = [_dot_nt(jnp.concatenate([a, bf(r)], axis=0), jnp.concatenate([stack(b), stack(k)], axis=0))
            for a, r, b, k in zip(at, rt, bt, kt)]
    l_ab = [jnp.where(strict, x[:c, :LANES], 0.0) for x in prod]
    l_ak = [bf(jnp.where(strict, x[:c, LANES:], 0.0)) for x in prod]
    m_rb = [bf(jnp.where(incl, x[c:, :LANES], 0.0)) for x in prod]
    m_rk = [bf(jnp.where(incl, x[c:, LANES:], 0.0)) for x in prod]

    dg = [bf(jnp.where(blk[RWKV_INV_BASE], x, 0.0)) for x in l_ab]
    d2 = [bf(_dot(x, stack(x))) for x in dg]
    d4 = [_dot(x, stack(x)) for x in d2]
    tinv = [eye + x.astype(F32) for x in dg]
    tinv = [t + _dot(bf(t), stack(y)) for t, y in zip(tinv, d2)]
    tinv = [t + _dot(bf(t), stack(y)) for t, y in zip(tinv, d4)]
    size = RWKV_INV_BASE
    while size < c:
        joins = jnp.logical_not(blk[size]) if 2 * size == c else blk[2 * size] & jnp.logical_not(blk[size])
        off = [jnp.where(joins, x, 0.0) for x in l_ab]
        te = [_dot(bf(t), stack(e)) for t, e in zip(tinv, off)]
        tinv = [t + _dot(bf(x), stack(t)) for t, x in zip(tinv, te)]
        size *= 2

    lakv = [_dot(x, v) for x, v in zip(l_ak, v_st)]
    gu = [_dot(bf(t), jnp.concatenate([stack(a), stack(y)], axis=1)) for t, a, y in zip(tinv, at, lakv)]
    g_nat = [x[:, :LANES] for x in gu]
    uv_nat = [x[:, LANES:] for x in gu]
    mg = [_dot(m, jnp.concatenate([stack(g), stack(u)], axis=1)) for m, g, u in zip(m_rb, g_nat, uv_nat)]
    rg = [bf(r + x[:, :LANES]) for r, x in zip(rt, mg)]
    o_v = [x[:, LANES:] + _dot(m, v) for x, m, v in zip(mg, m_rk, v_st)]
    q_mat = [bf(jnp.where(same_head, _dot(bf(g.T), b), 0.0) * pe) for g, b, pe in zip(g_nat, bt, p_end)]
    n_mat = [jnp.where(same_head, _dot(bf(jnp.concatenate([u, v.astype(F32)], axis=0).T),
                                       jnp.concatenate([b, k], axis=0)), 0.0) * pe
             for u, v, b, k, pe in zip(uv_nat, vc, bt, kt, p_end)]

    state = [s_ref[p] for p in range(n_pairs)]
    for i, (ch, p) in enumerate(streams):
        sb = bf(state[p])
        o_ref[ch * c:(ch + 1) * c, p * LANES:(p + 1) * LANES] = _dot_nt(rg[i], sb) + o_v[i]
        state[p] = state[p] * p_end[i] + _dot(sb, q_mat[i]) + n_mat[i]
    for p in range(n_pairs):
        s_ref[p] = state[p]


def _rwkv_post_kernel(o_ref, r_ref, k_ref, v_ref, g_ref, x_ref, lnw_ref, lnb_ref, rk_ref, bd_ref, w_ref,
                      gl_ref, bl_ref, xo_ref, xbo_ref):
    bd = bd_ref[...]
    o = o_ref[...]
    mean = _head_sums(o, bd) * (1.0 / RWKV_HEAD)
    dlt = o - mean
    var = _head_sums(dlt * dlt, bd) * (1.0 / RWKV_HEAD)
    y = dlt * lax.rsqrt(var + RWKV_GN_EPS) * lnw_ref[...] + lnb_ref[...]
    bonus = _head_sums(r_ref[...] * k_ref[...] * rk_ref[...], bd) * v_ref[...]
    out = ((y + bonus) * g_ref[...]).astype(BF16)
    res = _post_ln(x_ref[...], _dot(out, w_ref[...]), gl_ref[...], bl_ref[...])
    xo_ref[...] = res
    xbo_ref[...] = res.astype(BF16)


def _rwkv_layer(x, batch, mu, w_rkv, w0, w1, w2, a0, a1, a2, g1, g2, k_k, k_a, r_k, ln_w, ln_b, w_o,
                ln_g_post, ln_b_post, tm):
    t, d = x.shape
    seq = t // batch
    tmr = min(256, seq)
    vec = lambda p: p.reshape(1, d)
    idx = jnp.arange(LANES) // RWKV_HEAD
    ones_bd = (idx[:, None] == idx[None, :]).astype(BF16)
    tok = [jax.ShapeDtypeStruct((t, d), dt) for dt in (F32, F32, BF16, F32, F32, F32, BF16)]
    halo = pl.BlockSpec((8, d), lambda i: (jnp.maximum(i * (tmr // 8) - 1, 0), 0))
    r, k, v, kkn, a, lw, g = pl.pallas_call(
        functools.partial(_rwkv_pre_kernel, tiles_per_seq=seq // tmr),
        grid=(t // tmr,),
        in_specs=[_rows(tmr, d), halo, _resident((6, d)), _resident((3, d, d)),
                  _resident((1, d)), _resident(w1.shape), _resident(w2.shape),
                  _resident((1, d)), _resident(a1.shape), _resident(a2.shape),
                  _resident(g1.shape), _resident(g2.shape), _resident((1, d)), _resident((1, d)),
                  _resident((LANES, LANES))],
        out_specs=[_rows(tmr, d)] * 7,
        out_shape=tok,
        compiler_params=_params("parallel"),
        name="rwkv_pre",
    )(x, x, mu, w_rkv.astype(BF16), vec(w0), w1.astype(BF16), w2.astype(BF16), vec(a0), a1.astype(BF16),
      a2.astype(BF16), g1.astype(BF16), g2.astype(BF16), vec(k_k), vec(k_a), ones_bd)

    tc = min(RWKV_STEP_CHUNKS * RWKV_CHUNK, seq)
    nt = seq // tc
    width = RWKV_STEP_PAIRS * LANES
    pair = pl.BlockSpec((tc, width), lambda b, p, i: (b * nt + i, p))
    o = pl.pallas_call(
        functools.partial(_rwkv_chunk_kernel, n_pairs=RWKV_STEP_PAIRS, n_chunks=tc // RWKV_CHUNK),
        grid=(batch, d // width, nt),
        in_specs=[pair] * 6,
        out_specs=pair,
        out_shape=jax.ShapeDtypeStruct((t, d), F32),
        scratch_shapes=[pltpu.VMEM((RWKV_STEP_PAIRS, LANES, LANES), F32)],
        compiler_params=_params("parallel", "parallel", "arbitrary"),
        name="rwkv_chunk",
    )(r, k, v, kkn, a, lw)

    return pl.pallas_call(
        _rwkv_post_kernel,
        grid=(t // tm,),
        in_specs=[_rows(tm, d)] * 6 + [_resident((1, d))] * 3 + [_resident((LANES, LANES)), _resident((d, d)),
                                                                _resident((1, d)), _resident((1, d))],
        out_specs=[_rows(tm, d), _rows(tm, d)],
        out_shape=[jax.ShapeDtypeStruct((t, d), F32), jax.ShapeDtypeStruct((t, d), BF16)],
        compiler_params=_params("parallel"),
        name="rwkv_post",
    )(o, r, k, v, g, x, vec(ln_w), vec(ln_b), vec(r_k), ones_bd, w_o.astype(BF16), vec(ln_g_post), vec(ln_b_post))


def _ffn_core(x1, xe, wa_ref, wb_ref, cw_ref, cb_ref, wo_ref, g_ref, b_ref, xo_ref, xbo_ref, acc_ref,
              *, n_chunks, halo_rows):
    tm = x1.shape[0]
    xb = xe[halo_rows:, :]
    acc_ref[...] = jnp.zeros_like(acc_ref)

    def up(c):
        return _dot(xe, wa_ref[c]), _dot(xb, wb_ref[c])

    def down(c, ha, hb):
        cw = cw_ref[c]
        conv = (cw[0:1, :] * ha[halo_rows - 2:halo_rows - 2 + tm, :]
                + cw[1:2, :] * ha[halo_rows - 1:halo_rows - 1 + tm, :]
                + cw[2:3, :] * ha[halo_rows:, :] + cb_ref[c])
        act = conv * jax.nn.sigmoid(conv) * hb
        acc_ref[...] += _dot(act.astype(BF16), wo_ref[c])

    ahead = 2
    pending = {c: up(c) for c in range(min(ahead, n_chunks))}
    for c in range(n_chunks):
        if c + ahead < n_chunks:
            pending[c + ahead] = up(c + ahead)
        down(c, *pending.pop(c))
    y = _post_ln(x1, acc_ref[...], g_ref[...], b_ref[...])
    xo_ref[...] = y
    xbo_ref[...] = y.astype(BF16)


def _ffn_kernel(x_ref, xb_ref, halo_ref, *rest, tiles_per_seq, n_chunks, halo_rows):
    first = (pl.program_id(0) % tiles_per_seq) == 0
    halo = halo_ref[...]
    halo = jnp.where(first, jnp.zeros_like(halo), halo)
    xe = jnp.concatenate([halo, xb_ref[...]], axis=0)
    _ffn_core(x_ref[...], xe, *rest, n_chunks=n_chunks, halo_rows=halo_rows)


def _proj_ffn_kernel(a_ref, ahalo_ref, x_ref, xhalo_ref, wp_ref, g1_ref, b1_ref, *rest,
                     tiles_per_seq, n_chunks, halo_rows):
    first = (pl.program_id(0) % tiles_per_seq) == 0
    a_ext = jnp.concatenate([ahalo_ref[...], a_ref[...]], axis=0)
    x_ext = jnp.concatenate([xhalo_ref[...], x_ref[...]], axis=0)
    x1_ext = _post_ln(x_ext, _dot(a_ext, wp_ref[...]), g1_ref[...], b1_ref[...])
    row = lax.broadcasted_iota(jnp.int32, (x1_ext.shape[0], 1), 0)
    x1_ext = jnp.where(first & (row < halo_rows), 0.0, x1_ext)
    _ffn_core(x1_ext[halo_rows:, :], x1_ext.astype(BF16), *rest, n_chunks=n_chunks, halo_rows=halo_rows)


def _ffn_layer(x, xb, batch, w_in, conv_w, conv_b, w_out, ln_g, ln_b, tm, pre=None):
    t, d = x.shape
    seq = t // batch
    nch = D_FF_PAD // FF_CHUNK
    padc = D_FF_PAD - D_FF
    chunks = lambda w: jnp.pad(w, ((0, 0), (0, padc))).reshape(w.shape[0], nch, FF_CHUNK).transpose(1, 0, 2)
    wa = chunks(w_in[:, :D_FF]).astype(BF16)
    wb = chunks(w_in[:, D_FF:]).astype(BF16)
    cw = chunks(conv_w)
    cb = chunks(conv_b.reshape(1, D_FF))
    wo = jnp.pad(w_out, ((0, padc), (0, 0))).reshape(nch, FF_CHUNK, d).astype(BF16)
    halo_rows = 16
    halo = lambda width: pl.BlockSpec((halo_rows, width), lambda i: (jnp.maximum(i * (tm // halo_rows) - 1, 0), 0))
    ffn_specs = [_resident(wa.shape), _resident(wb.shape), _resident(cw.shape), _resident(cb.shape),
                 _resident(wo.shape), _resident((1, d)), _resident((1, d))]
    ffn_args = (wa, wb, cw, cb, wo, ln_g.reshape(1, d), ln_b.reshape(1, d))
    static = dict(tiles_per_seq=seq // tm, n_chunks=nch, halo_rows=halo_rows)
    if pre is None:
        body = functools.partial(_ffn_kernel, **static)
        in_specs = [_rows(tm, d), _rows(tm, d), halo(d)] + ffn_specs
        args = (x, xb, xb) + ffn_args
    else:
        a, w, g1, b1 = pre
        body = functools.partial(_proj_ffn_kernel, **static)
        in_specs = [_rows(tm, a.shape[1]), halo(a.shape[1]), _rows(tm, d), halo(d), _resident(w.shape),
                    _resident((1, d)), _resident((1, d))] + ffn_specs
        args = (a, a, x, x, w, g1.reshape(1, d), b1.reshape(1, d)) + ffn_args
    return pl.pallas_call(
        body,
        grid=(t // tm,),
        in_specs=in_specs,
        out_specs=[_rows(tm, d), _rows(tm, d)],
        out_shape=[jax.ShapeDtypeStruct((t, d), F32), jax.ShapeDtypeStruct((t, d), BF16)],
        scratch_shapes=[pltpu.VMEM((tm, d), F32)],
        compiler_params=_params("parallel"),
        name="conv_ffn" if pre is None else "proj_conv_ffn",
    )(*args)


def kernel(x, positions, ln_g, ln_b, mla_w_down, mla_q_norm, mla_kv_norm, mla_w_uq, mla_w_ukv, mla_w_o,
           dil_w_qkv, dil_w_o, rwkv_mu, rwkv_w_rkv, rwkv_w0, rwkv_w1, rwkv_w2, rwkv_a0, rwkv_a1, rwkv_a2,
           rwkv_g1, rwkv_g2, rwkv_k_k, rwkv_k_a, rwkv_r_k, rwkv_ln_w, rwkv_ln_b, rwkv_w_o,
           ffn_w_in, ffn_conv_w, ffn_conv_b, ffn_w_out):
    batch, seq, d = x.shape
    t = batch * seq
    tm = min(512, seq)
    xf = x.reshape(t, d)
    xb = xf
    cos, sin = _rope_tables(positions, tm)
    ia = ib = ic = 0
    for i in range(DEPTH):
        kind = i % 3
        pre = None
        if kind == 0:
            pre = _mla_layer(xf, xb, cos, sin, batch, mla_w_down[ia], mla_q_norm[ia], mla_kv_norm[ia],
                             mla_w_uq[ia], mla_w_ukv[ia], mla_w_o[ia], tm) + (ln_g[i, 0], ln_b[i, 0])
            ia += 1
        elif kind == 1:
            xf, xb = _dil_layer(xf, xb, batch, dil_w_qkv[ib], dil_w_o[ib], ln_g[i, 0], ln_b[i, 0], tm)
            ib += 1
        else:
            xf, xb = _rwkv_layer(xf, batch, rwkv_mu[ic], rwkv_w_rkv[ic], rwkv_w0[ic], rwkv_w1[ic], rwkv_w2[ic],
                                 rwkv_a0[ic], rwkv_a1[ic], rwkv_a2[ic], rwkv_g1[ic], rwkv_g2[ic],
                                 rwkv_k_k[ic], rwkv_k_a[ic], rwkv_r_k[ic], rwkv_ln_w[ic], rwkv_ln_b[ic],
                                 rwkv_w_o[ic], ln_g[i, 0], ln_b[i, 0], tm)
            ic += 1
        xf, xb = _ffn_layer(xf, xb, batch, ffn_w_in[i], ffn_conv_w[i], ffn_conv_b[i], ffn_w_out[i],
                            ln_g[i, 1], ln_b[i, 1], tm, pre=pre)
    return xf.reshape(batch, seq, d)
```

```python
import functools

import jax
import jax.numpy as jnp
from jax import lax
from jax.experimental import pallas as pl
from jax.experimental.pallas import tpu as pltpu

F32 = jnp.float32
BF16 = jnp.bfloat16

D_MODEL = 1024
DEPTH = 4
MLA_HEADS = 16
MLA_Q_LORA = 384
MLA_KV_LORA = 256
MLA_NOPE = 64
MLA_ROPE = 32
MLA_V = 64
ROPE_THETA = 10000.0
DIL_GROUPS = ((128, 1), (512, 4), (2048, 16))
DIL_HEADS = 16
DIL_HEAD_DIM = 64
DIL_BLOCK = 128
DIL_STEP_BLOCKS = 4
DIL_STAGE_PAIRS = 4
DIL_LSE_LANES = 128 // DIL_HEADS
RWKV_HEAD = 64
RWKV_GN_EPS = 64e-5
D_FF = 2752
ALPHA = (2 * DEPTH) ** 0.25
LN_EPS = 1e-5
RMS_EPS = 1e-6
NEG_INF = -1e30
LOG2_E = 1.4426950408889634

LANES = 128
VMEM_LIMIT_BYTES = 56 * 2**20

SLOT = LANES
ROPE_LO = MLA_NOPE
ROPE_HALF = MLA_ROPE // 2
MLA_STEP_PAIRS = 4
FF_CHUNK = 256
D_FF_PAD = -(-D_FF // FF_CHUNK) * FF_CHUNK
RWKV_CHUNK = 64
RWKV_INV_BASE = 8
RWKV_STEP_PAIRS = 8
RWKV_STEP_CHUNKS = 2


def _params(*sem):
    return pltpu.CompilerParams(dimension_semantics=sem, vmem_limit_bytes=VMEM_LIMIT_BYTES)


def _dot(a, b):
    return jnp.dot(a, b, preferred_element_type=F32)


def _dot_nt(a, b):
    return lax.dot_general(a, b, (((1,), (1,)), ((), ())), preferred_element_type=F32)


def _resident(shape):
    nd = len(shape)
    return pl.BlockSpec(shape, lambda *_: (0,) * nd, pipeline_mode=pl.Buffered(1))


def _rows(tm, width):
    return pl.BlockSpec((tm, width), lambda i: (i, 0))


def _post_ln(x, h, g, b):
    y = ALPHA * x + h
    mu = jnp.mean(y, axis=-1, keepdims=True)
    d = y - mu
    var = jnp.mean(d * d, axis=-1, keepdims=True)
    return d * lax.rsqrt(var + LN_EPS) * g + b


def _rms(x, g):
    return x * lax.rsqrt(jnp.mean(x * x, axis=-1, keepdims=True) + RMS_EPS) * g


def _split_bf16(x, parts):
    out = []
    for _ in range(parts - 1):
        hi = x.astype(BF16)
        out.append(hi)
        x = x - hi.astype(F32)
    out.append(x.astype(BF16))
    return out


def _head_sums(x, ones_bd):
    cols = []
    for p in range(x.shape[1] // LANES):
        xs = x[:, p * LANES:(p + 1) * LANES]
        hi, lo = _split_bf16(xs, 2)
        cols.append(_dot(hi, ones_bd) + _dot(lo, ones_bd))
    return jnp.concatenate(cols, axis=1)


def _rope_table_kernel(pos_ref, freq_ref, cos_ref, sin_ref):
    ang = pos_ref[...].astype(F32) * freq_ref[...]
    cos_ref[...] = jnp.cos(ang)
    sin_ref[...] = jnp.sin(ang)


def _rope_tables(positions, tm):
    t = positions.size
    inv_freq = ROPE_THETA ** (-jnp.arange(0, MLA_ROPE, 2, dtype=F32) / MLA_ROPE)
    freq = jnp.zeros((1, SLOT), F32)
    freq = freq.at[0, ROPE_LO:ROPE_LO + ROPE_HALF].set(inv_freq)
    freq = freq.at[0, ROPE_LO + ROPE_HALF:ROPE_LO + MLA_ROPE].set(inv_freq)
    return pl.pallas_call(
        _rope_table_kernel,
        grid=(t // tm,),
        in_specs=[_rows(tm, 1), _resident((1, SLOT))],
        out_specs=[_rows(tm, SLOT), _rows(tm, SLOT)],
        out_shape=[jax.ShapeDtypeStruct((t, SLOT), F32)] * 2,
        compiler_params=_params("parallel"),
        name="rope_tables",
    )(positions.reshape(t, 1), freq)


def _rope_slots(z, cos, sin):
    width = z.shape[1]
    lane = lax.broadcasted_iota(jnp.int32, (1, width), 1) % SLOT
    first = (lane >= ROPE_LO) & (lane < ROPE_LO + ROPE_HALF)
    second = (lane >= ROPE_LO + ROPE_HALF) & (lane < ROPE_LO + MLA_ROPE)
    from_left = pltpu.roll(z, ROPE_HALF, axis=1)
    from_right = pltpu.roll(z, width - ROPE_HALF, axis=1)
    partner = jnp.where(second, from_left, jnp.where(first, -from_right, 0.0))
    return z * cos + partner * sin


def _rope_slots_partnered(z, cos, sin):
    width = z.shape[1]
    return z * cos + pltpu.roll(z, width - MLA_ROPE, axis=1) * sin


def _mla_qkv_kernel(xb_ref, wd_ref, qn_ref, kvn_ref, wq_ref, wkv_ref, cos_ref, sin_ref, q_ref, k_ref, vt_ref, *, scale):
    cos, sin = cos_ref[...], sin_ref[...]
    lat = _dot(xb_ref[...].astype(BF16), wd_ref[...])
    cq = _rms(lat[:, :MLA_Q_LORA], qn_ref[...]).astype(BF16)
    ckv = _rms(lat[:, MLA_Q_LORA:MLA_Q_LORA + MLA_KV_LORA], kvn_ref[...]).astype(BF16)
    kpe = _rope_slots(lat[:, MLA_Q_LORA + MLA_KV_LORA:], cos, sin)
    q = _dot(cq, wq_ref[...])
    q_ref[...] = (_rope_slots_partnered(q, jnp.tile(cos, (1, MLA_HEADS)), jnp.tile(sin, (1, MLA_HEADS)))
                  * scale).astype(BF16)
    kv = _dot(ckv, wkv_ref[...])
    kw = MLA_HEADS * SLOT
    k_ref[...] = (kv[:, :kw] + jnp.tile(kpe, (1, MLA_HEADS))).astype(BF16)
    vt_ref[0] = kv[:, kw:].T.astype(BF16)


def _mla_attn_kernel(q_ref, k_ref, vt_ref, o_ref, *, tq, tk):
    assert tq == tk
    n_full = pl.program_id(2)
    heads = tuple(range(q_ref.shape[1] // SLOT))
    slots = tuple(slice(hh * SLOT, (hh + 1) * SLOT) for hh in heads)

    def own_rows(hh, width):
        sub = lax.broadcasted_iota(jnp.int32, (LANES, width), 0)
        return sub < MLA_V if hh % 2 == 0 else sub >= MLA_V

    def causal(keys, queries):
        return (lax.broadcasted_iota(jnp.int32, (keys, queries), 0)
                <= lax.broadcasted_iota(jnp.int32, (keys, queries), 1))

    def scores(j):
        start = pl.multiple_of(j * tk, tk)
        return tuple(_dot_nt(k_ref[pl.ds(start, tk), slot], q_ref[:, slot]) for slot in slots)

    def accumulate(s, state, vt, hh):
        m, acc = state
        m_new = jnp.maximum(m, jnp.max(s, axis=0, keepdims=True))
        alpha = jnp.exp2(m - m_new)
        p = jnp.exp2(s - m_new).astype(BF16)
        vt = jnp.where(own_rows(hh, vt.shape[1]), vt, jnp.ones(vt.shape, BF16))
        return m_new, alpha * acc + _dot(vt, p)

    def update(j, s_heads, states):
        return tuple(accumulate(s, st, vt_ref[j, hh // 2 * LANES:(hh // 2 + 1) * LANES, :], hh)
                     for hh, s, st in zip(heads, s_heads, states))

    def body(j, states):
        return update(j, scores(j), states)

    def body2(j2, states):
        s_even, s_odd = scores(2 * j2), scores(2 * j2 + 1)
        return update(2 * j2 + 1, s_odd, update(2 * j2, s_even, states))

    init = (jnp.full((1, tq), NEG_INF, F32), jnp.zeros((LANES, tq), F32))
    states = lax.fori_loop(0, n_full // 2, body2, (init,) * len(heads))
    states = lax.fori_loop(n_full - n_full % 2, n_full, body, states)
    s_diag = tuple(jnp.where(causal(tk, tq), s, NEG_INF) for s in scores(n_full))
    accs = [acc for _, acc in update(n_full, s_diag, states)]
    for pp in range(len(heads) // 2):
        acc0, acc1 = accs[2 * pp], accs[2 * pp + 1]
        out_t = jnp.where(own_rows(0, tq), acc0 / acc0[MLA_V:MLA_V + 1, :], acc1 / acc1[0:1, :])
        o_ref[:, pp * LANES:(pp + 1) * LANES] = out_t.T.astype(o_ref.dtype)


def _mla_layer(x, xb, cos, sin, batch, w_down, q_norm, kv_norm, w_uq, w_ukv, w_o, tm):
    t, d = x.shape
    seq = t // batch
    h = MLA_HEADS
    pad = jnp.zeros((d, ROPE_LO), F32)
    w_down_s = jnp.concatenate(
        [w_down[:, :MLA_Q_LORA + MLA_KV_LORA], pad, w_down[:, MLA_Q_LORA + MLA_KV_LORA:],
         jnp.zeros((d, SLOT - ROPE_LO - MLA_ROPE), F32)], axis=1).astype(BF16)
    w_uq3 = w_uq.reshape(MLA_Q_LORA, h, MLA_NOPE + MLA_ROPE)
    w_partner = jnp.concatenate([-w_uq3[:, :, MLA_NOPE + ROPE_HALF:], w_uq3[:, :, MLA_NOPE:MLA_NOPE + ROPE_HALF]], axis=2)
    w_uq_s = jnp.concatenate([w_uq3, w_partner], axis=2).reshape(MLA_Q_LORA, h * SLOT).astype(BF16)
    w_ukv3 = w_ukv.reshape(MLA_KV_LORA, h, MLA_NOPE + MLA_V)
    w_k_s = jnp.pad(w_ukv3[:, :, :MLA_NOPE], ((0, 0), (0, 0), (0, SLOT - MLA_NOPE))).reshape(MLA_KV_LORA, h * SLOT)
    w_v_s = w_ukv3[:, :, MLA_NOPE:].reshape(MLA_KV_LORA, h * MLA_V)
    w_kv_s = jnp.concatenate([w_k_s, w_v_s], axis=1).astype(BF16)

    assert SLOT == MLA_NOPE + 2 * MLA_ROPE
    tq = tk = tm
    nq = seq // tq
    q, k, vt = pl.pallas_call(
        functools.partial(_mla_qkv_kernel, scale=(MLA_NOPE + MLA_ROPE) ** -0.5 * LOG2_E),
        grid=(t // tm,),
        in_specs=[_rows(tm, d), _resident(w_down_s.shape), _resident((1, MLA_Q_LORA)), _resident((1, MLA_KV_LORA)),
                  _resident(w_uq_s.shape), _resident(w_kv_s.shape), _rows(tm, SLOT), _rows(tm, SLOT)],
        out_specs=[_rows(tm, h * SLOT), _rows(tm, h * SLOT), pl.BlockSpec((1, h * MLA_V, tm), lambda i: (i, 0, 0))],
        out_shape=[jax.ShapeDtypeStruct((t, h * SLOT), BF16), jax.ShapeDtypeStruct((t, h * SLOT), BF16),
                   jax.ShapeDtypeStruct((t // tm, h * MLA_V, tm), BF16)],
        compiler_params=_params("parallel"),
        name="mla_qkv",
    )(xb, w_down_s, q_norm.reshape(1, -1), kv_norm.reshape(1, -1), w_uq_s, w_kv_s, cos, sin)

    hps = 2 * MLA_STEP_PAIRS
    o = pl.pallas_call(
        functools.partial(_mla_attn_kernel, tq=tq, tk=tk),
        grid=(batch, h // hps, nq),
        in_specs=[pl.BlockSpec((tq, hps * SLOT), lambda b, p, i: (b * nq + i, p)),
                  pl.BlockSpec((seq, hps * SLOT), lambda b, p, i: (b, p)),
                  pl.BlockSpec((nq, hps * MLA_V, tk), lambda b, p, i: (b, p, 0))],
        out_specs=pl.BlockSpec((tq, hps * MLA_V), lambda b, p, i: (b * nq + i, p)),
        out_shape=jax.ShapeDtypeStruct((t, h * MLA_V), BF16),
        compiler_params=_params("parallel", "parallel", "arbitrary"),
        name="mla_attn",
    )(q, k, vt)

    return o, w_o.astype(BF16)


def _dil_qkv_kernel(x_ref, w_ref, o_ref, *scratch, dil):
    if dil == 1:
        o_ref[0, 0] = _dot(x_ref[...].astype(BF16), w_ref[...]).astype(o_ref.dtype)
        return
    lanes_ref, xperm_ref = scratch
    rows = x_ref.shape[0] // dil

    @pl.when(pl.program_id(1) == 0)
    def _():
        for p in range(x_ref.shape[1] // LANES):
            sl = slice(p * LANES, (p + 1) * LANES)
            lanes_ref[p] = x_ref[:, sl]
            for r in range(dil):
                xperm_ref[r * rows:(r + 1) * rows, sl] = lanes_ref[p, pl.ds(r, rows, stride=dil), :].astype(BF16)

    res = _dot(xperm_ref[...], w_ref[...])
    for r in range(dil):
        o_ref[0, r] = res[r * rows:(r + 1) * rows, :].astype(o_ref.dtype)


def _dil_attn_kernel(q_ref, kp_ref, kc_ref, vp_ref, vc_ref, o_ref, lse_ref, *, n_sub):
    q_ref, kp_ref, kc_ref, vp_ref, vc_ref, o_ref, lse_ref = (
        ref.at[0, 0] for ref in (q_ref, kp_ref, kc_ref, vp_ref, vc_ref, o_ref, lse_ref))
    bk = DIL_BLOCK
    i = lax.broadcasted_iota(jnp.int32, (bk, 2 * bk), 0)
    j = lax.broadcasted_iota(jnp.int32, (bk, 2 * bk), 1)
    dist = i + bk - j
    band = (dist >= 0) & (dist <= bk)
    prev_exists = (pl.program_id(2) > 0) | (j >= bk)
    lse_lane = lax.broadcasted_iota(jnp.int32, (1, LANES), 1)
    lo = lse_lane < DIL_HEAD_DIM
    pairs = [slice(p * LANES, (p + 1) * LANES) for p in range(DIL_HEADS // 2)]
    for sb in range(n_sub):
        rows = slice(sb * bk, (sb + 1) * bk)
        valid = band & prev_exists if sb == 0 else band

        def window(prev_ref, cur_ref, sl, sb=sb):
            if sb == 0:
                return jnp.concatenate([prev_ref[:, sl], cur_ref[:bk, sl]], axis=0)
            return cur_ref[(sb - 1) * bk:(sb + 1) * bk, sl]

        def pair_scores_of(sl):
            q = q_ref[rows, sl]
            k = window(kp_ref, kc_ref, sl)
            zero = jnp.zeros_like(q)
            return [_dot_nt(qh, k) for qh in (jnp.where(lo, q, zero), jnp.where(lo, zero, q))]

        lse_blk = jnp.zeros((bk, LANES), F32)
        scores = {}
        for p, sl in enumerate(pairs):
            if p % DIL_STAGE_PAIRS == 0:
                scores = {pp: pair_scores_of(pairs[pp]) for pp in range(p, min(p + DIL_STAGE_PAIRS, len(pairs)))}
            pair_scores = scores[p]
            v = window(vp_ref, vc_ref, sl)
            res = []
            for s in pair_scores:
                s = jnp.where(valid, s, NEG_INF)
                m = jnp.max(s, axis=-1, keepdims=True)
                e = jnp.exp(s - m)
                l = jnp.sum(e, axis=-1, keepdims=True)
                res.append((_dot(e.astype(BF16), v) / l, m + jnp.log(l)))
            o_ref[rows, sl] = jnp.where(lo, res[0][0], res[1][0]).astype(o_ref.dtype)
            lse_blk = jnp.where(lse_lane // DIL_LSE_LANES == 2 * p, res[0][1],
                                jnp.where(lse_lane // DIL_LSE_LANES == 2 * p + 1, res[1][1], lse_blk))
        lse_ref[rows, :] = lse_blk


def _dil_out_kernel(o1_ref, o2_ref, o3_ref, l1_ref, l2_ref, l3_ref, expand_ref, w_ref, x_ref, g_ref, b_ref,
                    xo_ref, xbo_ref, *bufs):
    bufs = list(bufs)

    def token_order(ref, dil):
        if dil == 1:
            return ref[0, 0].astype(F32)
        n_blocks = ref.shape[3] // LANES
        buf = bufs.pop()
        for p in range(n_blocks):
            for r in range(dil):
                buf[p, pl.ds(r, ref.shape[2], stride=dil), :] = ref[0, r, :, p * LANES:(p + 1) * LANES].astype(F32)
        return jnp.concatenate([buf[p] for p in range(n_blocks)], axis=1)

    dils = [dil for _, dil in DIL_GROUPS]
    l1, l2, l3 = (token_order(ref, dil) for ref, dil in zip((l1_ref, l2_ref, l3_ref), dils))
    m = jnp.maximum(jnp.maximum(l1, l2), l3)
    e1, e2, e3 = jnp.exp(l1 - m), jnp.exp(l2 - m), jnp.exp(l3 - m)
    inv = 1.0 / (e1 + e2 + e3)
    w1, w2, w3 = (_dot((e * inv).astype(BF16), expand_ref[...]) for e in (e1, e2, e3))
    o1, o2, o3 = (token_order(ref, dil) for ref, dil in zip((o1_ref, o2_ref, o3_ref), dils))
    o = w1 * o1 + w2 * o2 + w3 * o3
    y = _post_ln(x_ref[...], _dot(o.astype(BF16), w_ref[...]), g_ref[...], b_ref[...])
    xo_ref[...] = y
    xbo_ref[...] = y.astype(BF16)


def _dil_layer(x, xb, batch, w_qkv, w_o, ln_g, ln_b, tm):
    t, d = x.shape
    seq = t // batch
    ng = len(DIL_GROUPS)
    hd = DIL_HEADS * DIL_HEAD_DIM
    nqkv = ng * 3 * hd
    col_kind = (jnp.arange(nqkv) // hd) % 3
    w_s = (w_qkv * jnp.where(col_kind == 0, DIL_HEAD_DIM ** -0.5, 1.0)[None, :]).astype(BF16)
    tps = seq // tm
    bk = DIL_BLOCK
    outs, lses = [], []
    for gi, (window, dil) in enumerate(DIL_GROUPS):
        assert window // dil == bk and tm % (16 * dil) == 0
        assert seq % (dil * bk * min(DIL_STEP_BLOCKS, seq // (dil * bk))) == 0
        tmq = min(2 * tm, seq)
        tpq = seq // tmq
        qkv = pl.pallas_call(
            functools.partial(_dil_qkv_kernel, dil=dil),
            grid=(t // tmq, 3),
            in_specs=[pl.BlockSpec((tmq, d), lambda i, c: (i, 0)),
                      pl.BlockSpec((d, hd), lambda i, c, gi=gi: (0, gi * 3 + c))],
            out_specs=pl.BlockSpec((1, dil, tmq // dil, hd), lambda i, c, tpq=tpq: (i // tpq, 0, i % tpq, c)),
            out_shape=jax.ShapeDtypeStruct((batch, dil, seq // dil, 3 * hd), BF16),
            scratch_shapes=[] if dil == 1 else [pltpu.VMEM((d // LANES, tmq, LANES), F32), pltpu.VMEM((tmq, d), BF16)],
            compiler_params=_params("parallel", "arbitrary"),
            name=f"dil_qkv_d{dil}",
        )(xb if dil == 1 else x, w_s)

        n_sub = min(DIL_STEP_BLOCKS, seq // (dil * bk))

        def spec(kind, prev, n_sub=n_sub):
            if prev:
                return pl.BlockSpec((1, 1, bk, hd), lambda b, r, n: (b, r, jnp.maximum(n * n_sub - 1, 0), kind))
            return pl.BlockSpec((1, 1, n_sub * bk, hd), lambda b, r, n: (b, r, n, kind))

        out_specs = [pl.BlockSpec((1, 1, n_sub * bk, width), lambda b, r, n: (b, r, n, 0)) for width in (hd, LANES)]
        o, lse = pl.pallas_call(
            functools.partial(_dil_attn_kernel, n_sub=n_sub),
            grid=(batch, dil, seq // (dil * bk * n_sub)),
            in_specs=[spec(0, False), spec(1, True), spec(1, False), spec(2, True), spec(2, False)],
            out_specs=out_specs,
            out_shape=[jax.ShapeDtypeStruct((batch, dil, seq // dil, hd), BF16),
                       jax.ShapeDtypeStruct((batch, dil, seq // dil, LANES), F32)],
            compiler_params=_params("parallel", "parallel", "arbitrary"),
            name=f"dil_attn_d{dil}",
        )(qkv, qkv, qkv, qkv, qkv)
        outs.append(o)
        lses.append(lse)

    def res_specs(width):
        return [pl.BlockSpec((1, dil, tm // dil, width), lambda i: (i // tps, 0, i % tps, 0)) for _, dil in DIL_GROUPS]

    expand = (jnp.arange(LANES)[:, None] == DIL_LSE_LANES * (jnp.arange(hd) // DIL_HEAD_DIM)[None, :]).astype(BF16)
    n_strided = sum(dil > 1 for _, dil in DIL_GROUPS)
    bufs = [pltpu.VMEM((hd // LANES, tm, LANES), F32)] * n_strided + [pltpu.VMEM((1, tm, LANES), F32)] * n_strided
    return pl.pallas_call(
        _dil_out_kernel,
        grid=(t // tm,),
        in_specs=res_specs(hd) + res_specs(LANES) + [_resident((LANES, hd)), _resident((hd, d)), _rows(tm, d),
                                                     _resident((1, d)), _resident((1, d))],
        out_specs=[_rows(tm, d), _rows(tm, d)],
        out_shape=[jax.ShapeDtypeStruct((t, d), F32), jax.ShapeDtypeStruct((t, d), BF16)],
        scratch_shapes=bufs,
        compiler_params=_params("parallel"),
        name="dil_out",
    )(*outs, *lses, expand, w_o.astype(BF16), x, ln_g.reshape(1, d), ln_b.reshape(1, d))


def _rwkv_pre_kernel(x_ref, halo_ref, mu_ref, wrkv_ref, w0_ref, w1_ref, w2_ref, a0_ref, a1_ref, a2_ref,
                     g1_ref, g2_ref, kk_ref, ka_ref, bd_ref,
                     r_o, k_o, v_o, kkn_o, a_o, lw_o, g_o, *, tiles_per_seq):
    first = (pl.program_id(0) % tiles_per_seq) == 0
    x = x_ref[...]
    tm = x.shape[0]
    prev_row = jnp.where(first, 0.0, halo_ref[7:8, :])
    row = lax.broadcasted_iota(jnp.int32, (tm, 1), 0)
    xx = jnp.where(row == 0, prev_row, pltpu.roll(x, 1, axis=0)) - x
    mu = mu_ref[...]

    mixes = [(x + xx * mu[j:j + 1, :]).astype(BF16) for j in range(6)]
    down_w, down_a, down_g = (_dot(mixes[j], w_ref[...]) for j, w_ref in ((3, w1_ref), (4, a1_ref), (5, g1_ref)))
    r = _dot(mixes[0], wrkv_ref[0])
    k = _dot(mixes[1], wrkv_ref[1])
    v = _dot(mixes[2], wrkv_ref[2])
    z = w0_ref[...] + _dot(jnp.tanh(down_w).astype(BF16), w2_ref[...])
    nz = -z
    softplus = jnp.maximum(nz, 0.0) + jnp.log(1.0 + jnp.exp(-jnp.abs(nz)))
    lw = -jnp.exp(-softplus - 0.5)
    a = jax.nn.sigmoid(a0_ref[...] + _dot(down_a.astype(BF16), a2_ref[...]))
    g = _dot(jax.nn.sigmoid(down_g).astype(BF16), g2_ref[...])
    kk = k * kk_ref[...]
    norm = jnp.sqrt(_head_sums(kk * kk, bd_ref[...]))
    r_o[...] = r
    k_o[...] = k * (1.0 + (a - 1.0) * ka_ref[...])
    v_o[...] = v.astype(v_o.dtype)
    kkn_o[...] = kk / jnp.maximum(norm, 1e-12)
    a_o[...] = a
    lw_o[...] = lw
    g_o[...] = g.astype(g_o.dtype)


def _rwkv_chunk_kernel(r_ref, k_ref, v_ref, kk_ref, a_ref, lw_ref, o_ref, s_ref, *, n_pairs, n_chunks):
    c = RWKV_CHUNK

    @pl.when(pl.program_id(2) == 0)
    def _():
        s_ref[...] = jnp.zeros_like(s_ref)

    row = lax.broadcasted_iota(jnp.int32, (c, LANES), 0)
    col = lax.broadcasted_iota(jnp.int32, (c, LANES), 1) % RWKV_HEAD
    blk = {}
    size = RWKV_INV_BASE
    while size < c:
        blk[size] = (row // size) == (col // size)
        size *= 2
    strict = col < row
    incl = col <= row
    eye = jnp.where(row == col, 1.0, 0.0).astype(F32)
    same_head = (lax.broadcasted_iota(jnp.int32, (LANES, LANES), 0) // RWKV_HEAD
                 == lax.broadcasted_iota(jnp.int32, (LANES, LANES), 1) // RWKV_HEAD)
    tri = jnp.where(lax.broadcasted_iota(jnp.int32, (c, c), 1) <= lax.broadcasted_iota(jnp.int32, (c, c), 0),
                    1.0, 0.0).astype(BF16)
    lane_lo = lax.broadcasted_iota(jnp.int32, (1, LANES), 1) < RWKV_HEAD

    def stack(x):
        x = x.astype(BF16)
        zero = jnp.zeros_like(x)
        return jnp.concatenate([jnp.where(lane_lo, x, zero), jnp.where(lane_lo, zero, x)], axis=0)

    def bf(x):
        return x.astype(BF16)

    streams = [(ch, p) for ch in range(n_chunks) for p in range(n_pairs)]

    def load(ref):
        return [ref[ch * c:(ch + 1) * c, p * LANES:(p + 1) * LANES] for ch, p in streams]

    rc, kc, vc, kkc, ac, lwc = (load(ref) for ref in (r_ref, k_ref, v_ref, kk_ref, a_ref, lw_ref))

    cum = []
    for x in lwc:
        cs = _dot(tri, jnp.concatenate(_split_bf16(x, 3), axis=1))
        cum.append(cs[:, :LANES] + cs[:, LANES:2 * LANES] + cs[:, 2 * LANES:])
    p_incl = [jnp.exp(x) for x in cum]
    p_inv = [jnp.exp(-x) for x in cum]
    p_end = [x[c - 1:c, :] for x in p_incl]
    at = [bf(-kk * jnp.exp(cm - lw)) for kk, cm, lw in zip(kkc, cum, lwc)]
    bt = [bf(kk * a * pi) for kk, a, pi in zip(kkc, ac, p_inv)]
    kt = [bf(k * pi) for k, pi in zip(kc, p_inv)]
    rt = [r * pf for r, pf in zip(rc, p_incl)]
    v_st = [stack(x) for x in vc]

    prod = [_dot_nt(jnp.concatenate([a, bf(r)], axis=0), jnp.concatenate([stack(b), stack(k)], axis=0))
            for a, r, b, k in zip(at, rt, bt, kt)]
    l_ab = [jnp.where(strict, x[:c, :LANES], 0.0) for x in prod]
    l_ak = [bf(jnp.where(strict, x[:c, LANES:], 0.0)) for x in prod]
    m_rb = [bf(jnp.where(incl, x[c:, :LANES], 0.0)) for x in prod]
    m_rk = [bf(jnp.where(incl, x[c:, LANES:], 0.0)) for x in prod]

    dg = [bf(jnp.where(blk[RWKV_INV_BASE], x, 0.0)) for x in l_ab]
    d2 = [bf(_dot(x, stack(x))) for x in dg]
    d4 = [_dot(x, stack(x)) for x in d2]
    tinv = [eye + x.astype(F32) for x in dg]
    tinv = [t + _dot(bf(t), stack(y)) for t, y in zip(tinv, d2)]
    tinv = [t + _dot(bf(t), stack(y)) for t, y in zip(tinv, d4)]
    size = RWKV_INV_BASE
    while size < c:
        joins = jnp.logical_not(blk[size]) if 2 * size == c else blk[2 * size] & jnp.logical_not(blk[size])
        off = [jnp.where(joins, x, 0.0) for x in l_ab]
        te = [_dot(bf(t), stack(e)) for t, e in zip(tinv, off)]
        tinv = [t + _dot(bf(x), stack(t)) for t, x in zip(tinv, te)]
        size *= 2

    lakv = [_dot(x, v) for x, v in zip(l_ak, v_st)]
    gu = [_dot(bf(t), jnp.concatenate([stack(a), stack(y)], axis=1)) for t, a, y in zip(tinv, at, lakv)]
    g_nat = [x[:, :LANES] for x in gu]
    uv_nat = [x[:, LANES:] for x in gu]
    mg = [_dot(m, jnp.concatenate([stack(g), stack(u)], axis=1)) for m, g, u in zip(m_rb, g_nat, uv_nat)]
    rg = [bf(r + x[:, :LANES]) for r, x in zip(rt, mg)]
    o_v = [x[:, LANES:] + _dot(m, v) for x, m, v in zip(mg, m_rk, v_st)]
    q_mat = [bf(jnp.where(same_head, _dot(bf(g.T), b), 0.0) * pe) for g, b, pe in zip(g_nat, bt, p_end)]
    n_mat = [jnp.where(same_head, _dot(bf(jnp.concatenate([u, v.astype(F32)], axis=0).T),
                                       jnp.concatenate([b, k], axis=0)), 0.0) * pe
             for u, v, b, k, pe in zip(uv_nat, vc, bt, kt, p_end)]

    state = [s_ref[p] for p in range(n_pairs)]
    for i, (ch, p) in enumerate(streams):
        sb = bf(state[p])
        o_ref[ch * c:(ch + 1) * c, p * LANES:(p + 1) * LANES] = _dot_nt(rg[i], sb) + o_v[i]
        state[p] = state[p] * p_end[i] + _dot(sb, q_mat[i]) + n_mat[i]
    for p in range(n_pairs):
        s_ref[p] = state[p]


def _rwkv_post_kernel(o_ref, r_ref, k_ref, v_ref, g_ref, x_ref, lnw_ref, lnb_ref, rk_ref, bd_ref, w_ref,
                      gl_ref, bl_ref, xo_ref, xbo_ref):
    bd = bd_ref[...]
    o = o_ref[...]
    mean = _head_sums(o, bd) * (1.0 / RWKV_HEAD)
    dlt = o - mean
    var = _head_sums(dlt * dlt, bd) * (1.0 / RWKV_HEAD)
    y = dlt * lax.rsqrt(var + RWKV_GN_EPS) * lnw_ref[...] + lnb_ref[...]
    bonus = _head_sums(r_ref[...] * k_ref[...] * rk_ref[...], bd) * v_ref[...]
    out = ((y + bonus) * g_ref[...]).astype(BF16)
    res = _post_ln(x_ref[...], _dot(out, w_ref[...]), gl_ref[...], bl_ref[...])
    xo_ref[...] = res
    xbo_ref[...] = res.astype(BF16)


def _rwkv_layer(x, batch, mu, w_rkv, w0, w1, w2, a0, a1, a2, g1, g2, k_k, k_a, r_k, ln_w, ln_b, w_o,
                ln_g_post, ln_b_post, tm):
    t, d = x.shape
    seq = t // batch
    tmr = min(256, seq)
    vec = lambda p: p.reshape(1, d)
    idx = jnp.arange(LANES) // RWKV_HEAD
    ones_bd = (idx[:, None] == idx[None, :]).astype(BF16)
    tok = [jax.ShapeDtypeStruct((t, d), dt) for dt in (F32, F32, BF16, F32, F32, F32, BF16)]
    halo = pl.BlockSpec((8, d), lambda i: (jnp.maximum(i * (tmr // 8) - 1, 0), 0))
    r, k, v, kkn, a, lw, g = pl.pallas_call(
        functools.partial(_rwkv_pre_kernel, tiles_per_seq=seq // tmr),
        grid=(t // tmr,),
        in_specs=[_rows(tmr, d), halo, _resident((6, d)), _resident((3, d, d)),
                  _resident((1, d)), _resident(w1.shape), _resident(w2.shape),
                  _resident((1, d)), _resident(a1.shape), _resident(a2.shape),
                  _resident(g1.shape), _resident(g2.shape), _resident((1, d)), _resident((1, d)),
                  _resident((LANES, LANES))],
        out_specs=[_rows(tmr, d)] * 7,
        out_shape=tok,
        compiler_params=_params("parallel"),
        name="rwkv_pre",
    )(x, x, mu, w_rkv.astype(BF16), vec(w0), w1.astype(BF16), w2.astype(BF16), vec(a0), a1.astype(BF16),
      a2.astype(BF16), g1.astype(BF16), g2.astype(BF16), vec(k_k), vec(k_a), ones_bd)

    tc = min(RWKV_STEP_CHUNKS * RWKV_CHUNK, seq)
    nt = seq // tc
    width = RWKV_STEP_PAIRS * LANES
    pair = pl.BlockSpec((tc, width), lambda b, p, i: (b * nt + i, p))
    o = pl.pallas_call(
        functools.partial(_rwkv_chunk_kernel, n_pairs=RWKV_STEP_PAIRS, n_chunks=tc // RWKV_CHUNK),
        grid=(batch, d // width, nt),
        in_specs=[pair] * 6,
        out_specs=pair,
        out_shape=jax.ShapeDtypeStruct((t, d), F32),
        scratch_shapes=[pltpu.VMEM((RWKV_STEP_PAIRS, LANES, LANES), F32)],
        compiler_params=_params("parallel", "parallel", "arbitrary"),
        name="rwkv_chunk",
    )(r, k, v, kkn, a, lw)

    return pl.pallas_call(
        _rwkv_post_kernel,
        grid=(t // tm,),
        in_specs=[_rows(tm, d)] * 6 + [_resident((1, d))] * 3 + [_resident((LANES, LANES)), _resident((d, d)),
                                                                _resident((1, d)), _resident((1, d))],
        out_specs=[_rows(tm, d), _rows(tm, d)],
        out_shape=[jax.ShapeDtypeStruct((t, d), F32), jax.ShapeDtypeStruct((t, d), BF16)],
        compiler_params=_params("parallel"),
        name="rwkv_post",
    )(o, r, k, v, g, x, vec(ln_w), vec(ln_b), vec(r_k), ones_bd, w_o.astype(BF16), vec(ln_g_post), vec(ln_b_post))


def _ffn_core(x1, xe, wa_ref, wb_ref, cw_ref, cb_ref, wo_ref, g_ref, b_ref, xo_ref, xbo_ref, act_ref,
              *, n_chunks, halo_rows):
    tm = x1.shape[0]
    xb = xe[halo_rows:, :]

    def up(c):
        return _dot(xe, wa_ref[c]), _dot(xb, wb_ref[c])

    def down(c, ha, hb):
        cw = cw_ref[c]
        conv = (cw[0:1, :] * ha[halo_rows - 2:halo_rows - 2 + tm, :]
                + cw[1:2, :] * ha[halo_rows - 1:halo_rows - 1 + tm, :]
                + cw[2:3, :] * ha[halo_rows:, :] + cb_ref[c])
        act = conv * jax.nn.sigmoid(conv) * hb
        act_ref[:, c * FF_CHUNK:(c + 1) * FF_CHUNK] = act.astype(BF16)

    ahead = 2
    pending = {c: up(c) for c in range(min(ahead, n_chunks))}
    for c in range(n_chunks):
        if c + ahead < n_chunks:
            pending[c + ahead] = up(c + ahead)
        down(c, *pending.pop(c))
    y = _post_ln(x1, _dot(act_ref[...], wo_ref[...]), g_ref[...], b_ref[...])
    xo_ref[...] = y
    xbo_ref[...] = y.astype(BF16)


def _ffn_kernel(x_ref, xb_ref, halo_ref, *rest, tiles_per_seq, n_chunks, halo_rows):
    first = (pl.program_id(0) % tiles_per_seq) == 0
    halo = halo_ref[...]
    halo = jnp.where(first, jnp.zeros_like(halo), halo)
    xe = jnp.concatenate([halo, xb_ref[...]], axis=0)
    _ffn_core(x_ref[...], xe, *rest, n_chunks=n_chunks, halo_rows=halo_rows)


def _proj_ffn_kernel(a_ref, ahalo_ref, x_ref, xhalo_ref, wp_ref, g1_ref, b1_ref, *rest,
                     tiles_per_seq, n_chunks, halo_rows):
    first = (pl.program_id(0) % tiles_per_seq) == 0
    a_ext = jnp.concatenate([ahalo_ref[...], a_ref[...]], axis=0)
    x_ext = jnp.concatenate([xhalo_ref[...], x_ref[...]], axis=0)
    x1_ext = _post_ln(x_ext, _dot(a_ext, wp_ref[...]), g1_ref[...], b1_ref[...])
    row = lax.broadcasted_iota(jnp.int32, (x1_ext.shape[0], 1), 0)
    x1_ext = jnp.where(first & (row < halo_rows), 0.0, x1_ext)
    _ffn_core(x1_ext[halo_rows:, :], x1_ext.astype(BF16), *rest, n_chunks=n_chunks, halo_rows=halo_rows)


def _ffn_layer(x, xb, batch, w_in, conv_w, conv_b, w_out, ln_g, ln_b, tm, pre=None):
    t, d = x.shape
    seq = t // batch
    nch = D_FF_PAD // FF_CHUNK
    padc = D_FF_PAD - D_FF
    chunks = lambda w: jnp.pad(w, ((0, 0), (0, padc))).reshape(w.shape[0], nch, FF_CHUNK).transpose(1, 0, 2)
    wa = chunks(w_in[:, :D_FF]).astype(BF16)
    wb = chunks(w_in[:, D_FF:]).astype(BF16)
    cw = chunks(conv_w)
    cb = chunks(conv_b.reshape(1, D_FF))
    wo = jnp.pad(w_out, ((0, padc), (0, 0))).astype(BF16)
    halo_rows = 16
    halo = lambda width: pl.BlockSpec((halo_rows, width), lambda i: (jnp.maximum(i * (tm // halo_rows) - 1, 0), 0))
    ffn_specs = [_resident(wa.shape), _resident(wb.shape), _resident(cw.shape), _resident(cb.shape),
                 _resident(wo.shape), _resident((1, d)), _resident((1, d))]
    ffn_args = (wa, wb, cw, cb, wo, ln_g.reshape(1, d), ln_b.reshape(1, d))
    static = dict(tiles_per_seq=seq // tm, n_chunks=nch, halo_rows=halo_rows)
    if pre is None:
        body = functools.partial(_ffn_kernel, **static)
        in_specs = [_rows(tm, d), _rows(tm, d), halo(d)] + ffn_specs
        args = (x, xb, xb) + ffn_args
    else:
        a, w, g1, b1 = pre
        body = functools.partial(_proj_ffn_kernel, **static)
        in_specs = [_rows(tm, a.shape[1]), halo(a.shape[1]), _rows(tm, d), halo(d), _resident(w.shape),
                    _resident((1, d)), _resident((1, d))] + ffn_specs
        args = (a, a, x, x, w, g1.reshape(1, d), b1.reshape(1, d)) + ffn_args
    return pl.pallas_call(
        body,
        grid=(t // tm,),
        in_specs=in_specs,
        out_specs=[_rows(tm, d), _rows(tm, d)],
        out_shape=[jax.ShapeDtypeStruct((t, d), F32), jax.ShapeDtypeStruct((t, d), BF16)],
        scratch_shapes=[pltpu.VMEM((tm, D_FF_PAD), BF16)],
        compiler_params=_params("parallel"),
        name="conv_ffn" if pre is None else "proj_conv_ffn",
    )(*args)


def kernel(x, positions, ln_g, ln_b, mla_w_down, mla_q_norm, mla_kv_norm, mla_w_uq, mla_w_ukv, mla_w_o,
           dil_w_qkv, dil_w_o, rwkv_mu, rwkv_w_rkv, rwkv_w0, rwkv_w1, rwkv_w2, rwkv_a0, rwkv_a1, rwkv_a2,
           rwkv_g1, rwkv_g2, rwkv_k_k, rwkv_k_a, rwkv_r_k, rwkv_ln_w, rwkv_ln_b, rwkv_w_o,
           ffn_w_in, ffn_conv_w, ffn_conv_b, ffn_w_out):
    batch, seq, d = x.shape
    t = batch * seq
    tm = min(512, seq)
    xf = x.reshape(t, d)
    xb = xf
    cos, sin = _rope_tables(positions, tm)
    ia = ib = ic = 0
    for i in range(DEPTH):
        kind = i % 3
        pre = None
        if kind == 0:
            pre = _mla_layer(xf, xb, cos, sin, batch, mla_w_down[ia], mla_q_norm[ia], mla_kv_norm[ia],
                             mla_w_uq[ia], mla_w_ukv[ia], mla_w_o[ia], tm) + (ln_g[i, 0], ln_b[i, 0])
            ia += 1
        elif kind == 1:
            xf, xb = _dil_layer(xf, xb, batch, dil_w_qkv[ib], dil_w_o[ib], ln_g[i, 0], ln_b[i, 0], tm)
            ib += 1
        else:
            xf, xb = _rwkv_layer(xf, batch, rwkv_mu[ic], rwkv_w_rkv[ic], rwkv_w0[ic], rwkv_w1[ic], rwkv_w2[ic],
                                 rwkv_a0[ic], rwkv_a1[ic], rwkv_a2[ic], rwkv_g1[ic], rwkv_g2[ic],
                                 rwkv_k_k[ic], rwkv_k_a[ic], rwkv_r_k[ic], rwkv_ln_w[ic], rwkv_ln_b[ic],
                                 rwkv_w_o[ic], ln_g[i, 0], ln_b[i, 0], tm)
            ic += 1
        xf, xb = _ffn_layer(xf, xb, batch, ffn_w_in[i], ffn_conv_w[i], ffn_conv_b[i], ffn_w_out[i],
                            ln_g[i, 1], ln_b[i, 1], tm, pre=pre)
    return xf.reshape(batch, seq, d)
```
